```python
import jax
import jax.numpy as jnp
from jax import lax
import numpy as np

D_MODEL = 1024
BATCH = 8
SEQ = 2048
DEPTH = 2
DEC_BATCH = 128
DEC_SEQ = 8
PAST_LEN = 2048
PAGE_SIZE = 128

A_HEADS = 8
A_HEAD_DIM = 64
A_WIDTH = A_HEADS * A_HEAD_DIM
Q_BLOCK = 128
M_HEADS = 4
M_HEAD_DIM = 128
M_WIDTH = M_HEADS * M_HEAD_DIM
M_CHUNK = 64
C_WIDTH = 512
CONV_W = 3
D_FF = 2816
PLE_DIM = 256
EPS = 1e-6
SPARE_DIV = 4

IN_COLS = (
    ("a_q", A_WIDTH), ("a_k", A_WIDTH), ("a_v", A_WIDTH), ("a_f", A_HEADS),
    ("m_q", M_WIDTH), ("m_k", M_WIDTH), ("m_v", M_WIDTH), ("m_i", M_HEADS), ("m_f", M_HEADS), ("m_o", M_WIDTH),
    ("c_b", C_WIDTH), ("c_c", C_WIDTH), ("c_x", C_WIDTH),
    ("g_a", D_MODEL), ("g_m", D_MODEL), ("g_c", D_MODEL),
)
P_IN = 3 * A_WIDTH + A_HEADS + 4 * M_WIDTH + 2 * M_HEADS + 3 * C_WIDTH + 3 * D_MODEL

kernel_name = "fox_mlstm_shortconv_hybrid_step"


def _rmsnorm(x, g):
    xf = x.astype(jnp.float32)
    y = xf * lax.rsqrt(jnp.mean(xf * xf, axis=-1, keepdims=True) + EPS)
    return (y * g.astype(jnp.float32)).astype(x.dtype)


def _split_cols(z):
    bounds = np.cumsum([s for _, s in IN_COLS])[:-1].tolist()
    return jnp.split(z, bounds, axis=-1)


def _col_range(name):
    off = 0
    for n, s in IN_COLS:
        if n == name:
            return off, s
        off += s
    raise KeyError(name)


def _causal_dwconv(u, buf, w):
    t = u.shape[1]
    up = jnp.concatenate([buf.astype(u.dtype), u], axis=1)
    y = up[:, 0:t] * w[0]
    for j in range(1, CONV_W):
        y = y + up[:, j:j + t] * w[j]
    return y, up[:, up.shape[1] - (CONV_W - 1):]


def _gather_pages(pool, page_table, l):
    ps = pool.shape[1]
    g = pool[page_table[:, :, None], jnp.arange(ps)[None, None, :], l]
    return g.reshape((g.shape[0], g.shape[1] * ps) + g.shape[3:])


def _fox_prompt(q, k, v, logf):
    nb, t, h, dh = q.shape
    scale = dh ** -0.5
    cum = jnp.cumsum(logf, axis=1).transpose(0, 2, 1)
    n_blk = t // Q_BLOCK
    q_blk = q.reshape(nb, n_blk, Q_BLOCK, h, dh).swapaxes(0, 1)
    c_blk = cum.reshape(nb, h, n_blk, Q_BLOCK).transpose(2, 0, 1, 3)
    key_pos = jnp.arange(t)

    def block(args):
        i, q_i, c_i = args
        q_pos = i * Q_BLOCK + jnp.arange(Q_BLOCK)
        s = jnp.einsum("bqhd,bkhd->bhqk", q_i, k).astype(jnp.float32) * scale
        s = s + c_i[..., :, None] - cum[:, :, None, :]
        s = jnp.where(key_pos[None, :] <= q_pos[:, None], s, -jnp.inf)
        p = jax.nn.softmax(s, axis=-1).astype(v.dtype)
        return jnp.einsum("bhqk,bkhd->bqhd", p, v)

    out = lax.map(block, (jnp.arange(n_blk), q_blk, c_blk))
    return out.swapaxes(0, 1).reshape(nb, t, h * dh)


def _fox_sample(q, k_new, v_new, logf_new, k_past, v_past, logf_past):
    nb, s_len, h, dh = q.shape
    p_len = k_past.shape[1]
    scale = dh ** -0.5
    cum_past = jnp.cumsum(logf_past.astype(jnp.float32), axis=1)
    cum_new = cum_past[:, -1:] + jnp.cumsum(logf_new, axis=1)
    qc = cum_new.transpose(0, 2, 1)[..., None]
    s_past = jnp.einsum("bqhd,bkhd->bhqk", q, k_past.astype(q.dtype)).astype(jnp.float32) * scale
    s_past = s_past + qc - cum_past.transpose(0, 2, 1)[:, :, None, :]
    s_new = jnp.einsum("bqhd,bkhd->bhqk", q, k_new).astype(jnp.float32) * scale
    s_new = s_new + qc - cum_new.transpose(0, 2, 1)[:, :, None, :]
    causal = jnp.tril(jnp.ones((s_len, s_len), bool))
    s_new = jnp.where(causal, s_new, -jnp.inf)
    p = jax.nn.softmax(jnp.concatenate([s_past, s_new], axis=-1), axis=-1).astype(v_new.dtype)
    out = (jnp.einsum("bhqk,bkhd->bqhd", p[..., :p_len], v_past.astype(v_new.dtype))
           + jnp.einsum("bhqk,bkhd->bqhd", p[..., p_len:], v_new))
    return out.reshape(nb, s_len, h * dh)


def _mlstm(q, k, v, i_pre, logf, c0, n0, m0, chunk):
    f32 = jnp.float32
    nb, t, h, dk = q.shape
    dv = v.shape[-1]
    nc = t // chunk
    q = q.astype(f32)
    k = k.astype(f32) * (dk ** -0.5)
    v = v.astype(f32)

    def to_chunks(a):
        return a.reshape((nb, nc, chunk) + a.shape[2:]).swapaxes(0, 1)

    causal = jnp.tril(jnp.ones((chunk, chunk), bool))

    def step(carry, xs):
        c, n, m = carry
        q_c, k_c, v_c, i_c, f_c = xs
        big_f = jnp.cumsum(f_c, axis=1).transpose(0, 2, 1)
        it = i_c.transpose(0, 2, 1)
        d = big_f[..., :, None] - big_f[..., None, :] + it[..., None, :]
        d = jnp.where(causal, d, -jnp.inf)
        inter = m[..., None] + big_f
        m_t = jnp.maximum(inter, jnp.max(d, axis=-1))
        w = jnp.exp(d - m_t[..., None])
        a = jnp.exp(inter - m_t)
        wqk = w * jnp.einsum("blhd,bshd->bhls", q_c, k_c)
        num = (a[..., None] * jnp.einsum("bhvd,blhd->bhlv", c, q_c)
               + jnp.einsum("bhls,bshv->bhlv", wqk, v_c))
        den = a * jnp.einsum("bhd,blhd->bhl", n, q_c) + jnp.sum(wqk, axis=-1)
        h_t = num / jnp.maximum(jnp.abs(den), jnp.exp(-m_t))[..., None]
        w_last = w[..., -1, :]
        a_last = a[..., -1]
        c_new = a_last[..., None, None] * c + jnp.einsum("bhs,bshv,bshd->bhvd", w_last, v_c, k_c)
        n_new = a_last[..., None] * n + jnp.einsum("bhs,bshd->bhd", w_last, k_c)
        return (c_new, n_new, m_t[..., -1]), h_t.transpose(0, 2, 1, 3)

    xs = (to_chunks(q), to_chunks(k), to_chunks(v), to_chunks(i_pre), to_chunks(logf))
    (c1, n1, m1), hs = lax.scan(step, (c0.astype(f32), n0.astype(f32), m0.astype(f32)), xs)
    return hs.swapaxes(0, 1).reshape(nb, t, h, dv), c1, n1, m1


def _layer(x, pe, prm, past):
    (w_in, b_in, g_mix, g_mhead, w_sconv, w_oa, w_om, w_oc, w_o,
     g_ffn, w_up, w_fconv, b_fconv, w_down, g_ple, w_ple_gate, w_ple) = prm
    f32 = jnp.float32
    nb, t, _ = x.shape
    h = _rmsnorm(x, g_mix)
    z = h @ w_in + b_in
    (aq, ak, av, af, mq, mk, mv, mi, mf, mo, cb, cc, cx, ga, gm, gc) = _split_cols(z)
    aq = aq.reshape(nb, t, A_HEADS, A_HEAD_DIM)
    ak = ak.reshape(nb, t, A_HEADS, A_HEAD_DIM)
    av = av.reshape(nb, t, A_HEADS, A_HEAD_DIM)
    a_logf = jax.nn.log_sigmoid(af.astype(f32))
    mq = mq.reshape(nb, t, M_HEADS, M_HEAD_DIM)
    mk = mk.reshape(nb, t, M_HEADS, M_HEAD_DIM)
    mv = mv.reshape(nb, t, M_HEADS, M_HEAD_DIM)
    m_logf = jax.nn.log_sigmoid(mf.astype(f32))
    if past is None:
        ya = _fox_prompt(aq, ak, av, a_logf)
        c0 = jnp.zeros((nb, M_HEADS, M_HEAD_DIM, M_HEAD_DIM), f32)
        n0 = jnp.zeros((nb, M_HEADS, M_HEAD_DIM), f32)
        m0 = jnp.zeros((nb, M_HEADS), f32)
        conv0 = jnp.zeros((nb, CONV_W - 1, C_WIDTH), x.dtype)
        ffn0 = jnp.zeros((nb, CONV_W - 1, D_FF), x.dtype)
        chunk = M_CHUNK
    else:
        k_past, v_past, logf_past, c0, n0, m0, conv0, ffn0 = past
        ya = _fox_sample(aq, ak, av, a_logf, k_past, v_past, logf_past)
        chunk = t
    hm, c1, n1, m1 = _mlstm(mq, mk, mv, mi.astype(f32), m_logf, c0, n0, m0, chunk)
    hm = _rmsnorm(hm, g_mhead.reshape(M_HEADS, M_HEAD_DIM)).reshape(nb, t, M_WIDTH).astype(x.dtype)
    ym = jax.nn.sigmoid(mo) * hm
    uc, conv1 = _causal_dwconv(cc * cx, conv0, w_sconv)
    yc = (cb * uc) @ w_oc
    merged = (jax.nn.sigmoid(ga) * (ya @ w_oa) + jax.nn.sigmoid(gm) * (ym @ w_om)
              + jax.nn.sigmoid(gc) * yc)
    x = x + merged @ w_o
    h2 = _rmsnorm(x, g_ffn)
    ua, ub = jnp.split(h2 @ w_up, 2, axis=-1)
    uac, ffn1 = _causal_dwconv(ua, ffn0, w_fconv)
    x = x + (jax.nn.silu(uac + b_fconv) * ub) @ w_down
    gate = jax.nn.sigmoid(_rmsnorm(x, g_ple) @ w_ple_gate)
    x = x + gate * (pe @ w_ple)
    return x, (ak, av, a_logf, c1, n1, m1, conv1, ffn1)


def _stack_layers(per_layer):
    rows = [jnp.stack([s[i] for s in per_layer], axis=2) for i in range(3)]
    states = [jnp.stack([s[i] for s in per_layer], axis=1) for i in range(3, 8)]
    return tuple(rows + states)


def setup_inputs(seed: int = 0) -> dict:
    key = jax.random.key(seed)
    ks = iter(jax.random.split(key, 48))
    f32 = jnp.float32

    def nrm(shape, scale):
        return scale * jax.random.normal(next(ks), shape, f32)

    n_pages = PAST_LEN // PAGE_SIZE
    n_used = DEC_BATCH * n_pages
    n_phys = n_used + (n_used + SPARE_DIV - 1) // SPARE_DIV
    page_table = jax.random.permutation(next(ks), n_phys)[:n_used].reshape(DEC_BATCH, n_pages).astype(jnp.int32)

    b_in = nrm((DEPTH, P_IN), 0.02)
    o, s = _col_range("a_f")
    b_in = b_in.at[:, o:o + s].add(jnp.linspace(1.0, 4.0, s, dtype=f32))
    o, s = _col_range("m_f")
    b_in = b_in.at[:, o:o + s].add(jnp.linspace(3.0, 6.0, s, dtype=f32))

    return {
        "x_prompt": nrm((BATCH, SEQ, D_MODEL), 1.0),
        "x_sample": nrm((DEC_BATCH, DEC_SEQ, D_MODEL), 1.0),
        "cache_k": nrm((n_phys, PAGE_SIZE, DEPTH, A_HEADS, A_HEAD_DIM), 1.0),
        "cache_v": nrm((n_phys, PAGE_SIZE, DEPTH, A_HEADS, A_HEAD_DIM), 1.0),
        "cache_logf": jax.nn.log_sigmoid(2.5 + nrm((n_phys, PAGE_SIZE, DEPTH, A_HEADS), 1.0)),
        "state_mlstm_C": nrm((DEC_BATCH, DEPTH, M_HEADS, M_HEAD_DIM, M_HEAD_DIM), 0.3),
        "state_mlstm_n": nrm((DEC_BATCH, DEPTH, M_HEADS, M_HEAD_DIM), 0.3),
        "state_mlstm_m": nrm((DEC_BATCH, DEPTH, M_HEADS), 1.0),
        "state_conv": nrm((DEC_BATCH, DEPTH, CONV_W - 1, C_WIDTH), 1.0),
        "state_ffn_conv": nrm((DEC_BATCH, DEPTH, CONV_W - 1, D_FF), 1.0),
        "page_table": page_table,
        "p_prompt": nrm((DEPTH, BATCH, SEQ, PLE_DIM), 1.0),
        "p_sample": nrm((DEPTH, DEC_BATCH, DEC_SEQ, PLE_DIM), 1.0),
        "w_in": nrm((DEPTH, D_MODEL, P_IN), D_MODEL ** -0.5),
        "b_in": b_in,
        "g_mix": 1.0 + nrm((DEPTH, D_MODEL), 0.05),
        "g_mhead": 1.0 + nrm((DEPTH, M_WIDTH), 0.05),
        "w_sconv": nrm((DEPTH, CONV_W, C_WIDTH), CONV_W ** -0.5),
        "w_oa": nrm((DEPTH, A_WIDTH, D_MODEL), A_WIDTH ** -0.5),
        "w_om": nrm((DEPTH, M_WIDTH, D_MODEL), M_WIDTH ** -0.5),
        "w_oc": nrm((DEPTH, C_WIDTH, D_MODEL), C_WIDTH ** -0.5),
        "w_o": nrm((DEPTH, D_MODEL, D_MODEL), D_MODEL ** -0.5),
        "g_ffn": 1.0 + nrm((DEPTH, D_MODEL), 0.05),
        "w_up": nrm((DEPTH, D_MODEL, 2 * D_FF), D_MODEL ** -0.5),
        "w_fconv": nrm((DEPTH, CONV_W, D_FF), CONV_W ** -0.5),
        "b_fconv": nrm((DEPTH, D_FF), 0.02),
        "w_down": nrm((DEPTH, D_FF, D_MODEL), D_FF ** -0.5),
        "g_ple": 1.0 + nrm((DEPTH, D_MODEL), 0.05),
        "w_ple_gate": nrm((DEPTH, D_MODEL, D_MODEL), D_MODEL ** -0.5),
        "w_ple": nrm((DEPTH, PLE_DIM, D_MODEL), PLE_DIM ** -0.5),
        "g_final": 1.0 + nrm((D_MODEL,), 0.05),
    }


def reference(x_prompt, x_sample, cache_k, cache_v, cache_logf, state_mlstm_C, state_mlstm_n,
              state_mlstm_m, state_conv, state_ffn_conv, page_table, p_prompt, p_sample,
              w_in, b_in, g_mix, g_mhead, w_sconv, w_oa, w_om, w_oc, w_o, g_ffn, w_up,
              w_fconv, b_fconv, w_down, g_ple, w_ple_gate, w_ple, g_final):
    xp = x_prompt
    xs = x_sample
    new_p = []
    new_s = []
    for l in range(DEPTH):
        prm = (w_in[l], b_in[l], g_mix[l], g_mhead[l], w_sconv[l], w_oa[l], w_om[l], w_oc[l], w_o[l],
               g_ffn[l], w_up[l], w_fconv[l], b_fconv[l], w_down[l], g_ple[l], w_ple_gate[l], w_ple[l])
        xp, st_p = _layer(xp, p_prompt[l], prm, None)
        past = (_gather_pages(cache_k, page_table, l), _gather_pages(cache_v, page_table, l),
                _gather_pages(cache_logf, page_table, l), state_mlstm_C[:, l], state_mlstm_n[:, l],
                state_mlstm_m[:, l], state_conv[:, l], state_ffn_conv[:, l])
        xs, st_s = _layer(xs, p_sample[l], prm, past)
        new_p.append(st_p)
        new_s.append(st_s)
    y_prompt = _rmsnorm(xp, g_final)
    y_sample = _rmsnorm(xs, g_final)
    k_p, v_p, logf_p, c_p, n_p, m_p, conv_p, ffn_p = _stack_layers(new_p)
    k_s, v_s, logf_s, c_s, n_s, m_s, conv_s, ffn_s = _stack_layers(new_s)
    return (y_prompt, y_sample, k_p, v_p, logf_p, c_p, n_p, m_p, conv_p, ffn_p,
            k_s, v_s, logf_s, c_s, n_s, m_s, conv_s, ffn_s)
```

```python
import functools

import jax
import jax.numpy as jnp
from jax import lax
from jax.experimental import pallas as pl
from jax.experimental.pallas import tpu as pltpu

F32 = jnp.float32
BF16 = jnp.bfloat16

D_MODEL = 1024
A_HEADS = 8
A_HEAD_DIM = 64
A_WIDTH = A_HEADS * A_HEAD_DIM
M_HEADS = 4
M_HEAD_DIM = 128
M_WIDTH = M_HEADS * M_HEAD_DIM
C_WIDTH = 512
D_FF = 2816
EPS = 1e-6
LANES = 128
SUBLANES = 8
VMEM_LIMIT = 56 * 1024 * 1024

SM_AF = 0
SM_MI = 8
SM_MF = 12
SM_USED = 16

NT = (((1,), (1,)), ((), ()))


def _cparams(*sem):
    return pltpu.CompilerParams(dimension_semantics=sem, vmem_limit_bytes=VMEM_LIMIT)


def _const_spec(shape):
    nd = len(shape)
    return pl.BlockSpec(shape, lambda *_: (0,) * nd, pipeline_mode=pl.Buffered(1))


def _rms(x, g):
    return x * lax.rsqrt(jnp.mean(x * x, axis=-1, keepdims=True) + EPS) * g


def _log_sigmoid(x):
    return jnp.minimum(x, 0.0) - jnp.log1p(jnp.exp(-jnp.abs(x)))


def _sigmoid(x):
    return 1.0 / (1.0 + jnp.exp(-x))


def _dot(a, b):
    return jnp.dot(a, b, preferred_element_type=F32)


def _inproj_kernel(x_ref, g_ref, w_ref, b_ref,
                   q_ref, k_ref, v_ref, mq_ref, mk_ref, mv_ref, so_ref, sm_ref):
    h = _rms(x_ref[...], g_ref[...]).astype(BF16)

    def proj(c0, n):
        return _dot(h, w_ref[:, c0:c0 + n]) + b_ref[:, c0:c0 + n]

    q_ref[...] = (proj(0, A_WIDTH) * (A_HEAD_DIM ** -0.5)).astype(BF16)
    k_ref[...] = proj(A_WIDTH, A_WIDTH)
    v_ref[...] = proj(2 * A_WIDTH, A_WIDTH)
    o = 3 * A_WIDTH
    mq_ref[...] = proj(o, M_WIDTH).astype(BF16)
    mk_ref[...] = (proj(o + M_WIDTH, M_WIDTH) * (M_HEAD_DIM ** -0.5)).astype(BF16)
    mv_ref[...] = proj(o + 2 * M_WIDTH, M_WIDTH).astype(BF16)
    so_ref[...] = _sigmoid(proj(o + 3 * M_WIDTH, M_WIDTH))
    sm_ref[...] = proj(o + 4 * M_WIDTH, LANES)


def _inproj(x, g, w, b, tm):
    n = x.shape[0]
    wcols = w.shape[1]
    row = lambda c: pl.BlockSpec((tm, c), lambda i: (i, 0))
    outs = [
        jax.ShapeDtypeStruct((n, A_WIDTH), BF16),
        jax.ShapeDtypeStruct((n, A_WIDTH), F32),
        jax.ShapeDtypeStruct((n, A_WIDTH), F32),
        jax.ShapeDtypeStruct((n, M_WIDTH), BF16),
        jax.ShapeDtypeStruct((n, M_WIDTH), BF16),
        jax.ShapeDtypeStruct((n, M_WIDTH), BF16),
        jax.ShapeDtypeStruct((n, M_WIDTH), F32),
        jax.ShapeDtypeStruct((n, LANES), F32),
    ]
    return pl.pallas_call(
        _inproj_kernel,
        grid=(n // tm,),
        in_specs=[row(D_MODEL), _const_spec((1, D_MODEL)), _const_spec((D_MODEL, wcols)),
                  _const_spec((1, wcols))],
        out_specs=[row(A_WIDTH)] * 3 + [row(M_WIDTH)] * 4 + [row(LANES)],
        out_shape=outs,
        compiler_params=_cparams("parallel"),
        name="inproj",
    )(x, g, w, b)


def _gates_kernel(sm_ref, col_ref, logf_ref, *, seq_a, seg_m):
    x = sm_ref[...]
    rows = x.shape[0]
    lane = lax.broadcasted_iota(jnp.int32, (1, LANES), 1)
    row = lax.broadcasted_iota(jnp.int32, (rows, 1), 0)
    ls = _log_sigmoid(x)
    logf_ref[...] = ls[:, SM_AF:SM_AF + A_HEADS]
    is_a = lane < SM_MI
    is_i = (lane >= SM_MI) & (lane < SM_MF)
    is_f = (lane >= SM_MF) & (lane < SM_USED)
    y = jnp.where(is_i, x, ls)
    rmod_a = row & (seq_a - 1)
    rmod_m = row & (seg_m - 1)
    s = 1
    while s < max(seq_a, seg_m):
        take = jnp.zeros((rows, LANES), jnp.bool_)
        if s < seq_a:
            take = take | (is_a & (rmod_a >= s))
        if s < seg_m:
            take = take | (is_f & (rmod_m >= s))
        y = y + jnp.where(take, pltpu.roll(y, s, axis=0), 0.0)
        s *= 2
    col_ref[...] = y


def _gates(sm, rows, seq_a, seg_m):
    n = sm.shape[0]
    return pl.pallas_call(
        functools.partial(_gates_kernel, seq_a=seq_a, seg_m=seg_m),
        grid=(n // rows,),
        in_specs=[pl.BlockSpec((rows, LANES), lambda i: (i, 0))],
        out_specs=[pl.BlockSpec((rows, LANES), lambda i: (i, 0)),
                   pl.BlockSpec((rows, A_HEADS), lambda i: (i, 0))],
        out_shape=[jax.ShapeDtypeStruct((n, LANES), F32),
                   jax.ShapeDtypeStruct((n, A_HEADS), F32)],
        compiler_params=_cparams("parallel"),
        name="gates",
    )(sm)


def _fox_kernel(q_ref, k_ref, v_ref, col_ref, row_ref, o_ref, *, tq):
    hp = pl.program_id(1)
    qi = pl.program_id(2)
    lane = lax.broadcasted_iota(jnp.int32, (1, LANES), 1)
    q2 = q_ref[...]
    colv = col_ref[...]
    zero = jnp.zeros_like(q2)
    qm, ci = [], []
    for hh in range(2):
        h = hp * 2 + hh
        qm.append(jnp.where((lane >= A_HEAD_DIM) == bool(hh), q2, zero))
        ci.append(jnp.sum(jnp.where(lane == SM_AF + h, colv, 0.0), axis=1, keepdims=True))

    def step(j, carry, diag):
        start = pl.multiple_of(j * tq, tq)
        kb = k_ref[pl.ds(start, tq), :].astype(BF16)
        vb = v_ref[pl.ds(start, tq), :].astype(BF16)
        out = []
        for hh in range(2):
            m, l, acc = carry[hh]
            cj = row_ref[pl.ds(SM_AF + hp * 2 + hh, 1), pl.ds(start, tq)]
            s = lax.dot_general(qm[hh], kb, NT, preferred_element_type=F32) + (ci[hh] - cj)
            if diag:
                r = lax.broadcasted_iota(jnp.int32, (tq, tq), 0)
                c = lax.broadcasted_iota(jnp.int32, (tq, tq), 1)
                s = jnp.where(c <= r, s, -jnp.inf)
            m_new = jnp.maximum(m, jnp.max(s, axis=1, keepdims=True))
            alpha = jnp.exp(m - m_new)
            p = jnp.exp(s - m_new)
            l = alpha * l + jnp.sum(p, axis=1, keepdims=True)
            acc = alpha * acc + _dot(p.astype(BF16), vb)
            out.append((m_new, l, acc))
        return tuple(out)

    init = tuple((jnp.full((tq, 1), -jnp.inf, F32), jnp.zeros((tq, 1), F32),
                  jnp.zeros((tq, LANES), F32)) for _ in range(2))
    carry = lax.fori_loop(0, qi, lambda j, c: step(j, c, False), init)
    carry = step(qi, carry, True)
    y0 = carry[0][2] / carry[0][1]
    y1 = carry[1][2] / carry[1][1]
    o_ref[...] = jnp.where(lane < A_HEAD_DIM, y0, y1).astype(o_ref.dtype)


def _fox_prompt(q, k, v, col, rowt, nb, t, tq):
    n = q.shape[0]
    nq = t // tq
    pairs = A_WIDTH // LANES
    return pl.pallas_call(
        functools.partial(_fox_kernel, tq=tq),
        grid=(nb, pairs, nq),
        in_specs=[
            pl.BlockSpec((tq, LANES), lambda b, hp, qi: (b * nq + qi, hp)),
            pl.BlockSpec((t, LANES), lambda b, hp, qi: (b, hp)),
            pl.BlockSpec((t, LANES), lambda b, hp, qi: (b, hp)),
            pl.BlockSpec((tq, LANES), lambda b, hp, qi: (b * nq + qi, 0)),
            pl.BlockSpec((None, SM_USED, t), lambda b, hp, qi: (b, 0, 0)),
        ],
        out_specs=pl.BlockSpec((tq, LANES), lambda b, hp, qi: (b * nq + qi, hp)),
        out_shape=jax.ShapeDtypeStruct((n, A_WIDTH), BF16),
        compiler_params=_cparams("parallel", "parallel", "arbitrary"),
        name="fox_prompt",
    )(q, k, v, col, rowt)


def _pool_scan_kernel(x_ref, o_ref):
    y = x_ref[...]
    lane = lax.broadcasted_iota(jnp.int32, (1, LANES), 1)
    s = 1
    while s < LANES:
        y = y + jnp.where(lane >= s, pltpu.roll(y, s, axis=1), 0.0)
        s *= 2
    o_ref[...] = y


def _pool_scan(x, rows):
    n = x.shape[0]
    return pl.pallas_call(
        _pool_scan_kernel,
        grid=(n // rows,),
        in_specs=[pl.BlockSpec((rows, LANES), lambda i: (i, 0))],
        out_specs=pl.BlockSpec((rows, LANES), lambda i: (i, 0)),
        out_shape=jax.ShapeDtypeStruct((n, LANES), F32),
        compiler_params=_cparams("parallel"),
        name="pool_scan",
    )(x)


def _fox_sample_kernel(pt_ref, q_ref, kn_ref, vn_ref, col_ref, rown_ref, *rest,
                       n_pages, layer, s_len):
    del pt_ref
    k_refs = rest[:n_pages]
    v_refs = rest[n_pages:2 * n_pages]
    lc_refs = rest[2 * n_pages:3 * n_pages]
    o_ref = rest[3 * n_pages]
    s_ref = rest[3 * n_pages + 1]
    nh = A_HEADS
    rows = nh * s_len
    lane_w = lax.broadcasted_iota(jnp.int32, (1, A_WIDTH), 1)
    rid = lax.broadcasted_iota(jnp.int32, (rows, 1), 0)
    own = jnp.right_shift(lane_w, 6) == jnp.right_shift(rid, 3)
    q = q_ref[...].astype(F32)
    qbd = jnp.where(own, jnp.concatenate([q] * nh, axis=0), 0.0)
    colv = col_ref[...]
    rown = rown_ref[...]
    lo = layer * nh

    tot = [lc_refs[p][lo:lo + nh, LANES - 1:LANES] for p in range(n_pages)]
    base = [None] * n_pages
    suf = jnp.zeros((nh, 1), F32)
    for p in range(n_pages - 1, -1, -1):
        suf = suf + tot[p]
        base[p] = suf

    m_run = [None] * nh
    for p in range(n_pages):
        sp = lax.dot_general(qbd, k_refs[p][...], NT, preferred_element_type=F32)
        for h in range(nh):
            sl = slice(h * s_len, (h + 1) * s_len)
            cn = colv[:, SM_AF + h:SM_AF + h + 1]
            bias = (base[p][h:h + 1, :] - lc_refs[p][lo + h:lo + h + 1, :]) + cn
            sh = sp[sl, :] + bias
            s_ref[sl, p * LANES:(p + 1) * LANES] = sh
            mx = jnp.max(sh, axis=1, keepdims=True)
            m_run[h] = mx if m_run[h] is None else jnp.maximum(m_run[h], mx)
    sn = lax.dot_general(qbd, kn_ref[...], NT, preferred_element_type=F32)
    r = lax.broadcasted_iota(jnp.int32, (s_len, s_len), 0)
    c = lax.broadcasted_iota(jnp.int32, (s_len, s_len), 1)
    pn, m_all = [], []
    for h in range(nh):
        sl = slice(h * s_len, (h + 1) * s_len)
        cn = colv[:, SM_AF + h:SM_AF + h + 1]
        sh = sn[sl, :] + (cn - rown[SM_AF + h:SM_AF + h + 1, :])
        sh = jnp.where(c <= r, sh, -jnp.inf)
        pn.append(sh)
        m_all.append(jnp.maximum(m_run[h], jnp.max(sh, axis=1, keepdims=True)))
    m = jnp.concatenate(m_all, axis=0)
    p_new = jnp.exp(jnp.concatenate(pn, axis=0) - m)
    l = jnp.sum(p_new, axis=1, keepdims=True)
    acc = _dot(p_new, vn_ref[...])
    for p in range(n_pages):
        pp = jnp.exp(s_ref[:, p * LANES:(p + 1) * LANES] - m)
        l = l + jnp.sum(pp, axis=1, keepdims=True)
        acc = acc + _dot(pp, v_refs[p][...])
    acc = jnp.where(own, acc / l, 0.0)
    y = acc[0:s_len, :]
    for h in range(1, nh):
        y = y + acc[h * s_len:(h + 1) * s_len, :]
    o_ref[...] = y.astype(o_ref.dtype)


def _fox_sample(page_table, q, k_new, v_new, col, rown, cache_k, cache_v, lc, layer, nb, s_len):
    n_pages = page_table.shape[1]
    n_phys, page = cache_k.shape[0], cache_k.shape[1]
    depth = cache_k.shape[2]
    ck = cache_k.reshape(n_phys, page, depth * A_WIDTH)
    cv = cache_v.reshape(n_phys, page, depth * A_WIDTH)
    tok = lambda c: pl.BlockSpec((s_len, c), lambda b, pt: (b, 0))
    page_specs = [pl.BlockSpec((None, page, A_WIDTH),
                               lambda b, pt, j=j: (pt[b, j], 0, layer)) for j in range(n_pages)]
    lc_specs = [pl.BlockSpec((None, 2 * A_HEADS, LANES),
                             lambda b, pt, j=j: (pt[b, j], 0, 0)) for j in range(n_pages)]
    grid_spec = pltpu.PrefetchScalarGridSpec(
        num_scalar_prefetch=1,
        grid=(nb,),
        in_specs=[tok(A_WIDTH), tok(A_WIDTH), tok(A_WIDTH), tok(LANES),
                  pl.BlockSpec((None, SM_USED, s_len), lambda b, pt: (b, 0, 0))]
                 + page_specs + page_specs + lc_specs,
        out_specs=tok(A_WIDTH),
        scratch_shapes=[pltpu.VMEM((A_HEADS * s_len, n_pages * LANES), F32)],
    )
    return pl.pallas_call(
        functools.partial(_fox_sample_kernel, n_pages=n_pages, layer=layer, s_len=s_len),
        grid_spec=grid_spec,
        out_shape=jax.ShapeDtypeStruct((nb * s_len, A_WIDTH), BF16),
        compiler_params=_cparams("arbitrary"),
        name="fox_sample",
    )(page_table, q, k_new, v_new, col, rown, *([ck] * n_pages), *([cv] * n_pages), *([lc] * n_pages))


def _mlstm_kernel(q_ref, k_ref, v_ref, so_ref, col_ref, row_ref, g_ref, c0_ref, n0_ref, m0_ref,
                  y_ref, c1_ref, n1_ref, m1_ref, *, t, chunk):
    hd = pl.program_id(1)
    lane = lax.broadcasted_iota(jnp.int32, (1, LANES), 1)
    r = lax.broadcasted_iota(jnp.int32, (chunk, chunk), 0)
    c = lax.broadcasted_iota(jnp.int32, (chunk, chunk), 1)
    causal = c <= r
    g = g_ref[...]

    def body(ci, carry):
        cst, nst, mst = carry
        r0 = pl.multiple_of(ci * chunk, chunk)
        q = q_ref[pl.ds(r0, chunk), :].astype(F32)
        k = k_ref[pl.ds(r0, chunk), :].astype(F32)
        v = v_ref[pl.ds(r0, chunk), :].astype(F32)
        colv = col_ref[pl.ds(r0, chunk), :]
        f_col = jnp.sum(jnp.where(lane == SM_MF + hd, colv, 0.0), axis=1, keepdims=True)
        i_col = jnp.sum(jnp.where(lane == SM_MI + hd, colv, 0.0), axis=1, keepdims=True)
        f_row = row_ref[pl.ds(SM_MF + hd, 1), pl.ds(r0, chunk)]
        i_row = row_ref[pl.ds(SM_MI + hd, 1), pl.ds(r0, chunk)]
        d = jnp.where(causal, f_col + (i_row - f_row), -jnp.inf)
        inter = mst + f_col
        m_t = jnp.maximum(inter, jnp.max(d, axis=1, keepdims=True))
        w = jnp.exp(d - m_t)
        a = jnp.exp(inter - m_t)
        wqk = w * lax.dot_general(q, k, NT, preferred_element_type=F32)
        num = a * lax.dot_general(q, cst, NT, preferred_element_type=F32) + _dot(wqk, v)
        den = a * jnp.sum(q * nst, axis=1, keepdims=True) + jnp.sum(wqk, axis=1, keepdims=True)
        h_t = num / jnp.maximum(jnp.abs(den), jnp.exp(-m_t))
        hn = h_t * lax.rsqrt(jnp.mean(h_t * h_t, axis=1, keepdims=True) + EPS) * g
        y_ref[pl.ds(r0, chunk), :] = (so_ref[pl.ds(r0, chunk), :] * hn).astype(y_ref.dtype)
        m_last = m_t[chunk - 1:chunk, :]
        f_last = f_col[chunk - 1:chunk, :]
        a_last = a[chunk - 1:chunk, :]
        w_last = jnp.exp((i_col - f_col) + (f_last - m_last))
        vw = v * w_last
        c_new = a_last * cst + _dot(vw.T, k)
        n_new = a_last * nst + jnp.sum(k * w_last, axis=0, keepdims=True)
        return c_new, n_new, m_last

    cst, nst, mst = lax.fori_loop(0, t // chunk, body, (c0_ref[...], n0_ref[...], m0_ref[...]))
    c1_ref[...] = cst
    n1_ref[...] = nst
    m1_ref[...] = mst


def _mlstm(q, k, v, so, col, rowt, g, c0, n0, m0, nb, t, chunk):
    n = q.shape[0]
    hd = M_HEAD_DIM
    seq = lambda: pl.BlockSpec((t, hd), lambda b, h: (b, h))
    st = lambda a, b_: pl.BlockSpec((None, None, a, b_), lambda b, h: (b, h, 0, 0))
    return pl.pallas_call(
        functools.partial(_mlstm_kernel, t=t, chunk=chunk),
        grid=(nb, M_HEADS),
        in_specs=[seq(), seq(), seq(), seq(),
                  pl.BlockSpec((t, LANES), lambda b, h: (b, 0)),
                  pl.BlockSpec((None, SM_USED, t), lambda b, h: (b, 0, 0)),
                  pl.BlockSpec((1, hd), lambda b, h: (0, h)),
                  st(hd, hd), st(1, hd), st(1, 1)],
        out_specs=[seq(), st(hd, hd), st(1, hd), st(1, 1)],
        out_shape=[jax.ShapeDtypeStruct((n, M_WIDTH), BF16),
                   jax.ShapeDtypeStruct((nb, M_HEADS, hd, hd), F32),
                   jax.ShapeDtypeStruct((nb, M_HEADS, 1, hd), F32),
                   jax.ShapeDtypeStruct((nb, M_HEADS, 1, 1), F32)],
        compiler_params=_cparams("parallel", "parallel"),
        name="mlstm",
    )(q, k, v, so, col, rowt, g, c0, n0.reshape(nb, M_HEADS, 1, hd), m0.reshape(nb, M_HEADS, 1, 1))


def _conv_carry_init(state_ref, carry_ref, tiles_per_seq):
    @pl.when((pl.program_id(0) % tiles_per_seq) == 0)
    def _():
        carry_ref[...] = state_ref[...]


def _conv_long(u, w_ref, c0, carry_ref):
    tm, ch = u.shape
    prev = carry_ref[:, c0:c0 + ch]
    row = lax.broadcasted_iota(jnp.int32, (tm, 1), 0)
    p1 = jnp.where(row == 0, prev[1:2, :], pltpu.roll(u, 1, axis=0))
    p2 = jnp.where(row == 0, prev[0:1, :], jnp.where(row == 1, prev[1:2, :], pltpu.roll(u, 2, axis=0)))
    carry_ref[:, c0:c0 + ch] = u[tm - 2:tm, :]
    return p2 * w_ref[0:1, c0:c0 + ch] + p1 * w_ref[1:2, c0:c0 + ch] + u * w_ref[2:3, c0:c0 + ch]


def _conv_short(u, w_ref, c0, fill_ref, s_len):
    tm, ch = u.shape
    fill = fill_ref[:, c0:c0 + ch]
    rmod = lax.broadcasted_iota(jnp.int32, (tm, 1), 0) & (s_len - 1)
    p1 = jnp.where(rmod >= 1, pltpu.roll(u, 1, axis=0), pltpu.roll(fill, tm - 1, axis=0))
    p2 = jnp.where(rmod >= 2, pltpu.roll(u, 2, axis=0), fill)
    return p2 * w_ref[0:1, c0:c0 + ch] + p1 * w_ref[1:2, c0:c0 + ch] + u * w_ref[2:3, c0:c0 + ch]


def _merge_kernel(x_ref, g_ref, wc_ref, bc_ref, wg_ref, bg_ref, ws_ref, ya_ref, ym_ref,
                  woa_ref, wom_ref, woc_ref, wo_ref, st_ref, x1_ref, cst_ref, *scratch,
                  long_seq, tiles_per_seq, s_len):
    x = x_ref[...]
    h = _rms(x, g_ref[...]).astype(BF16)
    cw = C_WIDTH
    cb = _dot(h, wc_ref[:, 0:cw]) + bc_ref[:, 0:cw]
    u = (_dot(h, wc_ref[:, cw:2 * cw]) + bc_ref[:, cw:2 * cw]) * (_dot(h, wc_ref[:, 2 * cw:3 * cw]) + bc_ref[:, 2 * cw:3 * cw])
    if long_seq:
        _conv_carry_init(st_ref, scratch[0], tiles_per_seq)
        uc = _conv_long(u, ws_ref, 0, scratch[0])
        cst_ref[...] = u[u.shape[0] - 2:, :]
    else:
        uc = _conv_short(u, ws_ref, 0, st_ref, s_len)
        cst_ref[...] = u
    yc = (cb * uc).astype(BF16)
    d = D_MODEL

    def gate(i):
        return _sigmoid(_dot(h, wg_ref[:, i * d:(i + 1) * d]) + bg_ref[:, i * d:(i + 1) * d])

    merged = gate(0) * _dot(ya_ref[...], woa_ref[...])
    merged = merged + gate(1) * _dot(ym_ref[...], wom_ref[...])
    merged = merged + gate(2) * _dot(yc, woc_ref[...])
    x1_ref[...] = x + _dot(merged.astype(BF16), wo_ref[...])


def _merge(x, g, wc, bc, wg, bg, ws, ya, ym, woa, wom, woc, wo, st, nb, t, tm):
    n = x.shape[0]
    long_seq = t >= tm
    tiles_per_seq = max(t // tm, 1)
    row = lambda c: pl.BlockSpec((tm, c), lambda i: (i, 0))
    if long_seq:
        st_spec = pl.BlockSpec((None, 2, C_WIDTH), lambda i: (i // tiles_per_seq, 0, 0))
        cst_spec = pl.BlockSpec((None, 2, C_WIDTH), lambda i: (i // tiles_per_seq, 0, 0))
        cst_shape = jax.ShapeDtypeStruct((nb, 2, C_WIDTH), F32)
        scratch = [pltpu.VMEM((2, C_WIDTH), F32)]
    else:
        st_spec = row(C_WIDTH)
        cst_spec = row(C_WIDTH)
        cst_shape = jax.ShapeDtypeStruct((n, C_WIDTH), F32)
        scratch = []
    return pl.pallas_call(
        functools.partial(_merge_kernel, long_seq=long_seq, tiles_per_seq=tiles_per_seq, s_len=t),
        grid=(n // tm,),
        in_specs=[row(D_MODEL), _const_spec((1, D_MODEL)),
                  _const_spec(wc.shape), _const_spec(bc.shape), _const_spec(wg.shape), _const_spec(bg.shape),
                  _const_spec(ws.shape), row(A_WIDTH), row(M_WIDTH),
                  _const_spec(woa.shape), _const_spec(wom.shape), _const_spec(woc.shape), _const_spec(wo.shape),
                  st_spec],
        out_specs=[row(D_MODEL), cst_spec],
        out_shape=[jax.ShapeDtypeStruct((n, D_MODEL), F32), cst_shape],
        scratch_shapes=scratch,
        compiler_params=_cparams("arbitrary"),
        name="merge",
    )(x, g, wc, bc, wg, bg, ws, ya, ym, woa, wom, woc, wo, st)


FF_CHUNK = D_FF // 2


def _ffn_kernel(x_ref, pe_ref, gf_ref, wup_ref, wfc_ref, bfc_ref, wdn_ref, gp_ref, wpg_ref, wpp_ref, gfin_ref,
                st_ref, xo_ref, fst_ref, *scratch, long_seq, tiles_per_seq, s_len, final):
    x = x_ref[...]
    tm = x.shape[0]
    h2 = _rms(x, gf_ref[...]).astype(BF16)
    acc = jnp.zeros((tm, D_MODEL), F32)
    if long_seq:
        _conv_carry_init(st_ref, scratch[0], tiles_per_seq)
    for half in range(D_FF // FF_CHUNK):
        c0 = half * FF_CHUNK
        ua = _dot(h2, wup_ref[:, c0:c0 + FF_CHUNK])
        ub = _dot(h2, wup_ref[:, D_FF + c0:D_FF + c0 + FF_CHUNK])
        if long_seq:
            uac = _conv_long(ua, wfc_ref, c0, scratch[0])
            fst_ref[:, c0:c0 + FF_CHUNK] = ua[tm - 2:, :]
        else:
            uac = _conv_short(ua, wfc_ref, c0, st_ref, s_len)
            fst_ref[:, c0:c0 + FF_CHUNK] = ua
        z = uac + bfc_ref[:, c0:c0 + FF_CHUNK]
        act = (z * _sigmoid(z) * ub).astype(BF16)
        acc = acc + _dot(act, wdn_ref[c0:c0 + FF_CHUNK, :])
    x2 = x + acc
    gate = _sigmoid(_dot(_rms(x2, gp_ref[...]).astype(BF16), wpg_ref[...]))
    x3 = x2 + gate * _dot(pe_ref[...].astype(BF16), wpp_ref[...])
    if final:
        x3 = _rms(x3, gfin_ref[...])
    xo_ref[...] = x3


def _ffn(x, pe, gf, wup, wfc, bfc, wdn, gp, wpg, wpp, gfin, st, nb, t, tm, final):
    n = x.shape[0]
    long_seq = t >= tm
    tiles_per_seq = max(t // tm, 1)
    row = lambda c: pl.BlockSpec((tm, c), lambda i: (i, 0))
    if long_seq:
        st_spec = pl.BlockSpec((None, 2, D_FF), lambda i: (i // tiles_per_seq, 0, 0))
        fst_spec = pl.BlockSpec((None, 2, D_FF), lambda i: (i // tiles_per_seq, 0, 0))
        fst_shape = jax.ShapeDtypeStruct((nb, 2, D_FF), F32)
        scratch = [pltpu.VMEM((2, D_FF), F32)]
    else:
        st_spec = row(D_FF)
        fst_spec = row(D_FF)
        fst_shape = jax.ShapeDtypeStruct((n, D_FF), F32)
        scratch = []
    return pl.pallas_call(
        functools.partial(_ffn_kernel, long_seq=long_seq, tiles_per_seq=tiles_per_seq, s_len=t, final=final),
        grid=(n // tm,),
        in_specs=[row(D_MODEL), row(pe.shape[1]), _const_spec((1, D_MODEL)),
                  _const_spec(wup.shape), _const_spec(wfc.shape), _const_spec(bfc.shape), _const_spec(wdn.shape),
                  _const_spec((1, D_MODEL)), _const_spec(wpg.shape), _const_spec(wpp.shape),
                  _const_spec((1, D_MODEL)), st_spec],
        out_specs=[row(D_MODEL), fst_spec],
        out_shape=[jax.ShapeDtypeStruct((n, D_MODEL), F32), fst_shape],
        scratch_shapes=scratch,
        compiler_params=_cparams("arbitrary"),
        name="ffn",
    )(x, pe, gf, wup, wfc, bfc, wdn, gp, wpg, wpp, gfin, st)


def _row_form(col, nb, t):
    return col[:, :SM_USED].reshape(nb, t, SM_USED).transpose(0, 2, 1)


def _short_fill(state, s_len):
    nb, _, ch = state.shape
    return jnp.pad(state, ((0, 0), (0, s_len - 2), (0, 0))).reshape(nb * s_len, ch)


def _pick(n, pref):
    return pref if n % pref == 0 else n


def _layer(x, pe, w, nb, t, chunk, past, final):
    n = x.shape[0]
    tm_in = _pick(n, 512)
    tm_merge = _pick(n, 512)
    tm_ffn = _pick(n, 256)
    q, k, v, mq, mk, mv, so, sm = _inproj(x, w["g_mix"], w["w_am"], w["b_am"], tm_in)
    if past is None:
        col, logf = _gates(sm, t, t, chunk)
        rowt = _row_form(col, nb, t)
        ya = _fox_prompt(q, k, v, col, rowt, nb, t, _pick(t, 256))
        c0 = jnp.zeros((nb, M_HEADS, M_HEAD_DIM, M_HEAD_DIM), F32)
        n0 = jnp.zeros((nb, M_HEADS, M_HEAD_DIM), F32)
        m0 = jnp.zeros((nb, M_HEADS), F32)
        conv0 = jnp.zeros((nb, 2, C_WIDTH), F32)
        ffn0 = jnp.zeros((nb, 2, D_FF), F32)
    else:
        col, logf = _gates(sm, n, t, t)
        rowt = _row_form(col, nb, t)
        ya = _fox_sample(past["page_table"], q, k, v, col, rowt, past["cache_k"], past["cache_v"],
                         past["lc"], past["layer"], nb, t)
        c0, n0, m0, conv0, ffn0 = past["c0"], past["n0"], past["m0"], past["conv0"], past["ffn0"]
    ym, c1, n1, m1 = _mlstm(mq, mk, mv, so, col, rowt, w["g_mhead"], c0, n0, m0, nb, t, chunk)
    long_merge = t >= tm_merge
    long_ffn = t >= tm_ffn
    x1, conv1 = _merge(x, w["g_mix"], w["w_c"], w["b_c"], w["w_g"], w["b_g"], w["w_sconv"], ya, ym,
                       w["w_oa"], w["w_om"], w["w_oc"], w["w_o"],
                       conv0 if long_merge else _short_fill(conv0, t), nb, t, tm_merge)
    x3, ffn1 = _ffn(x1, pe, w["g_ffn"], w["w_up"], w["w_fconv"], w["b_fconv"], w["w_down"],
                    w["g_ple"], w["w_ple_gate"], w["w_ple"], w["g_final"],
                    ffn0 if long_ffn else _short_fill(ffn0, t), nb, t, tm_ffn, final)
    if not long_merge:
        conv1 = conv1.reshape(nb, t, C_WIDTH)[:, t - 2:, :]
    if not long_ffn:
        ffn1 = ffn1.reshape(nb, t, D_FF)[:, t - 2:, :]
    state = (k.reshape(nb, t, A_HEADS, A_HEAD_DIM), v.reshape(nb, t, A_HEADS, A_HEAD_DIM),
             logf.reshape(nb, t, A_HEADS), c1, n1.reshape(nb, M_HEADS, M_HEAD_DIM), m1.reshape(nb, M_HEADS),
             conv1, ffn1)
    return x3, state


def _layer_weights(l, w_in, b_in, g_mix, g_mhead, w_sconv, w_oa, w_om, w_oc, w_o, g_ffn, w_up,
                   w_fconv, b_fconv, w_down, g_ple, w_ple_gate, w_ple, g_final):
    a0 = 0
    af0 = 3 * A_WIDTH
    m0 = af0 + A_HEADS
    mi0 = m0 + 3 * M_WIDTH
    mo0 = mi0 + 2 * M_HEADS
    c0 = mo0 + M_WIDTH
    g0 = c0 + 3 * C_WIDTH
    wl, bl = w_in[l], b_in[l]
    pad = LANES - SM_USED

    def cols(a, lo, hi):
        return a[..., lo:hi]

    def am(a):
        return jnp.concatenate(
            [cols(a, a0, af0), cols(a, m0, mi0), cols(a, mo0, c0),
             cols(a, af0, m0), cols(a, mi0, mo0), jnp.zeros(a.shape[:-1] + (pad,), a.dtype)], axis=-1)

    row = lambda a: a.reshape(1, -1)
    return {
        "w_am": am(wl).astype(BF16), "b_am": row(am(bl)),
        "w_c": cols(wl, c0, g0).astype(BF16), "b_c": row(cols(bl, c0, g0)),
        "w_g": cols(wl, g0, g0 + 3 * D_MODEL).astype(BF16), "b_g": row(cols(bl, g0, g0 + 3 * D_MODEL)),
        "g_mix": row(g_mix[l]), "g_mhead": row(g_mhead[l]), "w_sconv": w_sconv[l],
        "w_oa": w_oa[l].astype(BF16), "w_om": w_om[l].astype(BF16), "w_oc": w_oc[l].astype(BF16),
        "w_o": w_o[l].astype(BF16), "g_ffn": row(g_ffn[l]), "w_up": w_up[l].astype(BF16),
        "w_fconv": w_fconv[l], "b_fconv": row(b_fconv[l]), "w_down": w_down[l].astype(BF16),
        "g_ple": row(g_ple[l]), "w_ple_gate": w_ple_gate[l].astype(BF16), "w_ple": w_ple[l].astype(BF16),
        "g_final": row(g_final),
    }


def kernel(x_prompt, x_sample, cache_k, cache_v, cache_logf, state_mlstm_C, state_mlstm_n, state_mlstm_m, state_conv, state_ffn_conv, page_table, p_prompt, p_sample, w_in, b_in, g_mix, g_mhead, w_sconv, w_oa, w_om, w_oc, w_o, g_ffn, w_up, w_fconv, b_fconv, w_down, g_ple, w_ple_gate, w_ple, g_final):
    nbp, tp, _ = x_prompt.shape
    nbs, ts, _ = x_sample.shape
    depth = w_in.shape[0]
    n_phys, page = cache_logf.shape[0], cache_logf.shape[1]
    assert page == LANES and ts == SUBLANES and depth * A_HEADS == SM_USED

    lf_t = cache_logf.reshape(n_phys, page, depth * A_HEADS).transpose(0, 2, 1).reshape(-1, LANES)
    lc = _pool_scan(lf_t, _pick(lf_t.shape[0], 2048)).reshape(n_phys, depth * A_HEADS, LANES)

    xp = x_prompt.reshape(nbp * tp, D_MODEL)
    xs = x_sample.reshape(nbs * ts, D_MODEL)
    chunk_p = _pick(tp, 256)
    new_p, new_s = [], []
    for l in range(depth):
        w = _layer_weights(l, w_in, b_in, g_mix, g_mhead, w_sconv, w_oa, w_om, w_oc, w_o, g_ffn, w_up,
                           w_fconv, b_fconv, w_down, g_ple, w_ple_gate, w_ple, g_final)
        final = l == depth - 1
        xp, st_p = _layer(xp, p_prompt[l].reshape(nbp * tp, -1), w, nbp, tp, chunk_p, None, final)
        past = {"page_table": page_table, "cache_k": cache_k, "cache_v": cache_v, "lc": lc, "layer": l,
                "c0": state_mlstm_C[:, l], "n0": state_mlstm_n[:, l], "m0": state_mlstm_m[:, l],
                "conv0": state_conv[:, l], "ffn0": state_ffn_conv[:, l]}
        xs, st_s = _layer(xs, p_sample[l].reshape(nbs * ts, -1), w, nbs, ts, ts, past, final)
        new_p.append(st_p)
        new_s.append(st_s)

    def stack(per_layer):
        rows = [jnp.stack([s[i] for s in per_layer], axis=2) for i in range(3)]
        states = [jnp.stack([s[i] for s in per_layer], axis=1) for i in range(3, 8)]
        return tuple(rows + states)

    return ((xp.reshape(nbp, tp, D_MODEL), xs.reshape(nbs, ts, D_MODEL)) + stack(new_p) + stack(new_s))
```

```python
import functools

import jax
import jax.numpy as jnp
from jax import lax
from jax.experimental import pallas as pl
from jax.experimental.pallas import tpu as pltpu

F32 = jnp.float32
BF16 = jnp.bfloat16

D_MODEL = 1024
A_HEADS = 8
A_HEAD_DIM = 64
A_WIDTH = A_HEADS * A_HEAD_DIM
M_HEADS = 4
M_HEAD_DIM = 128
M_WIDTH = M_HEADS * M_HEAD_DIM
C_WIDTH = 512
D_FF = 2816
EPS = 1e-6
LANES = 128
SUBLANES = 8
VMEM_LIMIT = 56 * 1024 * 1024

SM_AF = 0
SM_MI = 8
SM_MF = 12
SM_USED = 16

NT = (((1,), (1,)), ((), ()))


def _cparams(*sem):
    return pltpu.CompilerParams(dimension_semantics=sem, vmem_limit_bytes=VMEM_LIMIT)


def _const_spec(shape):
    nd = len(shape)
    return pl.BlockSpec(shape, lambda *_: (0,) * nd, pipeline_mode=pl.Buffered(1))


def _rms(x, g):
    return x * lax.rsqrt(jnp.mean(x * x, axis=-1, keepdims=True) + EPS) * g


def _log_sigmoid(x):
    return jnp.minimum(x, 0.0) - jnp.log1p(jnp.exp(-jnp.abs(x)))


def _sigmoid(x):
    return 1.0 / (1.0 + jnp.exp(-x))


def _dot(a, b):
    return jnp.dot(a, b, preferred_element_type=F32)


def _inproj_kernel(*refs, transposed_kv):
    if transposed_kv:
        (x_ref, g_ref, w_ref, b_ref, wt_ref, bt_ref, _, _,
         q_ref, k_ref, mq_ref, mk_ref, mv_ref, so_ref, sm_ref, kt_ref, vt_ref) = refs
    else:
        x_ref, g_ref, w_ref, b_ref, q_ref, k_ref, v_ref, mq_ref, mk_ref, mv_ref, so_ref, sm_ref = refs
    h = _rms(x_ref[...], g_ref[...]).astype(BF16)

    def proj(c0, n):
        return _dot(h, w_ref[:, c0:c0 + n]) + b_ref[:, c0:c0 + n]

    q_ref[...] = (proj(0, A_WIDTH) * (A_HEAD_DIM ** -0.5)).astype(BF16)
    if transposed_kv:
        k_ref[...] = proj(A_WIDTH, A_WIDTH).astype(BF16)
        kt_ref[...] = (lax.dot_general(wt_ref[0:A_WIDTH, :], h, NT, preferred_element_type=F32)
                       + bt_ref[0:A_WIDTH, :])
        vt_ref[...] = (lax.dot_general(wt_ref[A_WIDTH:2 * A_WIDTH, :], h, NT, preferred_element_type=F32)
                       + bt_ref[A_WIDTH:2 * A_WIDTH, :])
    else:
        k_ref[...] = proj(A_WIDTH, A_WIDTH)
        v_ref[...] = proj(2 * A_WIDTH, A_WIDTH)
    o = 3 * A_WIDTH
    mq_ref[...] = proj(o, M_WIDTH).astype(BF16)
    mk_ref[...] = (proj(o + M_WIDTH, M_WIDTH) * (M_HEAD_DIM ** -0.5)).astype(BF16)
    mv_ref[...] = proj(o + 2 * M_WIDTH, M_WIDTH).astype(BF16)
    so_ref[...] = _sigmoid(proj(o + 3 * M_WIDTH, M_WIDTH))
    sm_ref[...] = proj(o + 4 * M_WIDTH, LANES)


def _inproj(x, g, w, b, tm, kv=None):
    n = x.shape[0]
    wcols = w.shape[1]
    row = lambda c: pl.BlockSpec((tm, c), lambda i: (i, 0))
    common = [
        jax.ShapeDtypeStruct((n, M_WIDTH), BF16),
        jax.ShapeDtypeStruct((n, M_WIDTH), BF16),
        jax.ShapeDtypeStruct((n, M_WIDTH), BF16),
        jax.ShapeDtypeStruct((n, M_WIDTH), F32),
        jax.ShapeDtypeStruct((n, LANES), F32),
    ]
    common_specs = [row(M_WIDTH)] * 4 + [row(LANES)]
    in_specs = [row(D_MODEL), _const_spec((1, D_MODEL)), _const_spec((D_MODEL, wcols)), _const_spec((1, wcols))]
    if kv is None:
        return pl.pallas_call(
            functools.partial(_inproj_kernel, transposed_kv=False),
            grid=(n // tm,),
            in_specs=in_specs,
            out_specs=[row(A_WIDTH)] * 3 + common_specs,
            out_shape=[jax.ShapeDtypeStruct((n, A_WIDTH), BF16), jax.ShapeDtypeStruct((n, A_WIDTH), F32),
                       jax.ShapeDtypeStruct((n, A_WIDTH), F32)] + common,
            compiler_params=_cparams("parallel"),
            name="inproj",
        )(x, g, w, b)
    nb, t, depth, layer = kv["nb"], kv["t"], kv["depth"], kv["layer"]
    tps = t // tm
    t_shape = jax.ShapeDtypeStruct((nb, depth, A_WIDTH, t), F32)
    t_spec = pl.BlockSpec((None, None, A_WIDTH, tm), lambda i: (i // tps, layer, 0, i % tps))
    prev = kv["prev"]
    if prev is None:
        prev = (jnp.zeros((SUBLANES, LANES), F32),) * 2
        aliases = {}
    else:
        aliases = {6: 7, 7: 8}
    any_spec = pl.BlockSpec(memory_space=pl.ANY)
    return pl.pallas_call(
        functools.partial(_inproj_kernel, transposed_kv=True),
        grid=(n // tm,),
        in_specs=in_specs + [_const_spec(kv["wt"].shape), _const_spec(kv["bt"].shape), any_spec, any_spec],
        out_specs=[row(A_WIDTH)] * 2 + common_specs + [t_spec, t_spec],
        out_shape=[jax.ShapeDtypeStruct((n, A_WIDTH), BF16), jax.ShapeDtypeStruct((n, A_WIDTH), BF16)]
                  + common + [t_shape, t_shape],
        input_output_aliases=aliases,
        compiler_params=_cparams("parallel"),
        name="inproj_t",
    )(x, g, w, b, kv["wt"], kv["bt"], *prev)


def _gates_kernel(sm_ref, col_ref, logf_ref, *, seq_a, seg_m):
    x = sm_ref[...]
    rows = x.shape[0]
    lane = lax.broadcasted_iota(jnp.int32, (1, LANES), 1)
    row = lax.broadcasted_iota(jnp.int32, (rows, 1), 0)
    ls = _log_sigmoid(x)
    logf_ref[...] = ls[:, SM_AF:SM_AF + A_HEADS]
    is_a = lane < SM_MI
    is_i = (lane >= SM_MI) & (lane < SM_MF)
    is_f = (lane >= SM_MF) & (lane < SM_USED)
    y = jnp.where(is_i, x, ls)
    rmod_a = row & (seq_a - 1)
    rmod_m = row & (seg_m - 1)
    s = 1
    while s < max(seq_a, seg_m):
        take = jnp.zeros((rows, LANES), jnp.bool_)
        if s < seq_a:
            take = take | (is_a & (rmod_a >= s))
        if s < seg_m:
            take = take | (is_f & (rmod_m >= s))
        y = y + jnp.where(take, pltpu.roll(y, s, axis=0), 0.0)
        s *= 2
    col_ref[...] = y


def _gates(sm, rows, seq_a, seg_m):
    n = sm.shape[0]
    return pl.pallas_call(
        functools.partial(_gates_kernel, seq_a=seq_a, seg_m=seg_m),
        grid=(n // rows,),
        in_specs=[pl.BlockSpec((rows, LANES), lambda i: (i, 0))],
        out_specs=[pl.BlockSpec((rows, LANES), lambda i: (i, 0)),
                   pl.BlockSpec((rows, A_HEADS), lambda i: (i, 0))],
        out_shape=[jax.ShapeDtypeStruct((n, LANES), F32),
                   jax.ShapeDtypeStruct((n, A_HEADS), F32)],
        compiler_params=_cparams("parallel"),
        name="gates",
    )(sm)


def _fox_kernel(q_ref, k_ref, vt_ref, col_ref, row_ref, o_ref, vb_ref, ck_ref, *, tq):
    hp = pl.program_id(1)
    qi = pl.program_id(2)
    lane = lax.broadcasted_iota(jnp.int32, (1, LANES), 1)
    hd = A_HEAD_DIM

    @pl.when(qi == 0)
    def _():
        vb_ref[...] = vt_ref[...].astype(BF16)
        colv = col_ref[...]
        for hh in range(2):
            ck = jnp.sum(jnp.where(lane == SM_AF + hp * 2 + hh, colv, 0.0), axis=1, keepdims=True)
            ck_ref[hh] = jnp.broadcast_to(ck, colv.shape)

    q2 = q_ref[...]
    zero = jnp.zeros_like(q2)
    q0 = pl.multiple_of(qi * tq, tq)
    qm = [jnp.where((lane >= hd) == bool(hh), q2, zero) for hh in range(2)]
    cq = [row_ref[pl.ds(SM_AF + hp * 2 + hh, 1), pl.ds(q0, tq)] for hh in range(2)]

    def step(j, carry, diag):
        start = pl.multiple_of(j * tq, tq)
        kb = k_ref[pl.ds(start, tq), :]
        out = []
        for hh in range(2):
            m, l, acc = carry[hh]
            ck = ck_ref[hh, pl.ds(start, tq), :]
            s = lax.dot_general(kb, qm[hh], NT, preferred_element_type=F32)
            s = s + (cq[hh] - jnp.concatenate([ck] * (tq // LANES), axis=1))
            if diag:
                kpos = lax.broadcasted_iota(jnp.int32, (tq, tq), 0)
                qpos = lax.broadcasted_iota(jnp.int32, (tq, tq), 1)
                s = jnp.where(kpos <= qpos, s, -jnp.inf)
            m_new = jnp.maximum(m, jnp.max(s, axis=0, keepdims=True))
            alpha = jnp.exp(m - m_new)
            p = jnp.exp(s - m_new)
            l = alpha * l + jnp.sum(p, axis=0, keepdims=True)
            vtb = vb_ref[hh * hd:(hh + 1) * hd, pl.ds(start, tq)]
            acc = alpha * acc + _dot(vtb, p.astype(BF16))
            out.append((m_new, l, acc))
        return tuple(out)

    init = tuple((jnp.full((1, tq), -jnp.inf, F32), jnp.zeros((1, tq), F32),
                  jnp.zeros((hd, tq), F32)) for _ in range(2))
    carry = lax.fori_loop(0, qi, lambda j, c: step(j, c, False), init)
    carry = step(qi, carry, True)
    yt = jnp.concatenate([carry[0][2] / carry[0][1], carry[1][2] / carry[1][1]], axis=0)
    o_ref[...] = yt.T.astype(o_ref.dtype)


def _fox_prompt(q, k, vt, col, rowt, layer, nb, t, tq):
    n = q.shape[0]
    nq = t // tq
    pairs = A_WIDTH // LANES
    return pl.pallas_call(
        functools.partial(_fox_kernel, tq=tq),
        grid=(nb, pairs, nq),
        in_specs=[
            pl.BlockSpec((tq, LANES), lambda b, hp, qi: (b * nq + qi, hp)),
            pl.BlockSpec((t, LANES), lambda b, hp, qi: (b, hp)),
            pl.BlockSpec((None, None, LANES, t), lambda b, hp, qi: (b, layer, hp, 0)),
            pl.BlockSpec((t, LANES), lambda b, hp, qi: (b, 0)),
            pl.BlockSpec((None, SM_USED, t), lambda b, hp, qi: (b, 0, 0)),
        ],
        out_specs=pl.BlockSpec((tq, LANES), lambda b, hp, qi: (b * nq + qi, hp)),
        out_shape=jax.ShapeDtypeStruct((n, A_WIDTH), BF16),
        scratch_shapes=[pltpu.VMEM((LANES, t), BF16), pltpu.VMEM((2, t, LANES), F32)],
        compiler_params=_cparams("parallel", "parallel", "arbitrary"),
        name="fox_prompt",
    )(q, k, vt, col, rowt)


def _pool_scan_kernel(x_ref, o_ref):
    y = x_ref[...]
    lane = lax.broadcasted_iota(jnp.int32, (1, LANES), 1)
    s = 1
    while s < LANES:
        y = y + jnp.where(lane >= s, pltpu.roll(y, s, axis=1), 0.0)
        s *= 2
    o_ref[...] = y


def _pool_scan(x, rows):
    n = x.shape[0]
    return pl.pallas_call(
        _pool_scan_kernel,
        grid=(n // rows,),
        in_specs=[pl.BlockSpec((rows, LANES), lambda i: (i, 0))],
        out_specs=pl.BlockSpec((rows, LANES), lambda i: (i, 0)),
        out_shape=jax.ShapeDtypeStruct((n, LANES), F32),
        compiler_params=_cparams("parallel"),
        name="pool_scan",
    )(x)


def _fox_sample_kernel(pt_ref, q_ref, kn_ref, vn_ref, col_ref, rown_ref, *rest,
                       n_pages, layer, s_len):
    del pt_ref
    k_refs = rest[:n_pages]
    v_refs = rest[n_pages:2 * n_pages]
    lc_refs = rest[2 * n_pages:3 * n_pages]
    o_ref = rest[3 * n_pages]
    s_ref = rest[3 * n_pages + 1]
    nh = A_HEADS
    rows = nh * s_len
    lane_w = lax.broadcasted_iota(jnp.int32, (1, A_WIDTH), 1)
    rid = lax.broadcasted_iota(jnp.int32, (rows, 1), 0)
    own = jnp.right_shift(lane_w, 6) == jnp.right_shift(rid, 3)
    q = q_ref[...].astype(F32)
    qbd = jnp.where(own, jnp.concatenate([q] * nh, axis=0), 0.0)
    colv = col_ref[...]
    rown = rown_ref[...]
    lo = layer * nh

    tot = [lc_refs[p][lo:lo + nh, LANES - 1:LANES] for p in range(n_pages)]
    base = [None] * n_pages
    suf = jnp.zeros((nh, 1), F32)
    for p in range(n_pages - 1, -1, -1):
        suf = suf + tot[p]
        base[p] = suf

    m_run = [None] * nh
    for p in range(n_pages):
        sp = _dot(qbd, k_refs[p][...])
        for h in range(nh):
            sl = slice(h * s_len, (h + 1) * s_len)
            cn = colv[:, SM_AF + h:SM_AF + h + 1]
            bias = (base[p][h:h + 1, :] - lc_refs[p][lo + h:lo + h + 1, :]) + cn
            sh = sp[sl, :] + bias
            s_ref[sl, p * LANES:(p + 1) * LANES] = sh
            mx = jnp.max(sh, axis=1, keepdims=True)
            m_run[h] = mx if m_run[h] is None else jnp.maximum(m_run[h], mx)
    sn = lax.dot_general(qbd, kn_ref[...], NT, preferred_element_type=F32)
    r = lax.broadcasted_iota(jnp.int32, (s_len, s_len), 0)
    c = lax.broadcasted_iota(jnp.int32, (s_len, s_len), 1)
    pn, m_all = [], []
    for h in range(nh):
        sl = slice(h * s_len, (h + 1) * s_len)
        cn = colv[:, SM_AF + h:SM_AF + h + 1]
        sh = sn[sl, :] + (cn - rown[SM_AF + h:SM_AF + h + 1, :])
        sh = jnp.where(c <= r, sh, -jnp.inf)
        pn.append(sh)
        m_all.append(jnp.maximum(m_run[h], jnp.max(sh, axis=1, keepdims=True)))
    m = jnp.concatenate(m_all, axis=0)
    p_new = jnp.exp(jnp.concatenate(pn, axis=0) - m)
    l = jnp.sum(p_new, axis=1, keepdims=True)
    acc = _dot(p_new, vn_ref[...])
    for p in range(n_pages):
        pp = jnp.exp(s_ref[:, p * LANES:(p + 1) * LANES] - m)
        l = l + jnp.sum(pp, axis=1, keepdims=True)
        acc = acc + lax.dot_general(pp, v_refs[p][...], NT, preferred_element_type=F32)
    acc = jnp.where(own, acc / l, 0.0)
    y = acc[0:s_len, :]
    for h in range(1, nh):
        y = y + acc[h * s_len:(h + 1) * s_len, :]
    o_ref[...] = y.astype(o_ref.dtype)


def _fox_sample(page_table, q, k_new, v_new, col, rown, cache_k, cache_v, lc, layer, nb, s_len):
    n_pages = page_table.shape[1]
    n_phys, page = cache_k.shape[0], cache_k.shape[1]
    depth = cache_k.shape[2]
    ck = cache_k.transpose(0, 2, 3, 4, 1).reshape(n_phys, depth, A_WIDTH, page)
    cv = cache_v.transpose(0, 2, 3, 4, 1).reshape(n_phys, depth, A_WIDTH, page)
    tok = lambda c: pl.BlockSpec((s_len, c), lambda b, pt: (b, 0))
    page_specs = [pl.BlockSpec((None, None, A_WIDTH, page),
                               lambda b, pt, j=j: (pt[b, j], layer, 0, 0)) for j in range(n_pages)]
    lc_specs = [pl.BlockSpec((None, 2 * A_HEADS, LANES),
                             lambda b, pt, j=j: (pt[b, j], 0, 0)) for j in range(n_pages)]
    grid_spec = pltpu.PrefetchScalarGridSpec(
        num_scalar_prefetch=1,
        grid=(nb,),
        in_specs=[tok(A_WIDTH), tok(A_WIDTH), tok(A_WIDTH), tok(LANES),
                  pl.BlockSpec((None, SM_USED, s_len), lambda b, pt: (b, 0, 0))]
                 + page_specs + page_specs + lc_specs,
        out_specs=tok(A_WIDTH),
        scratch_shapes=[pltpu.VMEM((A_HEADS * s_len, n_pages * LANES), F32)],
    )
    return pl.pallas_call(
        functools.partial(_fox_sample_kernel, n_pages=n_pages, layer=layer, s_len=s_len),
        grid_spec=grid_spec,
        out_shape=jax.ShapeDtypeStruct((nb * s_len, A_WIDTH), BF16),
        compiler_params=_cparams("arbitrary"),
        name="fox_sample",
    )(page_table, q, k_new, v_new, col, rown, *([ck] * n_pages), *([cv] * n_pages), *([lc] * n_pages))


def _mlstm_kernel(q_ref, k_ref, v_ref, so_ref, col_ref, row_ref, g_ref, c0_ref, n0_ref, m0_ref,
                  y_ref, c1_ref, n1_ref, m1_ref, *, t, chunk):
    hd = pl.program_id(1)
    lane = lax.broadcasted_iota(jnp.int32, (1, LANES), 1)
    r = lax.broadcasted_iota(jnp.int32, (chunk, chunk), 0)
    c = lax.broadcasted_iota(jnp.int32, (chunk, chunk), 1)
    causal = c <= r
    g = g_ref[...]

    def body(ci, carry):
        cst, nst, mst = carry
        r0 = pl.multiple_of(ci * chunk, chunk)
        q = q_ref[pl.ds(r0, chunk), :].astype(F32)
        k = k_ref[pl.ds(r0, chunk), :].astype(F32)
        v = v_ref[pl.ds(r0, chunk), :].astype(F32)
        colv = col_ref[pl.ds(r0, chunk), :]
        f_col = jnp.sum(jnp.where(lane == SM_MF + hd, colv, 0.0), axis=1, keepdims=True)
        i_col = jnp.sum(jnp.where(lane == SM_MI + hd, colv, 0.0), axis=1, keepdims=True)
        f_row = row_ref[pl.ds(SM_MF + hd, 1), pl.ds(r0, chunk)]
        i_row = row_ref[pl.ds(SM_MI + hd, 1), pl.ds(r0, chunk)]
        d = jnp.where(causal, f_col + (i_row - f_row), -jnp.inf)
        inter = mst + f_col
        m_t = jnp.maximum(inter, jnp.max(d, axis=1, keepdims=True))
        w = jnp.exp(d - m_t)
        a = jnp.exp(inter - m_t)
        wqk = w * lax.dot_general(q, k, NT, preferred_element_type=F32)
        num = a * lax.dot_general(q, cst, NT, preferred_element_type=F32) + _dot(wqk, v)
        den = a * jnp.sum(q * nst, axis=1, keepdims=True) + jnp.sum(wqk, axis=1, keepdims=True)
        h_t = num / jnp.maximum(jnp.abs(den), jnp.exp(-m_t))
        hn = h_t * lax.rsqrt(jnp.mean(h_t * h_t, axis=1, keepdims=True) + EPS) * g
        y_ref[pl.ds(r0, chunk), :] = (so_ref[pl.ds(r0, chunk), :] * hn).astype(y_ref.dtype)
        m_last = m_t[chunk - 1:chunk, :]
        f_last = f_col[chunk - 1:chunk, :]
        a_last = a[chunk - 1:chunk, :]
        w_last = jnp.exp((i_col - f_col) + (f_last - m_last))
        vw = v * w_last
        c_new = a_last * cst + _dot(vw.T, k)
        n_new = a_last * nst + jnp.sum(k * w_last, axis=0, keepdims=True)
        return c_new, n_new, m_last

    cst, nst, mst = lax.fori_loop(0, t // chunk, body, (c0_ref[...], n0_ref[...], m0_ref[...]))
    c1_ref[...] = cst
    n1_ref[...] = nst
    m1_ref[...] = mst


def _mlstm(q, k, v, so, col, rowt, g, c0, n0, m0, nb, t, chunk):
    n = q.shape[0]
    hd = M_HEAD_DIM
    seq = lambda: pl.BlockSpec((t, hd), lambda b, h: (b, h))
    st = lambda a, b_: pl.BlockSpec((None, None, a, b_), lambda b, h: (b, h, 0, 0))
    return pl.pallas_call(
        functools.partial(_mlstm_kernel, t=t, chunk=chunk),
        grid=(nb, M_HEADS),
        in_specs=[seq(), seq(), seq(), seq(),
                  pl.BlockSpec((t, LANES), lambda b, h: (b, 0)),
                  pl.BlockSpec((None, SM_USED, t), lambda b, h: (b, 0, 0)),
                  pl.BlockSpec((1, hd), lambda b, h: (0, h)),
                  st(hd, hd), st(1, hd), st(1, 1)],
        out_specs=[seq(), st(hd, hd), st(1, hd), st(1, 1)],
        out_shape=[jax.ShapeDtypeStruct((n, M_WIDTH), BF16),
                   jax.ShapeDtypeStruct((nb, M_HEADS, hd, hd), F32),
                   jax.ShapeDtypeStruct((nb, M_HEADS, 1, hd), F32),
                   jax.ShapeDtypeStruct((nb, M_HEADS, 1, 1), F32)],
        compiler_params=_cparams("parallel", "parallel"),
        name="mlstm",
    )(q, k, v, so, col, rowt, g, c0, n0.reshape(nb, M_HEADS, 1, hd), m0.reshape(nb, M_HEADS, 1, 1))


def _conv_carry_init(state_ref, carry_ref, tiles_per_seq):
    @pl.when((pl.program_id(0) % tiles_per_seq) == 0)
    def _():
        carry_ref[...] = state_ref[...]


def _conv_long(u, w_ref, c0, carry_ref):
    tm, ch = u.shape
    prev = carry_ref[:, c0:c0 + ch]
    row = lax.broadcasted_iota(jnp.int32, (tm, 1), 0)
    p1 = jnp.where(row == 0, prev[1:2, :], pltpu.roll(u, 1, axis=0))
    p2 = jnp.where(row == 0, prev[0:1, :], jnp.where(row == 1, prev[1:2, :], pltpu.roll(u, 2, axis=0)))
    carry_ref[:, c0:c0 + ch] = u[tm - 2:tm, :]
    return p2 * w_ref[0:1, c0:c0 + ch] + p1 * w_ref[1:2, c0:c0 + ch] + u * w_ref[2:3, c0:c0 + ch]


def _conv_short(u, w_ref, c0, fill_ref, s_len):
    tm, ch = u.shape
    fill = fill_ref[:, c0:c0 + ch]
    rmod = lax.broadcasted_iota(jnp.int32, (tm, 1), 0) & (s_len - 1)
    p1 = jnp.where(rmod >= 1, pltpu.roll(u, 1, axis=0), pltpu.roll(fill, tm - 1, axis=0))
    p2 = jnp.where(rmod >= 2, pltpu.roll(u, 2, axis=0), fill)
    return p2 * w_ref[0:1, c0:c0 + ch] + p1 * w_ref[1:2, c0:c0 + ch] + u * w_ref[2:3, c0:c0 + ch]


def _merge_kernel(x_ref, g_ref, wc_ref, bc_ref, wg_ref, bg_ref, ws_ref, ya_ref, ym_ref,
                  woa_ref, wom_ref, woc_ref, wo_ref, st_ref, x1_ref, cst_ref, *scratch,
                  long_seq, tiles_per_seq, s_len):
    x = x_ref[...]
    h = _rms(x, g_ref[...]).astype(BF16)
    cw = C_WIDTH
    cb = _dot(h, wc_ref[:, 0:cw]) + bc_ref[:, 0:cw]
    u = (_dot(h, wc_ref[:, cw:2 * cw]) + bc_ref[:, cw:2 * cw]) * (_dot(h, wc_ref[:, 2 * cw:3 * cw]) + bc_ref[:, 2 * cw:3 * cw])
    if long_seq:
        _conv_carry_init(st_ref, scratch[0], tiles_per_seq)
        uc = _conv_long(u, ws_ref, 0, scratch[0])
        cst_ref[...] = u[u.shape[0] - 2:, :]
    else:
        uc = _conv_short(u, ws_ref, 0, st_ref, s_len)
        cst_ref[...] = u
    yc = (cb * uc).astype(BF16)
    d = D_MODEL

    def gate(i):
        return _sigmoid(_dot(h, wg_ref[:, i * d:(i + 1) * d]) + bg_ref[:, i * d:(i + 1) * d])

    merged = gate(0) * _dot(ya_ref[...], woa_ref[...])
    merged = merged + gate(1) * _dot(ym_ref[...], wom_ref[...])
    merged = merged + gate(2) * _dot(yc, woc_ref[...])
    x1_ref[...] = x + _dot(merged.astype(BF16), wo_ref[...])


def _merge(x, g, wc, bc, wg, bg, ws, ya, ym, woa, wom, woc, wo, st, nb, t, tm):
    n = x.shape[0]
    long_seq = t >= tm
    tiles_per_seq = max(t // tm, 1)
    row = lambda c: pl.BlockSpec((tm, c), lambda i: (i, 0))
    if long_seq:
        st_spec = pl.BlockSpec((None, 2, C_WIDTH), lambda i: (i // tiles_per_seq, 0, 0))
        cst_spec = pl.BlockSpec((None, 2, C_WIDTH), lambda i: (i // tiles_per_seq, 0, 0))
        cst_shape = jax.ShapeDtypeStruct((nb, 2, C_WIDTH), F32)
        scratch = [pltpu.VMEM((2, C_WIDTH), F32)]
    else:
        st_spec = row(C_WIDTH)
        cst_spec = row(C_WIDTH)
        cst_shape = jax.ShapeDtypeStruct((n, C_WIDTH), F32)
        scratch = []
    return pl.pallas_call(
        functools.partial(_merge_kernel, long_seq=long_seq, tiles_per_seq=tiles_per_seq, s_len=t),
        grid=(n // tm,),
        in_specs=[row(D_MODEL), _const_spec((1, D_MODEL)),
                  _const_spec(wc.shape), _const_spec(bc.shape), _const_spec(wg.shape), _const_spec(bg.shape),
                  _const_spec(ws.shape), row(A_WIDTH), row(M_WIDTH),
                  _const_spec(woa.shape), _const_spec(wom.shape), _const_spec(woc.shape), _const_spec(wo.shape),
                  st_spec],
        out_specs=[row(D_MODEL), cst_spec],
        out_shape=[jax.ShapeDtypeStruct((n, D_MODEL), F32), cst_shape],
        scratch_shapes=scratch,
        compiler_params=_cparams("arbitrary"),
        name="merge",
    )(x, g, wc, bc, wg, bg, ws, ya, ym, woa, wom, woc, wo, st)


FF_CHUNK = D_FF // 2


def _ffn_kernel(x_ref, pe_ref, gf_ref, wup_ref, wfc_ref, bfc_ref, wdn_ref, gp_ref, wpg_ref, wpp_ref, gfin_ref,
                st_ref, xo_ref, fst_ref, *scratch, long_seq, tiles_per_seq, s_len, final):
    x = x_ref[...]
    tm = x.shape[0]
    h2 = _rms(x, gf_ref[...]).astype(BF16)
    acc = jnp.zeros((tm, D_MODEL), F32)
    if long_seq:
        _conv_carry_init(st_ref, scratch[0], tiles_per_seq)
    for half in range(D_FF // FF_CHUNK):
        c0 = half * FF_CHUNK
        ua = _dot(h2, wup_ref[:, c0:c0 + FF_CHUNK])
        ub = _dot(h2, wup_ref[:, D_FF + c0:D_FF + c0 + FF_CHUNK])
        if long_seq:
            uac = _conv_long(ua, wfc_ref, c0, scratch[0])
            fst_ref[:, c0:c0 + FF_CHUNK] = ua[tm - 2:, :]
        else:
            uac = _conv_short(ua, wfc_ref, c0, st_ref, s_len)
            fst_ref[:, c0:c0 + FF_CHUNK] = ua
        z = uac + bfc_ref[:, c0:c0 + FF_CHUNK]
        act = (z * _sigmoid(z) * ub).astype(BF16)
        acc = acc + _dot(act, wdn_ref[c0:c0 + FF_CHUNK, :])
    x2 = x + acc
    gate = _sigmoid(_dot(_rms(x2, gp_ref[...]).astype(BF16), wpg_ref[...]))
    x3 = x2 + gate * _dot(pe_ref[...].astype(BF16), wpp_ref[...])
    if final:
        x3 = _rms(x3, gfin_ref[...])
    xo_ref[...] = x3


def _ffn(x, pe, gf, wup, wfc, bfc, wdn, gp, wpg, wpp, gfin, st, nb, t, tm, final):
    n = x.shape[0]
    long_seq = t >= tm
    tiles_per_seq = max(t // tm, 1)
    row = lambda c: pl.BlockSpec((tm, c), lambda i: (i, 0))
    if long_seq:
        st_spec = pl.BlockSpec((None, 2, D_FF), lambda i: (i // tiles_per_seq, 0, 0))
        fst_spec = pl.BlockSpec((None, 2, D_FF), lambda i: (i // tiles_per_seq, 0, 0))
        fst_shape = jax.ShapeDtypeStruct((nb, 2, D_FF), F32)
        scratch = [pltpu.VMEM((2, D_FF), F32)]
    else:
        st_spec = row(D_FF)
        fst_spec = row(D_FF)
        fst_shape = jax.ShapeDtypeStruct((n, D_FF), F32)
        scratch = []
    return pl.pallas_call(
        functools.partial(_ffn_kernel, long_seq=long_seq, tiles_per_seq=tiles_per_seq, s_len=t, final=final),
        grid=(n // tm,),
        in_specs=[row(D_MODEL), row(pe.shape[1]), _const_spec((1, D_MODEL)),
                  _const_spec(wup.shape), _const_spec(wfc.shape), _const_spec(bfc.shape), _const_spec(wdn.shape),
                  _const_spec((1, D_MODEL)), _const_spec(wpg.shape), _const_spec(wpp.shape),
                  _const_spec((1, D_MODEL)), st_spec],
        out_specs=[row(D_MODEL), fst_spec],
        out_shape=[jax.ShapeDtypeStruct((n, D_MODEL), F32), fst_shape],
        scratch_shapes=scratch,
        compiler_params=_cparams("arbitrary"),
        name="ffn",
    )(x, pe, gf, wup, wfc, bfc, wdn, gp, wpg, wpp, gfin, st)


def _row_form(col, nb, t):
    return col[:, :SM_USED].reshape(nb, t, SM_USED).transpose(0, 2, 1)


def _short_fill(state, s_len):
    nb, _, ch = state.shape
    return jnp.pad(state, ((0, 0), (0, s_len - 2), (0, 0))).reshape(nb * s_len, ch)


def _pick(n, pref):
    return pref if n % pref == 0 else n


def _layer(x, pe, w, nb, t, chunk, past, final, kv=None):
    n = x.shape[0]
    tm_in = _pick(n, 512)
    tm_merge = _pick(n, 512)
    tm_ffn = _pick(n, 256)
    if past is None:
        kvp = dict(kv, nb=nb, t=t, wt=w["w_kvt"], bt=w["b_kvt"])
        q, k, mq, mk, mv, so, sm, kt, vt = _inproj(x, w["g_mix"], w["w_am"], w["b_am"], tm_in, kvp)
        col, logf = _gates(sm, t, t, chunk)
        rowt = _row_form(col, nb, t)
        ya = _fox_prompt(q, k, vt, col, rowt, kv["layer"], nb, t, _pick(t, 256))
        k, v = kt, vt
        c0 = jnp.zeros((nb, M_HEADS, M_HEAD_DIM, M_HEAD_DIM), F32)
        n0 = jnp.zeros((nb, M_HEADS, M_HEAD_DIM), F32)
        m0 = jnp.zeros((nb, M_HEADS), F32)
        conv0 = jnp.zeros((nb, 2, C_WIDTH), F32)
        ffn0 = jnp.zeros((nb, 2, D_FF), F32)
    else:
        q, k, v, mq, mk, mv, so, sm = _inproj(x, w["g_mix"], w["w_am"], w["b_am"], tm_in)
        col, logf = _gates(sm, n, t, t)
        rowt = _row_form(col, nb, t)
        ya = _fox_sample(past["page_table"], q, k, v, col, rowt, past["cache_k"], past["cache_v"],
                         past["lc"], past["layer"], nb, t)
        c0, n0, m0, conv0, ffn0 = past["c0"], past["n0"], past["m0"], past["conv0"], past["ffn0"]
    ym, c1, n1, m1 = _mlstm(mq, mk, mv, so, col, rowt, w["g_mhead"], c0, n0, m0, nb, t, chunk)
    long_merge = t >= tm_merge
    long_ffn = t >= tm_ffn
    x1, conv1 = _merge(x, w["g_mix"], w["w_c"], w["b_c"], w["w_g"], w["b_g"], w["w_sconv"], ya, ym,
                       w["w_oa"], w["w_om"], w["w_oc"], w["w_o"],
                       conv0 if long_merge else _short_fill(conv0, t), nb, t, tm_merge)
    x3, ffn1 = _ffn(x1, pe, w["g_ffn"], w["w_up"], w["w_fconv"], w["b_fconv"], w["w_down"],
                    w["g_ple"], w["w_ple_gate"], w["w_ple"], w["g_final"],
                    ffn0 if long_ffn else _short_fill(ffn0, t), nb, t, tm_ffn, final)
    if not long_merge:
        conv1 = conv1.reshape(nb, t, C_WIDTH)[:, t - 2:, :]
    if not long_ffn:
        ffn1 = ffn1.reshape(nb, t, D_FF)[:, t - 2:, :]
    if past is not None:
        k = k.reshape(nb, t, A_HEADS, A_HEAD_DIM)
        v = v.reshape(nb, t, A_HEADS, A_HEAD_DIM)
    state = (k, v, logf.reshape(nb, t, A_HEADS), c1, n1.reshape(nb, M_HEADS, M_HEAD_DIM),
             m1.reshape(nb, M_HEADS), conv1, ffn1)
    return x3, state


def _layer_weights(l, w_in, b_in, g_mix, g_mhead, w_sconv, w_oa, w_om, w_oc, w_o, g_ffn, w_up,
                   w_fconv, b_fconv, w_down, g_ple, w_ple_gate, w_ple, g_final):
    a0 = 0
    af0 = 3 * A_WIDTH
    m0 = af0 + A_HEADS
    mi0 = m0 + 3 * M_WIDTH
    mo0 = mi0 + 2 * M_HEADS
    c0 = mo0 + M_WIDTH
    g0 = c0 + 3 * C_WIDTH
    wl, bl = w_in[l], b_in[l]
    pad = LANES - SM_USED

    def cols(a, lo, hi):
        return a[..., lo:hi]

    def am(a):
        return jnp.concatenate(
            [cols(a, a0, af0), cols(a, m0, mi0), cols(a, mo0, c0),
             cols(a, af0, m0), cols(a, mi0, mo0), jnp.zeros(a.shape[:-1] + (pad,), a.dtype)], axis=-1)

    row = lambda a: a.reshape(1, -1)
    return {
        "w_am": am(wl).astype(BF16), "b_am": row(am(bl)),
        "w_kvt": cols(wl, A_WIDTH, af0).T.astype(BF16), "b_kvt": cols(bl, A_WIDTH, af0).reshape(-1, 1),
        "w_c": cols(wl, c0, g0).astype(BF16), "b_c": row(cols(bl, c0, g0)),
        "w_g": cols(wl, g0, g0 + 3 * D_MODEL).astype(BF16), "b_g": row(cols(bl, g0, g0 + 3 * D_MODEL)),
        "g_mix": row(g_mix[l]), "g_mhead": row(g_mhead[l]), "w_sconv": w_sconv[l],
        "w_oa": w_oa[l].astype(BF16), "w_om": w_om[l].astype(BF16), "w_oc": w_oc[l].astype(BF16),
        "w_o": w_o[l].astype(BF16), "g_ffn": row(g_ffn[l]), "w_up": w_up[l].astype(BF16),
        "w_fconv": w_fconv[l], "b_fconv": row(b_fconv[l]), "w_down": w_down[l].astype(BF16),
        "g_ple": row(g_ple[l]), "w_ple_gate": w_ple_gate[l].astype(BF16), "w_ple": w_ple[l].astype(BF16),
        "g_final": row(g_final),
    }


def kernel(x_prompt, x_sample, cache_k, cache_v, cache_logf, state_mlstm_C, state_mlstm_n, state_mlstm_m, state_conv, state_ffn_conv, page_table, p_prompt, p_sample, w_in, b_in, g_mix, g_mhead, w_sconv, w_oa, w_om, w_oc, w_o, g_ffn, w_up, w_fconv, b_fconv, w_down, g_ple, w_ple_gate, w_ple, g_final):
    nbp, tp, _ = x_prompt.shape
    nbs, ts, _ = x_sample.shape
    depth = w_in.shape[0]
    n_phys, page = cache_logf.shape[0], cache_logf.shape[1]
    assert page == LANES and ts == SUBLANES and depth * A_HEADS == SM_USED

    lf_t = cache_logf.reshape(n_phys, page, depth * A_HEADS).transpose(0, 2, 1).reshape(-1, LANES)
    lc = _pool_scan(lf_t, _pick(lf_t.shape[0], 2048)).reshape(n_phys, depth * A_HEADS, LANES)

    xp = x_prompt.reshape(nbp * tp, D_MODEL)
    xs = x_sample.reshape(nbs * ts, D_MODEL)
    chunk_p = _pick(tp, 256)
    new_p, new_s = [], []
    kvt = None
    for l in range(depth):
        w = _layer_weights(l, w_in, b_in, g_mix, g_mhead, w_sconv, w_oa, w_om, w_oc, w_o, g_ffn, w_up,
                           w_fconv, b_fconv, w_down, g_ple, w_ple_gate, w_ple, g_final)
        final = l == depth - 1
        xp, st_p = _layer(xp, p_prompt[l].reshape(nbp * tp, -1), w, nbp, tp, chunk_p, None, final,
                          kv={"layer": l, "depth": depth, "prev": kvt})
        kvt = st_p[:2]
        past = {"page_table": page_table, "cache_k": cache_k, "cache_v": cache_v, "lc": lc, "layer": l,
                "c0": state_mlstm_C[:, l], "n0": state_mlstm_n[:, l], "m0": state_mlstm_m[:, l],
                "conv0": state_conv[:, l], "ffn0": state_ffn_conv[:, l]}
        xs, st_s = _layer(xs, p_sample[l].reshape(nbs * ts, -1), w, nbs, ts, ts, past, final)
        new_p.append(st_p)
        new_s.append(st_s)

    def stack(per_layer, first):
        rows = [jnp.stack([s[i] for s in per_layer], axis=2) for i in range(first, 3)]
        states = [jnp.stack([s[i] for s in per_layer], axis=1) for i in range(3, 8)]
        return tuple(rows + states)

    kv_p = tuple(a.reshape(nbp, depth, A_HEADS, A_HEAD_DIM, tp).transpose(0, 4, 1, 2, 3) for a in kvt)
    return ((xp.reshape(nbp, tp, D_MODEL), xs.reshape(nbs, ts, D_MODEL)) + kv_p + stack(new_p, 2) + stack(new_s, 0))
```

```python
import functools

import jax
import jax.numpy as jnp
from jax import lax
from jax.experimental import pallas as pl
from jax.experimental.pallas import tpu as pltpu

F32 = jnp.float32
BF16 = jnp.bfloat16

D_MODEL = 1024
A_HEADS = 8
A_HEAD_DIM = 64
A_WIDTH = A_HEADS * A_HEAD_DIM
M_HEADS = 4
M_HEAD_DIM = 128
M_WIDTH = M_HEADS * M_HEAD_DIM
C_WIDTH = 512
D_FF = 2816
EPS = 1e-6
LANES = 128
SUBLANES = 8
VMEM_LIMIT = 56 * 1024 * 1024

SM_AF = 0
SM_MI = 8
SM_MF = 12
SM_USED = 16

NT = (((1,), (1,)), ((), ()))


def _cparams(*sem):
    return pltpu.CompilerParams(dimension_semantics=sem, vmem_limit_bytes=VMEM_LIMIT)


def _const_spec(shape):
    nd = len(shape)
    return pl.BlockSpec(shape, lambda *_: (0,) * nd, pipeline_mode=pl.Buffered(1))


def _rms(x, g):
    return x * lax.rsqrt(jnp.mean(x * x, axis=-1, keepdims=True) + EPS) * g


def _log_sigmoid(x):
    return jnp.minimum(x, 0.0) - jnp.log1p(jnp.exp(-jnp.abs(x)))


def _sigmoid(x):
    return 1.0 / (1.0 + jnp.exp(-x))


def _dot(a, b):
    return jnp.dot(a, b, preferred_element_type=F32)


def _inproj_kernel(*refs, transposed_kv):
    if transposed_kv:
        (x_ref, g_ref, w_ref, b_ref, wt_ref, bt_ref, _, _,
         q_ref, k_ref, mq_ref, mk_ref, mv_ref, so_ref, sm_ref, kt_ref, vt_ref) = refs
    else:
        x_ref, g_ref, w_ref, b_ref, q_ref, k_ref, v_ref, mq_ref, mk_ref, mv_ref, so_ref, sm_ref = refs
    h = _rms(x_ref[...], g_ref[...]).astype(BF16)

    def proj(c0, n):
        return _dot(h, w_ref[:, c0:c0 + n]) + b_ref[:, c0:c0 + n]

    q_ref[...] = (proj(0, A_WIDTH) * (A_HEAD_DIM ** -0.5)).astype(BF16)
    if transposed_kv:
        k_ref[...] = proj(A_WIDTH, A_WIDTH).astype(BF16)
        kt_ref[...] = (lax.dot_general(wt_ref[0:A_WIDTH, :], h, NT, preferred_element_type=F32)
                       + bt_ref[0:A_WIDTH, :])
        vt_ref[...] = (lax.dot_general(wt_ref[A_WIDTH:2 * A_WIDTH, :], h, NT, preferred_element_type=F32)
                       + bt_ref[A_WIDTH:2 * A_WIDTH, :])
    else:
        k_ref[...] = proj(A_WIDTH, A_WIDTH)
        v_ref[...] = proj(2 * A_WIDTH, A_WIDTH)
    o = 3 * A_WIDTH
    mq_ref[...] = proj(o, M_WIDTH).astype(BF16)
    mk_ref[...] = (proj(o + M_WIDTH, M_WIDTH) * (M_HEAD_DIM ** -0.5)).astype(BF16)
    mv_ref[...] = proj(o + 2 * M_WIDTH, M_WIDTH).astype(BF16)
    so_ref[...] = _sigmoid(proj(o + 3 * M_WIDTH, M_WIDTH))
    sm_ref[...] = proj(o + 4 * M_WIDTH, LANES)


def _inproj(x, g, w, b, tm, kv=None):
    n = x.shape[0]
    wcols = w.shape[1]
    row = lambda c: pl.BlockSpec((tm, c), lambda i: (i, 0))
    common = [
        jax.ShapeDtypeStruct((n, M_WIDTH), BF16),
        jax.ShapeDtypeStruct((n, M_WIDTH), BF16),
        jax.ShapeDtypeStruct((n, M_WIDTH), BF16),
        jax.ShapeDtypeStruct((n, M_WIDTH), F32),
        jax.ShapeDtypeStruct((n, LANES), F32),
    ]
    common_specs = [row(M_WIDTH)] * 4 + [row(LANES)]
    in_specs = [row(D_MODEL), _const_spec((1, D_MODEL)), _const_spec((D_MODEL, wcols)), _const_spec((1, wcols))]
    if kv is None:
        return pl.pallas_call(
            functools.partial(_inproj_kernel, transposed_kv=False),
            grid=(n // tm,),
            in_specs=in_specs,
            out_specs=[row(A_WIDTH)] * 3 + common_specs,
            out_shape=[jax.ShapeDtypeStruct((n, A_WIDTH), BF16), jax.ShapeDtypeStruct((n, A_WIDTH), F32),
                       jax.ShapeDtypeStruct((n, A_WIDTH), F32)] + common,
            compiler_params=_cparams("parallel"),
            name="inproj",
        )(x, g, w, b)
    nb, t, depth, layer = kv["nb"], kv["t"], kv["depth"], kv["layer"]
    tps = t // tm
    t_shape = jax.ShapeDtypeStruct((nb, depth, A_WIDTH, t), F32)
    t_spec = pl.BlockSpec((None, None, A_WIDTH, tm), lambda i: (i // tps, layer, 0, i % tps))
    prev = kv["prev"]
    if prev is None:
        prev = (jnp.zeros((SUBLANES, LANES), F32),) * 2
        aliases = {}
    else:
        aliases = {6: 7, 7: 8}
    any_spec = pl.BlockSpec(memory_space=pl.ANY)
    return pl.pallas_call(
        functools.partial(_inproj_kernel, transposed_kv=True),
        grid=(n // tm,),
        in_specs=in_specs + [_const_spec(kv["wt"].shape), _const_spec(kv["bt"].shape), any_spec, any_spec],
        out_specs=[row(A_WIDTH)] * 2 + common_specs + [t_spec, t_spec],
        out_shape=[jax.ShapeDtypeStruct((n, A_WIDTH), BF16), jax.ShapeDtypeStruct((n, A_WIDTH), BF16)]
                  + common + [t_shape, t_shape],
        input_output_aliases=aliases,
        compiler_params=_cparams("parallel"),
        name="inproj_t",
    )(x, g, w, b, kv["wt"], kv["bt"], *prev)


def _gates_kernel(sm_ref, col_ref, logf_ref, *, seq_a, seg_m):
    x = sm_ref[...]
    rows = x.shape[0]
    lane = lax.broadcasted_iota(jnp.int32, (1, LANES), 1)
    row = lax.broadcasted_iota(jnp.int32, (rows, 1), 0)
    ls = _log_sigmoid(x)
    logf_ref[...] = ls[:, SM_AF:SM_AF + A_HEADS]
    is_a = lane < SM_MI
    is_i = (lane >= SM_MI) & (lane < SM_MF)
    is_f = (lane >= SM_MF) & (lane < SM_USED)
    y = jnp.where(is_i, x, ls)
    rmod_a = row & (seq_a - 1)
    rmod_m = row & (seg_m - 1)
    s = 1
    while s < max(seq_a, seg_m):
        take = jnp.zeros((rows, LANES), jnp.bool_)
        if s < seq_a:
            take = take | (is_a & (rmod_a >= s))
        if s < seg_m:
            take = take | (is_f & (rmod_m >= s))
        y = y + jnp.where(take, pltpu.roll(y, s, axis=0), 0.0)
        s *= 2
    col_ref[...] = y


def _gates(sm, rows, seq_a, seg_m):
    n = sm.shape[0]
    return pl.pallas_call(
        functools.partial(_gates_kernel, seq_a=seq_a, seg_m=seg_m),
        grid=(n // rows,),
        in_specs=[pl.BlockSpec((rows, LANES), lambda i: (i, 0))],
        out_specs=[pl.BlockSpec((rows, LANES), lambda i: (i, 0)),
                   pl.BlockSpec((rows, A_HEADS), lambda i: (i, 0))],
        out_shape=[jax.ShapeDtypeStruct((n, LANES), F32),
                   jax.ShapeDtypeStruct((n, A_HEADS), F32)],
        compiler_params=_cparams("parallel"),
        name="gates",
    )(sm)


def _fox_kernel(q_ref, k_ref, vt_ref, col_ref, row_ref, o_ref, vb_ref, ck_ref, *, tq):
    hp = pl.program_id(1)
    qi = pl.program_id(2)
    lane = lax.broadcasted_iota(jnp.int32, (1, LANES), 1)
    hd = A_HEAD_DIM

    @pl.when(qi == 0)
    def _():
        vb_ref[...] = vt_ref[...].astype(BF16)
        colv = col_ref[...]
        for hh in range(2):
            ck = jnp.sum(jnp.where(lane == SM_AF + hp * 2 + hh, colv, 0.0), axis=1, keepdims=True)
            ck_ref[hh] = jnp.broadcast_to(ck, colv.shape)

    q2 = q_ref[...]
    zero = jnp.zeros_like(q2)
    q0 = pl.multiple_of(qi * tq, tq)
    qm = [jnp.where((lane >= hd) == bool(hh), q2, zero) for hh in range(2)]
    cq = [row_ref[pl.ds(SM_AF + hp * 2 + hh, 1), pl.ds(q0, tq)] for hh in range(2)]

    def scores(j):
        start = pl.multiple_of(j * tq, tq)
        kb = k_ref[pl.ds(start, tq), :]
        return [lax.dot_general(kb, qm[hh], NT, preferred_element_type=F32) for hh in range(2)]

    def values(j, ps):
        start = pl.multiple_of(j * tq, tq)
        return [_dot(vb_ref[hh * hd:(hh + 1) * hd, pl.ds(start, tq)], ps[hh]) for hh in range(2)]

    def softmax(j, ss, ms, ls, diag):
        start = pl.multiple_of(j * tq, tq)
        ps, m_out, l_out, alphas = [], [], [], []
        for hh in range(2):
            ck = ck_ref[hh, pl.ds(start, tq), :]
            s = ss[hh] + (cq[hh] - jnp.concatenate([ck] * (tq // LANES), axis=1))
            if diag:
                kpos = lax.broadcasted_iota(jnp.int32, (tq, tq), 0)
                qpos = lax.broadcasted_iota(jnp.int32, (tq, tq), 1)
                s = jnp.where(kpos <= qpos, s, -jnp.inf)
            m_new = jnp.maximum(ms[hh], jnp.max(s, axis=0, keepdims=True))
            alpha = jnp.exp(ms[hh] - m_new)
            p = jnp.exp(s - m_new)
            l_out.append(alpha * ls[hh] + jnp.sum(p, axis=0, keepdims=True))
            m_out.append(m_new)
            alphas.append(alpha)
            ps.append(p.astype(BF16))
        return ps, m_out, l_out, alphas

    def body(j, carry):
        ss, ps_prev, a_prev, ms, ls, accs = carry
        pv = values(jnp.maximum(j - 1, 0), ps_prev)
        ss_next = scores(j + 1)
        ps, ms, ls, alphas = softmax(j, ss, ms, ls, False)
        accs = [a_prev[hh] * accs[hh] + pv[hh] for hh in range(2)]
        return ss_next, ps, alphas, ms, ls, accs

    two = lambda a: [a, a]
    init = (scores(0), two(jnp.zeros((tq, tq), BF16)), two(jnp.ones((1, tq), F32)),
            two(jnp.full((1, tq), -jnp.inf, F32)), two(jnp.zeros((1, tq), F32)), two(jnp.zeros((hd, tq), F32)))
    ss, ps_prev, a_prev, ms, ls, accs = lax.fori_loop(0, qi, body, init)
    pv = values(jnp.maximum(qi - 1, 0), ps_prev)
    ps, ms, ls, alphas = softmax(qi, ss, ms, ls, True)
    pd = values(qi, ps)
    yt = [(alphas[hh] * (a_prev[hh] * accs[hh] + pv[hh]) + pd[hh]) / ls[hh] for hh in range(2)]
    o_ref[...] = jnp.concatenate(yt, axis=0).T.astype(o_ref.dtype)


def _fox_prompt(q, k, vt, col, rowt, layer, nb, t, tq):
    n = q.shape[0]
    nq = t // tq
    pairs = A_WIDTH // LANES
    return pl.pallas_call(
        functools.partial(_fox_kernel, tq=tq),
        grid=(nb, pairs, nq),
        in_specs=[
            pl.BlockSpec((tq, LANES), lambda b, hp, qi: (b * nq + qi, hp)),
            pl.BlockSpec((t, LANES), lambda b, hp, qi: (b, hp)),
            pl.BlockSpec((None, None, LANES, t), lambda b, hp, qi: (b, layer, hp, 0)),
            pl.BlockSpec((t, LANES), lambda b, hp, qi: (b, 0)),
            pl.BlockSpec((None, SM_USED, t), lambda b, hp, qi: (b, 0, 0)),
        ],
        out_specs=pl.BlockSpec((tq, LANES), lambda b, hp, qi: (b * nq + qi, hp)),
        out_shape=jax.ShapeDtypeStruct((n, A_WIDTH), BF16),
        scratch_shapes=[pltpu.VMEM((LANES, t), BF16), pltpu.VMEM((2, t, LANES), F32)],
        compiler_params=_cparams("parallel", "parallel", "arbitrary"),
        name="fox_prompt",
    )(q, k, vt, col, rowt)


def _pool_scan_kernel(x_ref, o_ref):
    y = x_ref[...]
    lane = lax.broadcasted_iota(jnp.int32, (1, LANES), 1)
    s = 1
    while s < LANES:
        y = y + jnp.where(lane >= s, pltpu.roll(y, s, axis=1), 0.0)
        s *= 2
    o_ref[...] = y


def _pool_scan(x, rows):
    n = x.shape[0]
    return pl.pallas_call(
        _pool_scan_kernel,
        grid=(n // rows,),
        in_specs=[pl.BlockSpec((rows, LANES), lambda i: (i, 0))],
        out_specs=pl.BlockSpec((rows, LANES), lambda i: (i, 0)),
        out_shape=jax.ShapeDtypeStruct((n, LANES), F32),
        compiler_params=_cparams("parallel"),
        name="pool_scan",
    )(x)


def _fox_sample_kernel(pt_ref, q_ref, kn_ref, vn_ref, col_ref, rown_ref, *rest,
                       n_pages, layer, s_len):
    del pt_ref
    k_refs = rest[:n_pages]
    v_refs = rest[n_pages:2 * n_pages]
    lc_refs = rest[2 * n_pages:3 * n_pages]
    o_ref = rest[3 * n_pages]
    s_ref = rest[3 * n_pages + 1]
    nh = A_HEADS
    rows = nh * s_len
    lane_w = lax.broadcasted_iota(jnp.int32, (1, A_WIDTH), 1)
    rid = lax.broadcasted_iota(jnp.int32, (rows, 1), 0)
    own = jnp.right_shift(lane_w, 6) == jnp.right_shift(rid, 3)
    q = q_ref[...].astype(F32)
    qbd = jnp.where(own, jnp.concatenate([q] * nh, axis=0), 0.0)
    colv = col_ref[...]
    rown = rown_ref[...]
    lo = layer * nh

    tot = [lc_refs[p][lo:lo + nh, LANES - 1:LANES] for p in range(n_pages)]
    base = [None] * n_pages
    suf = jnp.zeros((nh, 1), F32)
    for p in range(n_pages - 1, -1, -1):
        suf = suf + tot[p]
        base[p] = suf

    m_run = [None] * nh
    for p in range(n_pages):
        sp = _dot(qbd, k_refs[p][...])
        for h in range(nh):
            sl = slice(h * s_len, (h + 1) * s_len)
            cn = colv[:, SM_AF + h:SM_AF + h + 1]
            bias = (base[p][h:h + 1, :] - lc_refs[p][lo + h:lo + h + 1, :]) + cn
            sh = sp[sl, :] + bias
            s_ref[sl, p * LANES:(p + 1) * LANES] = sh
            mx = jnp.max(sh, axis=1, keepdims=True)
            m_run[h] = mx if m_run[h] is None else jnp.maximum(m_run[h], mx)
    sn = lax.dot_general(qbd, kn_ref[...], NT, preferred_element_type=F32)
    r = lax.broadcasted_iota(jnp.int32, (s_len, s_len), 0)
    c = lax.broadcasted_iota(jnp.int32, (s_len, s_len), 1)
    pn, m_all = [], []
    for h in range(nh):
        sl = slice(h * s_len, (h + 1) * s_len)
        cn = colv[:, SM_AF + h:SM_AF + h + 1]
        sh = sn[sl, :] + (cn - rown[SM_AF + h:SM_AF + h + 1, :])
        sh = jnp.where(c <= r, sh, -jnp.inf)
        pn.append(sh)
        m_all.append(jnp.maximum(m_run[h], jnp.max(sh, axis=1, keepdims=True)))
    m = jnp.concatenate(m_all, axis=0)
    p_new = jnp.exp(jnp.concatenate(pn, axis=0) - m)
    l = jnp.sum(p_new, axis=1, keepdims=True)
    acc = _dot(p_new, vn_ref[...])
    for p in range(n_pages):
        pp = jnp.exp(s_ref[:, p * LANES:(p + 1) * LANES] - m)
        l = l + jnp.sum(pp, axis=1, keepdims=True)
        acc = acc + lax.dot_general(pp, v_refs[p][...], NT, preferred_element_type=F32)
    acc = jnp.where(own, acc / l, 0.0)
    y = acc[0:s_len, :]
    for h in range(1, nh):
        y = y + acc[h * s_len:(h + 1) * s_len, :]
    o_ref[...] = y.astype(o_ref.dtype)


def _fox_sample(page_table, q, k_new, v_new, col, rown, cache_k, cache_v, lc, layer, nb, s_len):
    n_pages = page_table.shape[1]
    n_phys, page = cache_k.shape[0], cache_k.shape[1]
    depth = cache_k.shape[2]
    ck = cache_k.transpose(0, 2, 3, 4, 1).reshape(n_phys, depth, A_WIDTH, page)
    cv = cache_v.transpose(0, 2, 3, 4, 1).reshape(n_phys, depth, A_WIDTH, page)
    tok = lambda c: pl.BlockSpec((s_len, c), lambda b, pt: (b, 0))
    page_specs = [pl.BlockSpec((None, None, A_WIDTH, page),
                               lambda b, pt, j=j: (pt[b, j], layer, 0, 0)) for j in range(n_pages)]
    lc_specs = [pl.BlockSpec((None, 2 * A_HEADS, LANES),
                             lambda b, pt, j=j: (pt[b, j], 0, 0)) for j in range(n_pages)]
    grid_spec = pltpu.PrefetchScalarGridSpec(
        num_scalar_prefetch=1,
        grid=(nb,),
        in_specs=[tok(A_WIDTH), tok(A_WIDTH), tok(A_WIDTH), tok(LANES),
                  pl.BlockSpec((None, SM_USED, s_len), lambda b, pt: (b, 0, 0))]
                 + page_specs + page_specs + lc_specs,
        out_specs=tok(A_WIDTH),
        scratch_shapes=[pltpu.VMEM((A_HEADS * s_len, n_pages * LANES), F32)],
    )
    return pl.pallas_call(
        functools.partial(_fox_sample_kernel, n_pages=n_pages, layer=layer, s_len=s_len),
        grid_spec=grid_spec,
        out_shape=jax.ShapeDtypeStruct((nb * s_len, A_WIDTH), BF16),
        compiler_params=_cparams("arbitrary"),
        name="fox_sample",
    )(page_table, q, k_new, v_new, col, rown, *([ck] * n_pages), *([cv] * n_pages), *([lc] * n_pages))


def _mlstm_kernel(q_ref, k_ref, v_ref, so_ref, col_ref, row_ref, g_ref, c0_ref, n0_ref, m0_ref, _,
                  y_ref, c1_ref, n1_ref, m1_ref, *, t, chunk, bb, lowp):
    nh, hd = M_HEADS, M_HEAD_DIM
    r = lax.broadcasted_iota(jnp.int32, (chunk, chunk), 0)
    c = lax.broadcasted_iota(jnp.int32, (chunk, chunk), 1)
    causal = c <= r
    g = g_ref[...]
    op = (lambda a: a.astype(BF16)) if lowp else (lambda a: a.astype(F32))
    heads = range(nh)
    hs = lambda h: slice(h * hd, (h + 1) * hd)

    def chunk_body(bi, ci, carry):
        cs, ns, ms = carry
        r0 = pl.multiple_of(bi * t + ci * chunk, chunk)
        l0 = pl.multiple_of(ci * chunk, chunk)
        qa = q_ref[pl.ds(r0, chunk), :]
        ka = k_ref[pl.ds(r0, chunk), :]
        va = v_ref[pl.ds(r0, chunk), :]
        colv = col_ref[pl.ds(r0, chunk), :]
        rows = row_ref[bi, :, pl.ds(l0, chunk)]
        qs = [op(qa[:, hs(h)]) for h in heads]
        ks = [op(ka[:, hs(h)]) for h in heads]
        vs = [op(va[:, hs(h)]) for h in heads]
        qk = [lax.dot_general(qs[h], ks[h], NT, preferred_element_type=F32) for h in heads]
        qc = [lax.dot_general(qs[h], op(cs[h]), NT, preferred_element_type=F32) for h in heads]
        wqk, a_, m_t_, fcol_, icol_ = [], [], [], [], []
        for h in heads:
            f_col = colv[:, SM_MF + h:SM_MF + h + 1]
            i_col = colv[:, SM_MI + h:SM_MI + h + 1]
            f_row = rows[SM_MF + h:SM_MF + h + 1, :]
            i_row = rows[SM_MI + h:SM_MI + h + 1, :]
            d = jnp.where(causal, f_col + (i_row - f_row), -jnp.inf)
            inter = ms[h] + f_col
            m_t = jnp.maximum(inter, jnp.max(d, axis=1, keepdims=True))
            wqk.append(jnp.exp(d - m_t) * qk[h])
            a_.append(jnp.exp(inter - m_t))
            m_t_.append(m_t)
            fcol_.append(f_col)
            icol_.append(i_col)
        pv = [_dot(op(wqk[h]), vs[h]) for h in heads]
        ys, cs_new, ns_new, ms_new = [], [], [], []
        for h in heads:
            a, m_t = a_[h], m_t_[h]
            qf = qa[:, hs(h)].astype(F32)
            num = a * qc[h] + pv[h]
            den = a * jnp.sum(qf * ns[h], axis=1, keepdims=True) + jnp.sum(wqk[h], axis=1, keepdims=True)
            h_t = num / jnp.maximum(jnp.abs(den), jnp.exp(-m_t))
            ys.append(h_t * lax.rsqrt(jnp.mean(h_t * h_t, axis=1, keepdims=True) + EPS))
            m_last = m_t[chunk - 1:chunk, :]
            a_last = a[chunk - 1:chunk, :]
            w_last = jnp.exp((icol_[h] - fcol_[h]) + (fcol_[h][chunk - 1:chunk, :] - m_last))
            vw = va[:, hs(h)].astype(F32) * w_last
            cs_new.append(a_last * cs[h] + _dot(op(vw.T), ks[h]))
            ns_new.append(a_last * ns[h] + jnp.sum(ka[:, hs(h)].astype(F32) * w_last, axis=0, keepdims=True))
            ms_new.append(m_last)
        hn = jnp.concatenate(ys, axis=1) * g
        y_ref[pl.ds(r0, chunk), :] = (so_ref[pl.ds(r0, chunk), :] * hn).astype(y_ref.dtype)
        return cs_new, ns_new, ms_new

    def seq_body(bi, _):
        init = ([c0_ref[bi, h] for h in heads], [n0_ref[bi, h] for h in heads], [m0_ref[bi, h] for h in heads])
        if t == chunk:
            cs, ns, ms = chunk_body(bi, 0, init)
        else:
            cs, ns, ms = lax.fori_loop(0, t // chunk, lambda ci, cr: chunk_body(bi, ci, cr), init)
        for h in heads:
            c1_ref[bi, h] = cs[h]
            n1_ref[bi, h] = ns[h]
            m1_ref[bi, h] = ms[h]
        return 0

    if bb == 1:
        seq_body(0, 0)
    else:
        lax.fori_loop(0, bb, seq_body, 0)


def _mlstm(q, k, v, so, col, rowt, g, c0, n0, m0, layer_in, c_prev, layer_out, depth, nb, t, chunk, bb):
    n = q.shape[0]
    nh, hd = M_HEADS, M_HEAD_DIM
    seq = lambda w: pl.BlockSpec((bb * t, w), lambda i: (i, 0))
    st_in = lambda a, b_: pl.BlockSpec((bb, None, nh, a, b_), lambda i: (i, layer_in, 0, 0, 0))
    st_out = lambda a, b_: pl.BlockSpec((bb, nh, a, b_), lambda i: (i, 0, 0, 0))
    if c_prev is None:
        c_prev = jnp.zeros((SUBLANES, LANES), F32)
        aliases = {}
    else:
        aliases = {10: 1}
    return pl.pallas_call(
        functools.partial(_mlstm_kernel, t=t, chunk=chunk, bb=bb, lowp=chunk >= 16),
        grid=(nb // bb,),
        in_specs=[seq(M_WIDTH), seq(M_WIDTH), seq(M_WIDTH), seq(M_WIDTH), seq(LANES),
                  pl.BlockSpec((bb, SM_USED, t), lambda i: (i, 0, 0)),
                  _const_spec((1, M_WIDTH)),
                  st_in(hd, hd), st_in(1, hd), st_in(1, 1),
                  pl.BlockSpec(memory_space=pl.ANY)],
        out_specs=[seq(M_WIDTH),
                   pl.BlockSpec((bb, None, nh, hd, hd), lambda i: (i, layer_out, 0, 0, 0)),
                   st_out(1, hd), st_out(1, 1)],
        out_shape=[jax.ShapeDtypeStruct((n, M_WIDTH), BF16),
                   jax.ShapeDtypeStruct((nb, depth, nh, hd, hd), F32),
                   jax.ShapeDtypeStruct((nb, nh, 1, hd), F32),
                   jax.ShapeDtypeStruct((nb, nh, 1, 1), F32)],
        input_output_aliases=aliases,
        compiler_params=_cparams("parallel"),
        name="mlstm",
    )(q, k, v, so, col, rowt, g, c0, n0.reshape(n0.shape[:3] + (1, hd)), m0.reshape(m0.shape[:3] + (1, 1)), c_prev)


def _conv_carry_init(state_ref, carry_ref, tiles_per_seq):
    @pl.when((pl.program_id(0) % tiles_per_seq) == 0)
    def _():
        carry_ref[...] = state_ref[...]


def _conv_long(u, w_ref, c0, carry_ref):
    tm, ch = u.shape
    prev = carry_ref[:, c0:c0 + ch]
    row = lax.broadcasted_iota(jnp.int32, (tm, 1), 0)
    p1 = jnp.where(row == 0, prev[1:2, :], pltpu.roll(u, 1, axis=0))
    p2 = jnp.where(row == 0, prev[0:1, :], jnp.where(row == 1, prev[1:2, :], pltpu.roll(u, 2, axis=0)))
    carry_ref[:, c0:c0 + ch] = u[tm - 2:tm, :]
    return p2 * w_ref[0:1, c0:c0 + ch] + p1 * w_ref[1:2, c0:c0 + ch] + u * w_ref[2:3, c0:c0 + ch]


def _conv_short(u, w_ref, c0, fill_ref, s_len):
    tm, ch = u.shape
    fill = fill_ref[:, c0:c0 + ch]
    rmod = lax.broadcasted_iota(jnp.int32, (tm, 1), 0) & (s_len - 1)
    p1 = jnp.where(rmod >= 1, pltpu.roll(u, 1, axis=0), pltpu.roll(fill, tm - 1, axis=0))
    p2 = jnp.where(rmod >= 2, pltpu.roll(u, 2, axis=0), fill)
    return p2 * w_ref[0:1, c0:c0 + ch] + p1 * w_ref[1:2, c0:c0 + ch] + u * w_ref[2:3, c0:c0 + ch]


def _merge_kernel(x_ref, g_ref, wc_ref, bc_ref, wg_ref, bg_ref, ws_ref, ya_ref, ym_ref,
                  woa_ref, wom_ref, woc_ref, wo_ref, st_ref, x1_ref, cst_ref, *scratch,
                  long_seq, tiles_per_seq, s_len):
    x = x_ref[...]
    h = _rms(x, g_ref[...]).astype(BF16)
    cw = C_WIDTH
    cb = _dot(h, wc_ref[:, 0:cw]) + bc_ref[:, 0:cw]
    u = (_dot(h, wc_ref[:, cw:2 * cw]) + bc_ref[:, cw:2 * cw]) * (_dot(h, wc_ref[:, 2 * cw:3 * cw]) + bc_ref[:, 2 * cw:3 * cw])
    if long_seq:
        _conv_carry_init(st_ref, scratch[0], tiles_per_seq)
        uc = _conv_long(u, ws_ref, 0, scratch[0])
        cst_ref[...] = u[u.shape[0] - 2:, :]
    else:
        uc = _conv_short(u, ws_ref, 0, st_ref, s_len)
        cst_ref[...] = u
    yc = (cb * uc).astype(BF16)
    d = D_MODEL

    def gate(i):
        return _sigmoid(_dot(h, wg_ref[:, i * d:(i + 1) * d]) + bg_ref[:, i * d:(i + 1) * d])

    merged = gate(0) * _dot(ya_ref[...], woa_ref[...])
    merged = merged + gate(1) * _dot(ym_ref[...], wom_ref[...])
    merged = merged + gate(2) * _dot(yc, woc_ref[...])
    x1_ref[...] = x + _dot(merged.astype(BF16), wo_ref[...])


def _merge(x, g, wc, bc, wg, bg, ws, ya, ym, woa, wom, woc, wo, st, nb, t, tm):
    n = x.shape[0]
    long_seq = t >= tm
    tiles_per_seq = max(t // tm, 1)
    row = lambda c: pl.BlockSpec((tm, c), lambda i: (i, 0))
    if long_seq:
        st_spec = pl.BlockSpec((None, 2, C_WIDTH), lambda i: (i // tiles_per_seq, 0, 0))
        cst_spec = pl.BlockSpec((None, 2, C_WIDTH), lambda i: (i // tiles_per_seq, 0, 0))
        cst_shape = jax.ShapeDtypeStruct((nb, 2, C_WIDTH), F32)
        scratch = [pltpu.VMEM((2, C_WIDTH), F32)]
    else:
        st_spec = row(C_WIDTH)
        cst_spec = row(C_WIDTH)
        cst_shape = jax.ShapeDtypeStruct((n, C_WIDTH), F32)
        scratch = []
    return pl.pallas_call(
        functools.partial(_merge_kernel, long_seq=long_seq, tiles_per_seq=tiles_per_seq, s_len=t),
        grid=(n // tm,),
        in_specs=[row(D_MODEL), _const_spec((1, D_MODEL)),
                  _const_spec(wc.shape), _const_spec(bc.shape), _const_spec(wg.shape), _const_spec(bg.shape),
                  _const_spec(ws.shape), row(A_WIDTH), row(M_WIDTH),
                  _const_spec(woa.shape), _const_spec(wom.shape), _const_spec(woc.shape), _const_spec(wo.shape),
                  st_spec],
        out_specs=[row(D_MODEL), cst_spec],
        out_shape=[jax.ShapeDtypeStruct((n, D_MODEL), F32), cst_shape],
        scratch_shapes=scratch,
        compiler_params=_cparams("arbitrary"),
        name="merge",
    )(x, g, wc, bc, wg, bg, ws, ya, ym, woa, wom, woc, wo, st)


FF_CHUNK = D_FF // 2


def _ffn_kernel(x_ref, pe_ref, gf_ref, wup_ref, wfc_ref, bfc_ref, wdn_ref, gp_ref, wpg_ref, wpp_ref, gfin_ref,
                st_ref, xo_ref, fst_ref, *scratch, long_seq, tiles_per_seq, s_len, final):
    x = x_ref[...]
    tm = x.shape[0]
    h2 = _rms(x, gf_ref[...]).astype(BF16)
    acc = jnp.zeros((tm, D_MODEL), F32)
    if long_seq:
        _conv_carry_init(st_ref, scratch[0], tiles_per_seq)
    for half in range(D_FF // FF_CHUNK):
        c0 = half * FF_CHUNK
        ua = _dot(h2, wup_ref[:, c0:c0 + FF_CHUNK])
        ub = _dot(h2, wup_ref[:, D_FF + c0:D_FF + c0 + FF_CHUNK])
        if long_seq:
            uac = _conv_long(ua, wfc_ref, c0, scratch[0])
            fst_ref[:, c0:c0 + FF_CHUNK] = ua[tm - 2:, :]
        else:
            uac = _conv_short(ua, wfc_ref, c0, st_ref, s_len)
            fst_ref[:, c0:c0 + FF_CHUNK] = ua
        z = uac + bfc_ref[:, c0:c0 + FF_CHUNK]
        act = (z * _sigmoid(z) * ub).astype(BF16)
        acc = acc + _dot(act, wdn_ref[c0:c0 + FF_CHUNK, :])
    x2 = x + acc
    gate = _sigmoid(_dot(_rms(x2, gp_ref[...]).astype(BF16), wpg_ref[...]))
    x3 = x2 + gate * _dot(pe_ref[...].astype(BF16), wpp_ref[...])
    if final:
        x3 = _rms(x3, gfin_ref[...])
    xo_ref[...] = x3


def _ffn(x, pe, gf, wup, wfc, bfc, wdn, gp, wpg, wpp, gfin, st, nb, t, tm, final):
    n = x.shape[0]
    long_seq = t >= tm
    tiles_per_seq = max(t // tm, 1)
    row = lambda c: pl.BlockSpec((tm, c), lambda i: (i, 0))
    if long_seq:
        st_spec = pl.BlockSpec((None, 2, D_FF), lambda i: (i // tiles_per_seq, 0, 0))
        fst_spec = pl.BlockSpec((None, 2, D_FF), lambda i: (i // tiles_per_seq, 0, 0))
        fst_shape = jax.ShapeDtypeStruct((nb, 2, D_FF), F32)
        scratch = [pltpu.VMEM((2, D_FF), F32)]
    else:
        st_spec = row(D_FF)
        fst_spec = row(D_FF)
        fst_shape = jax.ShapeDtypeStruct((n, D_FF), F32)
        scratch = []
    return pl.pallas_call(
        functools.partial(_ffn_kernel, long_seq=long_seq, tiles_per_seq=tiles_per_seq, s_len=t, final=final),
        grid=(n // tm,),
        in_specs=[row(D_MODEL), row(pe.shape[1]), _const_spec((1, D_MODEL)),
                  _const_spec(wup.shape), _const_spec(wfc.shape), _const_spec(bfc.shape), _const_spec(wdn.shape),
                  _const_spec((1, D_MODEL)), _const_spec(wpg.shape), _const_spec(wpp.shape),
                  _const_spec((1, D_MODEL)), st_spec],
        out_specs=[row(D_MODEL), fst_spec],
        out_shape=[jax.ShapeDtypeStruct((n, D_MODEL), F32), fst_shape],
        scratch_shapes=scratch,
        compiler_params=_cparams("arbitrary"),
        name="ffn",
    )(x, pe, gf, wup, wfc, bfc, wdn, gp, wpg, wpp, gfin, st)


def _row_form(col, nb, t):
    return col[:, :SM_USED].reshape(nb, t, SM_USED).transpose(0, 2, 1)


def _short_fill(state, s_len):
    nb, _, ch = state.shape
    return jnp.pad(state, ((0, 0), (0, s_len - 2), (0, 0))).reshape(nb * s_len, ch)


def _pick(n, pref):
    return pref if n % pref == 0 else n


def _layer(x, pe, w, nb, t, chunk, past, final, carry):
    n = x.shape[0]
    layer, depth = carry["layer"], carry["depth"]
    tm_in = _pick(n, 512)
    tm_merge = _pick(n, 512)
    tm_ffn = _pick(n, 256)
    if past is None:
        kvp = dict(layer=layer, depth=depth, prev=carry["kvt"], nb=nb, t=t, wt=w["w_kvt"], bt=w["b_kvt"])
        q, k, mq, mk, mv, so, sm, kt, vt = _inproj(x, w["g_mix"], w["w_am"], w["b_am"], tm_in, kvp)
        col, logf = _gates(sm, t, t, chunk)
        rowt = _row_form(col, nb, t)
        ya = _fox_prompt(q, k, vt, col, rowt, layer, nb, t, _pick(t, 256))
        k, v = kt, vt
        c0 = jnp.zeros((nb, 1, M_HEADS, M_HEAD_DIM, M_HEAD_DIM), F32)
        n0 = jnp.zeros((nb, 1, M_HEADS, M_HEAD_DIM), F32)
        m0 = jnp.zeros((nb, 1, M_HEADS), F32)
        layer_in, bb = 0, 1
        conv0 = jnp.zeros((nb, 2, C_WIDTH), F32)
        ffn0 = jnp.zeros((nb, 2, D_FF), F32)
    else:
        q, k, v, mq, mk, mv, so, sm = _inproj(x, w["g_mix"], w["w_am"], w["b_am"], tm_in)
        col, logf = _gates(sm, n, t, t)
        rowt = _row_form(col, nb, t)
        ya = _fox_sample(past["page_table"], q, k, v, col, rowt, past["cache_k"], past["cache_v"],
                         past["lc"], layer, nb, t)
        c0, n0, m0, conv0, ffn0 = past["c0"], past["n0"], past["m0"], past["conv0"], past["ffn0"]
        layer_in, bb = layer, _pick(nb, SUBLANES)
    ym, c1, n1, m1 = _mlstm(mq, mk, mv, so, col, rowt, w["g_mhead"], c0, n0, m0, layer_in, carry["c"], layer,
                            depth, nb, t, chunk, bb)
    long_merge = t >= tm_merge
    long_ffn = t >= tm_ffn
    x1, conv1 = _merge(x, w["g_mix"], w["w_c"], w["b_c"], w["w_g"], w["b_g"], w["w_sconv"], ya, ym,
                       w["w_oa"], w["w_om"], w["w_oc"], w["w_o"],
                       conv0 if long_merge else _short_fill(conv0, t), nb, t, tm_merge)
    x3, ffn1 = _ffn(x1, pe, w["g_ffn"], w["w_up"], w["w_fconv"], w["b_fconv"], w["w_down"],
                    w["g_ple"], w["w_ple_gate"], w["w_ple"], w["g_final"],
                    ffn0 if long_ffn else _short_fill(ffn0, t), nb, t, tm_ffn, final)
    if not long_merge:
        conv1 = conv1.reshape(nb, t, C_WIDTH)[:, t - 2:, :]
    if not long_ffn:
        ffn1 = ffn1.reshape(nb, t, D_FF)[:, t - 2:, :]
    if past is not None:
        k = k.reshape(nb, t, A_HEADS, A_HEAD_DIM)
        v = v.reshape(nb, t, A_HEADS, A_HEAD_DIM)
    state = (k, v, logf.reshape(nb, t, A_HEADS), c1, n1.reshape(nb, M_HEADS, M_HEAD_DIM),
             m1.reshape(nb, M_HEADS), conv1, ffn1)
    return x3, state


def _layer_weights(l, w_in, b_in, g_mix, g_mhead, w_sconv, w_oa, w_om, w_oc, w_o, g_ffn, w_up,
                   w_fconv, b_fconv, w_down, g_ple, w_ple_gate, w_ple, g_final):
    a0 = 0
    af0 = 3 * A_WIDTH
    m0 = af0 + A_HEADS
    mi0 = m0 + 3 * M_WIDTH
    mo0 = mi0 + 2 * M_HEADS
    c0 = mo0 + M_WIDTH
    g0 = c0 + 3 * C_WIDTH
    wl, bl = w_in[l], b_in[l]
    pad = LANES - SM_USED

    def cols(a, lo, hi):
        return a[..., lo:hi]

    def am(a):
        return jnp.concatenate(
            [cols(a, a0, af0), cols(a, m0, mi0), cols(a, mo0, c0),
             cols(a, af0, m0), cols(a, mi0, mo0), jnp.zeros(a.shape[:-1] + (pad,), a.dtype)], axis=-1)

    row = lambda a: a.reshape(1, -1)
    return {
        "w_am": am(wl).astype(BF16), "b_am": row(am(bl)),
        "w_kvt": cols(wl, A_WIDTH, af0).T.astype(BF16), "b_kvt": cols(bl, A_WIDTH, af0).reshape(-1, 1),
        "w_c": cols(wl, c0, g0).astype(BF16), "b_c": row(cols(bl, c0, g0)),
        "w_g": cols(wl, g0, g0 + 3 * D_MODEL).astype(BF16), "b_g": row(cols(bl, g0, g0 + 3 * D_MODEL)),
        "g_mix": row(g_mix[l]), "g_mhead": row(g_mhead[l]), "w_sconv": w_sconv[l],
        "w_oa": w_oa[l].astype(BF16), "w_om": w_om[l].astype(BF16), "w_oc": w_oc[l].astype(BF16),
        "w_o": w_o[l].astype(BF16), "g_ffn": row(g_ffn[l]), "w_up": w_up[l].astype(BF16),
        "w_fconv": w_fconv[l], "b_fconv": row(b_fconv[l]), "w_down": w_down[l].astype(BF16),
        "g_ple": row(g_ple[l]), "w_ple_gate": w_ple_gate[l].astype(BF16), "w_ple": w_ple[l].astype(BF16),
        "g_final": row(g_final),
    }


def kernel(x_prompt, x_sample, cache_k, cache_v, cache_logf, state_mlstm_C, state_mlstm_n, state_mlstm_m, state_conv, state_ffn_conv, page_table, p_prompt, p_sample, w_in, b_in, g_mix, g_mhead, w_sconv, w_oa, w_om, w_oc, w_o, g_ffn, w_up, w_fconv, b_fconv, w_down, g_ple, w_ple_gate, w_ple, g_final):
    nbp, tp, _ = x_prompt.shape
    nbs, ts, _ = x_sample.shape
    depth = w_in.shape[0]
    n_phys, page = cache_logf.shape[0], cache_logf.shape[1]
    assert page == LANES and ts == SUBLANES and depth * A_HEADS == SM_USED

    lf_t = cache_logf.reshape(n_phys, page, depth * A_HEADS).transpose(0, 2, 1).reshape(-1, LANES)
    lc = _pool_scan(lf_t, _pick(lf_t.shape[0], 2048)).reshape(n_phys, depth * A_HEADS, LANES)

    xp = x_prompt.reshape(nbp * tp, D_MODEL)
    xs = x_sample.reshape(nbs * ts, D_MODEL)
    chunk_p = _pick(tp, 256)
    new_p, new_s = [], []
    kvt = c_p = c_s = None
    for l in range(depth):
        w = _layer_weights(l, w_in, b_in, g_mix, g_mhead, w_sconv, w_oa, w_om, w_oc, w_o, g_ffn, w_up,
                           w_fconv, b_fconv, w_down, g_ple, w_ple_gate, w_ple, g_final)
        final = l == depth - 1
        xp, st_p = _layer(xp, p_prompt[l].reshape(nbp * tp, -1), w, nbp, tp, chunk_p, None, final,
                          {"layer": l, "depth": depth, "kvt": kvt, "c": c_p})
        kvt, c_p = st_p[:2], st_p[3]
        past = {"page_table": page_table, "cache_k": cache_k, "cache_v": cache_v, "lc": lc,
                "c0": state_mlstm_C, "n0": state_mlstm_n, "m0": state_mlstm_m,
                "conv0": state_conv[:, l], "ffn0": state_ffn_conv[:, l]}
        xs, st_s = _layer(xs, p_sample[l].reshape(nbs * ts, -1), w, nbs, ts, ts, past, final,
                          {"layer": l, "depth": depth, "kvt": None, "c": c_s})
        c_s = st_s[3]
        new_p.append(st_p)
        new_s.append(st_s)

    def stack(per_layer, c_all, first):
        rows = [jnp.stack([s[i] for s in per_layer], axis=2) for i in range(first, 3)]
        states = [jnp.stack([s[i] for s in per_layer], axis=1) for i in range(4, 8)]
        return tuple(rows + [c_all] + states)

    kv_p = tuple(a.reshape(nbp, depth, A_HEADS, A_HEAD_DIM, tp).transpose(0, 4, 1, 2, 3) for a in kvt)
    return ((xp.reshape(nbp, tp, D_MODEL), xs.reshape(nbs, ts, D_MODEL)) + kv_p + stack(new_p, c_p, 2)
            + stack(new_s, c_s, 0))
```

```python
import functools

import jax
import jax.numpy as jnp
from jax import lax
from jax.experimental import pallas as pl
from jax.experimental.pallas import tpu as pltpu

F32 = jnp.float32
BF16 = jnp.bfloat16

D_MODEL = 1024
A_HEADS = 8
A_HEAD_DIM = 64
A_WIDTH = A_HEADS * A_HEAD_DIM
M_HEADS = 4
M_HEAD_DIM = 128
M_WIDTH = M_HEADS * M_HEAD_DIM
C_WIDTH = 512
D_FF = 2816
EPS = 1e-6
LANES = 128
SUBLANES = 8
VMEM_LIMIT = 56 * 1024 * 1024

SM_AF = 0
SM_MI = 8
SM_MF = 12
SM_USED = 16
SM_MG = 16
SM_ROWS = 24

NT = (((1,), (1,)), ((), ()))


def _cparams(*sem):
    return pltpu.CompilerParams(dimension_semantics=sem, vmem_limit_bytes=VMEM_LIMIT)


def _const_spec(shape):
    nd = len(shape)
    return pl.BlockSpec(shape, lambda *_: (0,) * nd, pipeline_mode=pl.Buffered(1))


def _rms(x, g):
    return x * lax.rsqrt(jnp.mean(x * x, axis=-1, keepdims=True) + EPS) * g


def _log_sigmoid(x):
    return jnp.minimum(x, 0.0) - jnp.log1p(jnp.exp(-jnp.abs(x)))


def _sigmoid(x):
    return 1.0 / (1.0 + jnp.exp(-x))


def _dot(a, b):
    return jnp.dot(a, b, preferred_element_type=F32)


def _inproj_kernel(*refs, transposed_kv):
    if transposed_kv:
        (x_ref, g_ref, w_ref, b_ref, wt_ref, bt_ref, _, _,
         q_ref, k_ref, mq_ref, mk_ref, mv_ref, so_ref, sm_ref, kt_ref, vt_ref, mvt_ref) = refs
    else:
        x_ref, g_ref, w_ref, b_ref, q_ref, k_ref, v_ref, mq_ref, mk_ref, mv_ref, so_ref, sm_ref = refs
    h = _rms(x_ref[...], g_ref[...]).astype(BF16)

    def proj(c0, n):
        return _dot(h, w_ref[:, c0:c0 + n]) + b_ref[:, c0:c0 + n]

    q_ref[...] = (proj(0, A_WIDTH) * (A_HEAD_DIM ** -0.5)).astype(BF16)
    if transposed_kv:
        k_ref[...] = proj(A_WIDTH, A_WIDTH).astype(BF16)
        kt_ref[...] = (lax.dot_general(wt_ref[0:A_WIDTH, :], h, NT, preferred_element_type=F32)
                       + bt_ref[0:A_WIDTH, :])
        vt_ref[...] = (lax.dot_general(wt_ref[A_WIDTH:2 * A_WIDTH, :], h, NT, preferred_element_type=F32)
                       + bt_ref[A_WIDTH:2 * A_WIDTH, :])
        mvt_ref[...] = (lax.dot_general(wt_ref[2 * A_WIDTH:, :], h, NT, preferred_element_type=F32)
                        + bt_ref[2 * A_WIDTH:, :]).astype(BF16)
    else:
        k_ref[...] = proj(A_WIDTH, A_WIDTH)
        v_ref[...] = proj(2 * A_WIDTH, A_WIDTH)
    o = 3 * A_WIDTH
    mq_ref[...] = proj(o, M_WIDTH).astype(BF16)
    mk_ref[...] = (proj(o + M_WIDTH, M_WIDTH) * (M_HEAD_DIM ** -0.5)).astype(BF16)
    mv_ref[...] = proj(o + 2 * M_WIDTH, M_WIDTH).astype(BF16)
    so_ref[...] = _sigmoid(proj(o + 3 * M_WIDTH, M_WIDTH))
    sm_ref[...] = proj(o + 4 * M_WIDTH, LANES)


def _inproj(x, g, w, b, tm, kv=None):
    n = x.shape[0]
    wcols = w.shape[1]
    row = lambda c: pl.BlockSpec((tm, c), lambda i: (i, 0))
    common = [
        jax.ShapeDtypeStruct((n, M_WIDTH), BF16),
        jax.ShapeDtypeStruct((n, M_WIDTH), BF16),
        jax.ShapeDtypeStruct((n, M_WIDTH), BF16),
        jax.ShapeDtypeStruct((n, M_WIDTH), F32),
        jax.ShapeDtypeStruct((n, LANES), F32),
    ]
    common_specs = [row(M_WIDTH)] * 4 + [row(LANES)]
    in_specs = [row(D_MODEL), _const_spec((1, D_MODEL)), _const_spec((D_MODEL, wcols)), _const_spec((1, wcols))]
    if kv is None:
        return pl.pallas_call(
            functools.partial(_inproj_kernel, transposed_kv=False),
            grid=(n // tm,),
            in_specs=in_specs,
            out_specs=[row(A_WIDTH)] * 3 + common_specs,
            out_shape=[jax.ShapeDtypeStruct((n, A_WIDTH), BF16), jax.ShapeDtypeStruct((n, A_WIDTH), F32),
                       jax.ShapeDtypeStruct((n, A_WIDTH), F32)] + common,
            compiler_params=_cparams("parallel"),
            name="inproj",
        )(x, g, w, b)
    nb, t, depth, layer = kv["nb"], kv["t"], kv["depth"], kv["layer"]
    tps = t // tm
    t_shape = jax.ShapeDtypeStruct((nb, depth, A_WIDTH, t), F32)
    t_spec = pl.BlockSpec((None, None, A_WIDTH, tm), lambda i: (i // tps, layer, 0, i % tps))
    prev = kv["prev"]
    if prev is None:
        prev = (jnp.zeros((SUBLANES, LANES), F32),) * 2
        aliases = {}
    else:
        aliases = {6: 7, 7: 8}
    any_spec = pl.BlockSpec(memory_space=pl.ANY)
    return pl.pallas_call(
        functools.partial(_inproj_kernel, transposed_kv=True),
        grid=(n // tm,),
        in_specs=in_specs + [_const_spec(kv["wt"].shape), _const_spec(kv["bt"].shape), any_spec, any_spec],
        out_specs=[row(A_WIDTH)] * 2 + common_specs + [t_spec, t_spec,
                   pl.BlockSpec((None, M_WIDTH, tm), lambda i: (i // tps, 0, i % tps))],
        out_shape=[jax.ShapeDtypeStruct((n, A_WIDTH), BF16), jax.ShapeDtypeStruct((n, A_WIDTH), BF16)]
                  + common + [t_shape, t_shape, jax.ShapeDtypeStruct((nb, M_WIDTH, t), BF16)],
        input_output_aliases=aliases,
        compiler_params=_cparams("parallel"),
        name="inproj_t",
    )(x, g, w, b, kv["wt"], kv["bt"], *prev)


def _gates_kernel(sm_ref, col_ref, logf_ref, *, seq_a, seg_m):
    x = sm_ref[...]
    rows = x.shape[0]
    lane = lax.broadcasted_iota(jnp.int32, (1, LANES), 1)
    row = lax.broadcasted_iota(jnp.int32, (rows, 1), 0)
    ls = _log_sigmoid(x)
    logf_ref[...] = ls[:, SM_AF:SM_AF + A_HEADS]
    is_a = lane < SM_MI
    is_i = (lane >= SM_MI) & (lane < SM_MF)
    is_f = (lane >= SM_MF) & (lane < SM_USED)
    y = jnp.where(is_i, x, ls)
    rmod_a = row & (seq_a - 1)
    rmod_m = row & (seg_m - 1)
    s = 1
    while s < max(seq_a, seg_m):
        take = jnp.zeros((rows, LANES), jnp.bool_)
        if s < seq_a:
            take = take | (is_a & (rmod_a >= s))
        if s < seg_m:
            take = take | (is_f & (rmod_m >= s))
        y = y + jnp.where(take, pltpu.roll(y, s, axis=0), 0.0)
        s *= 2
    nm = SM_MF - SM_MI
    is_g = (lane >= SM_MG) & (lane < SM_MG + nm)
    g = pltpu.roll(y, SM_MG - SM_MI, axis=1) - pltpu.roll(y, SM_MG - SM_MF, axis=1)
    s = 1
    while s < seg_m:
        g = jnp.where(is_g & (rmod_m >= s), jnp.maximum(g, pltpu.roll(g, s, axis=0)), g)
        s *= 2
    col_ref[...] = jnp.where(is_g, g, y)


def _gates(sm, rows, seq_a, seg_m):
    n = sm.shape[0]
    return pl.pallas_call(
        functools.partial(_gates_kernel, seq_a=seq_a, seg_m=seg_m),
        grid=(n // rows,),
        in_specs=[pl.BlockSpec((rows, LANES), lambda i: (i, 0))],
        out_specs=[pl.BlockSpec((rows, LANES), lambda i: (i, 0)),
                   pl.BlockSpec((rows, A_HEADS), lambda i: (i, 0))],
        out_shape=[jax.ShapeDtypeStruct((n, LANES), F32),
                   jax.ShapeDtypeStruct((n, A_HEADS), F32)],
        compiler_params=_cparams("parallel"),
        name="gates",
    )(sm)


def _fox_kernel(q_ref, k_ref, vt_ref, col_ref, o_ref, vb_ref, ka_ref, *, tq):
    hp = pl.program_id(1)
    qi = pl.program_id(2)
    lane = lax.broadcasted_iota(jnp.int32, (1, LANES), 1)
    hd = A_HEAD_DIM

    aug0 = [hd * (1 - hh) for hh in range(2)]

    @pl.when(qi == 0)
    def _():
        vb_ref[...] = vt_ref[...].astype(BF16)
        colv = col_ref[...]
        kf = k_ref[...]
        for hh in range(2):
            ck = jnp.sum(jnp.where(lane == SM_AF + hp * 2 + hh, colv, 0.0), axis=1, keepdims=True)
            c1 = (-ck).astype(BF16)
            r1 = -ck - c1.astype(F32)
            c2 = r1.astype(BF16)
            c3 = (r1 - c2.astype(F32)).astype(BF16)
            a0 = aug0[hh]
            ka_ref[hh] = jnp.where(lane == a0, c1, jnp.where(lane == a0 + 1, c2, jnp.where(lane == a0 + 2, c3, kf)))

    q2 = q_ref[...]
    zero = jnp.zeros_like(q2)
    one = jnp.ones_like(q2)
    qm = [jnp.where((lane >= aug0[hh]) & (lane < aug0[hh] + 3), one,
                    jnp.where((lane >= hd) == bool(hh), q2, zero)) for hh in range(2)]

    def scores(j):
        start = pl.multiple_of(j * tq, tq)
        return [lax.dot_general(ka_ref[hh, pl.ds(start, tq), :], qm[hh], NT, preferred_element_type=F32)
                for hh in range(2)]

    def values(j, ps):
        start = pl.multiple_of(j * tq, tq)
        return [_dot(vb_ref[hh * hd:(hh + 1) * hd, pl.ds(start, tq)], ps[hh]) for hh in range(2)]

    def softmax(ss, ms, ls, diag):
        ps, m_out, l_out, alphas = [], [], [], []
        for hh in range(2):
            s = ss[hh]
            if diag:
                kpos = lax.broadcasted_iota(jnp.int32, (tq, tq), 0)
                qpos = lax.broadcasted_iota(jnp.int32, (tq, tq), 1)
                s = jnp.where(kpos <= qpos, s, -jnp.inf)
            m_new = jnp.maximum(ms[hh], jnp.max(s, axis=0, keepdims=True))
            alpha = jnp.exp(ms[hh] - m_new)
            p = jnp.exp(s - m_new)
            l_out.append(alpha * ls[hh] + jnp.sum(p, axis=0, keepdims=True))
            m_out.append(m_new)
            alphas.append(alpha)
            ps.append(p.astype(BF16))
        return ps, m_out, l_out, alphas

    def body(j, carry):
        ss, ps_prev, a_prev, ms, ls, accs = carry
        pv = values(jnp.maximum(j - 1, 0), ps_prev)
        ss_next = scores(j + 1)
        ps, ms, ls, alphas = softmax(ss, ms, ls, False)
        accs = [a_prev[hh] * accs[hh] + pv[hh] for hh in range(2)]
        return ss_next, ps, alphas, ms, ls, accs

    two = lambda a: [a, a]
    init = (scores(0), two(jnp.zeros((tq, tq), BF16)), two(jnp.ones((1, tq), F32)),
            two(jnp.full((1, tq), -jnp.inf, F32)), two(jnp.zeros((1, tq), F32)), two(jnp.zeros((hd, tq), F32)))
    ss, ps_prev, a_prev, ms, ls, accs = lax.fori_loop(0, qi, body, init)
    pv = values(jnp.maximum(qi - 1, 0), ps_prev)
    ps, ms, ls, alphas = softmax(ss, ms, ls, True)
    pd = values(qi, ps)
    yt = [(alphas[hh] * (a_prev[hh] * accs[hh] + pv[hh]) + pd[hh]) / ls[hh] for hh in range(2)]
    o_ref[...] = jnp.concatenate(yt, axis=0).T.astype(o_ref.dtype)


def _fox_prompt(q, k, vt, col, layer, nb, t, tq):
    n = q.shape[0]
    nq = t // tq
    pairs = A_WIDTH // LANES
    return pl.pallas_call(
        functools.partial(_fox_kernel, tq=tq),
        grid=(nb, pairs, nq),
        in_specs=[
            pl.BlockSpec((tq, LANES), lambda b, hp, qi: (b * nq + qi, hp)),
            pl.BlockSpec((t, LANES), lambda b, hp, qi: (b, hp)),
            pl.BlockSpec((None, None, LANES, t), lambda b, hp, qi: (b, layer, hp, 0)),
            pl.BlockSpec((t, LANES), lambda b, hp, qi: (b, 0)),
        ],
        out_specs=pl.BlockSpec((tq, LANES), lambda b, hp, qi: (b * nq + qi, hp)),
        out_shape=jax.ShapeDtypeStruct((n, A_WIDTH), BF16),
        scratch_shapes=[pltpu.VMEM((LANES, t), BF16), pltpu.VMEM((2, t, LANES), BF16)],
        compiler_params=_cparams("parallel", "parallel", "arbitrary"),
        name="fox_prompt",
    )(q, k, vt, col)


def _pool_scan_kernel(x_ref, o_ref):
    y = x_ref[...]
    lane = lax.broadcasted_iota(jnp.int32, (1, LANES), 1)
    s = 1
    while s < LANES:
        y = y + jnp.where(lane >= s, pltpu.roll(y, s, axis=1), 0.0)
        s *= 2
    o_ref[...] = y


def _pool_scan(x, rows):
    n = x.shape[0]
    return pl.pallas_call(
        _pool_scan_kernel,
        grid=(n // rows,),
        in_specs=[pl.BlockSpec((rows, LANES), lambda i: (i, 0))],
        out_specs=pl.BlockSpec((rows, LANES), lambda i: (i, 0)),
        out_shape=jax.ShapeDtypeStruct((n, LANES), F32),
        compiler_params=_cparams("parallel"),
        name="pool_scan",
    )(x)


def _fox_sample_kernel(pt_ref, q_ref, kn_ref, vn_ref, col_ref, rown_ref, *rest,
                       n_pages, layer, s_len):
    del pt_ref
    k_refs = rest[:n_pages]
    v_refs = rest[n_pages:2 * n_pages]
    lc_refs = rest[2 * n_pages:3 * n_pages]
    o_ref = rest[3 * n_pages]
    s_ref = rest[3 * n_pages + 1]
    nh = A_HEADS
    rows = nh * s_len
    lane_w = lax.broadcasted_iota(jnp.int32, (1, A_WIDTH), 1)
    rid = lax.broadcasted_iota(jnp.int32, (rows, 1), 0)
    own = jnp.right_shift(lane_w, 6) == jnp.right_shift(rid, 3)
    q = q_ref[...].astype(F32)
    qbd = jnp.where(own, jnp.concatenate([q] * nh, axis=0), 0.0)
    colv = col_ref[...]
    rown = rown_ref[...]
    lo = layer * nh

    tot = [lc_refs[p][lo:lo + nh, LANES - 1:LANES] for p in range(n_pages)]
    base = [None] * n_pages
    suf = jnp.zeros((nh, 1), F32)
    for p in range(n_pages - 1, -1, -1):
        suf = suf + tot[p]
        base[p] = suf

    m_run = [None] * nh
    for p in range(n_pages):
        sp = _dot(qbd, k_refs[p][...])
        for h in range(nh):
            sl = slice(h * s_len, (h + 1) * s_len)
            cn = colv[:, SM_AF + h:SM_AF + h + 1]
            bias = (base[p][h:h + 1, :] - lc_refs[p][lo + h:lo + h + 1, :]) + cn
            sh = sp[sl, :] + bias
            s_ref[sl, p * LANES:(p + 1) * LANES] = sh
            mx = jnp.max(sh, axis=1, keepdims=True)
            m_run[h] = mx if m_run[h] is None else jnp.maximum(m_run[h], mx)
    sn = lax.dot_general(qbd, kn_ref[...], NT, preferred_element_type=F32)
    r = lax.broadcasted_iota(jnp.int32, (s_len, s_len), 0)
    c = lax.broadcasted_iota(jnp.int32, (s_len, s_len), 1)
    pn, m_all = [], []
    for h in range(nh):
        sl = slice(h * s_len, (h + 1) * s_len)
        cn = colv[:, SM_AF + h:SM_AF + h + 1]
        sh = sn[sl, :] + (cn - rown[SM_AF + h:SM_AF + h + 1, :])
        sh = jnp.where(c <= r, sh, -jnp.inf)
        pn.append(sh)
        m_all.append(jnp.maximum(m_run[h], jnp.max(sh, axis=1, keepdims=True)))
    m = jnp.concatenate(m_all, axis=0)
    p_new = jnp.exp(jnp.concatenate(pn, axis=0) - m)
    l = jnp.sum(p_new, axis=1, keepdims=True)
    acc = _dot(p_new, vn_ref[...])
    for p in range(n_pages):
        pp = jnp.exp(s_ref[:, p * LANES:(p + 1) * LANES] - m)
        l = l + jnp.sum(pp, axis=1, keepdims=True)
        acc = acc + lax.dot_general(pp, v_refs[p][...], NT, preferred_element_type=F32)
    acc = jnp.where(own, acc / l, 0.0)
    y = acc[0:s_len, :]
    for h in range(1, nh):
        y = y + acc[h * s_len:(h + 1) * s_len, :]
    o_ref[...] = y.astype(o_ref.dtype)


def _fox_sample(page_table, q, k_new, v_new, col, rown, cache_k, cache_v, lc, layer, nb, s_len):
    n_pages = page_table.shape[1]
    n_phys, page = cache_k.shape[0], cache_k.shape[1]
    depth = cache_k.shape[2]
    ck = cache_k.transpose(0, 2, 3, 4, 1).reshape(n_phys, depth, A_WIDTH, page)
    cv = cache_v.transpose(0, 2, 3, 4, 1).reshape(n_phys, depth, A_WIDTH, page)
    tok = lambda c: pl.BlockSpec((s_len, c), lambda b, pt: (b, 0))
    page_specs = [pl.BlockSpec((None, None, A_WIDTH, page),
                               lambda b, pt, j=j: (pt[b, j], layer, 0, 0)) for j in range(n_pages)]
    lc_specs = [pl.BlockSpec((None, 2 * A_HEADS, LANES),
                             lambda b, pt, j=j: (pt[b, j], 0, 0)) for j in range(n_pages)]
    grid_spec = pltpu.PrefetchScalarGridSpec(
        num_scalar_prefetch=1,
        grid=(nb,),
        in_specs=[tok(A_WIDTH), tok(A_WIDTH), tok(A_WIDTH), tok(LANES),
                  pl.BlockSpec((None, SM_ROWS, s_len), lambda b, pt: (b, 0, 0))]
                 + page_specs + page_specs + lc_specs,
        out_specs=tok(A_WIDTH),
        scratch_shapes=[pltpu.VMEM((A_HEADS * s_len, n_pages * LANES), F32)],
    )
    return pl.pallas_call(
        functools.partial(_fox_sample_kernel, n_pages=n_pages, layer=layer, s_len=s_len),
        grid_spec=grid_spec,
        out_shape=jax.ShapeDtypeStruct((nb * s_len, A_WIDTH), BF16),
        compiler_params=_cparams("arbitrary"),
        name="fox_sample",
    )(page_table, q, k_new, v_new, col, rown, *([ck] * n_pages), *([cv] * n_pages), *([lc] * n_pages))


def _mlstm_kernel(q_ref, k_ref, v_ref, so_ref, col_ref, row_ref, g_ref, c0_ref, n0_ref, m0_ref, _,
                  y_ref, c1_ref, n1_ref, m1_ref, *, t, chunk, bb, lowp):
    nh, hd = M_HEADS, M_HEAD_DIM
    r = lax.broadcasted_iota(jnp.int32, (chunk, chunk), 0)
    c = lax.broadcasted_iota(jnp.int32, (chunk, chunk), 1)
    causal = c <= r
    g = g_ref[...]
    op = (lambda a: a.astype(BF16)) if lowp else (lambda a: a.astype(F32))
    heads = range(nh)
    hs = lambda h: slice(h * hd, (h + 1) * hd)

    def chunk_body(bi, ci, carry):
        cs, ns, ms = carry
        r0 = pl.multiple_of(bi * t + ci * chunk, chunk)
        l0 = pl.multiple_of(ci * chunk, chunk)
        qa = q_ref[pl.ds(r0, chunk), :]
        ka = k_ref[pl.ds(r0, chunk), :]
        va = v_ref[pl.ds(r0, chunk), :]
        colv = col_ref[pl.ds(r0, chunk), :]
        rows = row_ref[bi, :, pl.ds(l0, chunk)]
        qs = [op(qa[:, hs(h)]) for h in heads]
        ks = [op(ka[:, hs(h)]) for h in heads]
        vs = [op(va[:, hs(h)]) for h in heads]
        qk = [lax.dot_general(qs[h], ks[h], NT, preferred_element_type=F32) for h in heads]
        qc = [lax.dot_general(qs[h], op(cs[h]), NT, preferred_element_type=F32) for h in heads]
        wqk, a_, m_t_, fcol_, icol_ = [], [], [], [], []
        for h in heads:
            f_col = colv[:, SM_MF + h:SM_MF + h + 1]
            i_col = colv[:, SM_MI + h:SM_MI + h + 1]
            f_row = rows[SM_MF + h:SM_MF + h + 1, :]
            i_row = rows[SM_MI + h:SM_MI + h + 1, :]
            d = jnp.where(causal, f_col + (i_row - f_row), -jnp.inf)
            inter = ms[h] + f_col
            m_t = jnp.maximum(inter, jnp.max(d, axis=1, keepdims=True))
            wqk.append(jnp.exp(d - m_t) * qk[h])
            a_.append(jnp.exp(inter - m_t))
            m_t_.append(m_t)
            fcol_.append(f_col)
            icol_.append(i_col)
        pv = [_dot(op(wqk[h]), vs[h]) for h in heads]
        ys, cs_new, ns_new, ms_new = [], [], [], []
        for h in heads:
            a, m_t = a_[h], m_t_[h]
            qf = qa[:, hs(h)].astype(F32)
            num = a * qc[h] + pv[h]
            den = a * jnp.sum(qf * ns[h], axis=1, keepdims=True) + jnp.sum(wqk[h], axis=1, keepdims=True)
            h_t = num / jnp.maximum(jnp.abs(den), jnp.exp(-m_t))
            ys.append(h_t * lax.rsqrt(jnp.mean(h_t * h_t, axis=1, keepdims=True) + EPS))
            m_last = m_t[chunk - 1:chunk, :]
            a_last = a[chunk - 1:chunk, :]
            w_last = jnp.exp((icol_[h] - fcol_[h]) + (fcol_[h][chunk - 1:chunk, :] - m_last))
            vw = va[:, hs(h)].astype(F32) * w_last
            cs_new.append(a_last * cs[h] + _dot(op(vw.T), ks[h]))
            ns_new.append(a_last * ns[h] + jnp.sum(ka[:, hs(h)].astype(F32) * w_last, axis=0, keepdims=True))
            ms_new.append(m_last)
        hn = jnp.concatenate(ys, axis=1) * g
        y_ref[pl.ds(r0, chunk), :] = (so_ref[pl.ds(r0, chunk), :] * hn).astype(y_ref.dtype)
        return cs_new, ns_new, ms_new

    def seq_body(bi, _):
        init = ([c0_ref[bi, h] for h in heads], [n0_ref[bi, h] for h in heads], [m0_ref[bi, h] for h in heads])
        if t == chunk:
            cs, ns, ms = chunk_body(bi, 0, init)
        else:
            cs, ns, ms = lax.fori_loop(0, t // chunk, lambda ci, cr: chunk_body(bi, ci, cr), init)
        for h in heads:
            c1_ref[bi, h] = cs[h]
            n1_ref[bi, h] = ns[h]
            m1_ref[bi, h] = ms[h]
        return 0

    if bb == 1:
        seq_body(0, 0)
    else:
        lax.fori_loop(0, bb, seq_body, 0)


def _mlstm(q, k, v, so, col, rowt, g, c0, n0, m0, layer_in, c_prev, layer_out, depth, nb, t, chunk, bb):
    n = q.shape[0]
    nh, hd = M_HEADS, M_HEAD_DIM
    seq = lambda w: pl.BlockSpec((bb * t, w), lambda i: (i, 0))
    st_in = lambda a, b_: pl.BlockSpec((bb, None, nh, a, b_), lambda i: (i, layer_in, 0, 0, 0))
    st_out = lambda a, b_: pl.BlockSpec((bb, nh, a, b_), lambda i: (i, 0, 0, 0))
    if c_prev is None:
        c_prev = jnp.zeros((SUBLANES, LANES), F32)
        aliases = {}
    else:
        aliases = {10: 1}
    return pl.pallas_call(
        functools.partial(_mlstm_kernel, t=t, chunk=chunk, bb=bb, lowp=chunk >= 16),
        grid=(nb // bb,),
        in_specs=[seq(M_WIDTH), seq(M_WIDTH), seq(M_WIDTH), seq(M_WIDTH), seq(LANES),
                  pl.BlockSpec((bb, SM_ROWS, t), lambda i: (i, 0, 0)),
                  _const_spec((1, M_WIDTH)),
                  st_in(hd, hd), st_in(1, hd), st_in(1, 1),
                  pl.BlockSpec(memory_space=pl.ANY)],
        out_specs=[seq(M_WIDTH),
                   pl.BlockSpec((bb, None, nh, hd, hd), lambda i: (i, layer_out, 0, 0, 0)),
                   st_out(1, hd), st_out(1, 1)],
        out_shape=[jax.ShapeDtypeStruct((n, M_WIDTH), BF16),
                   jax.ShapeDtypeStruct((nb, depth, nh, hd, hd), F32),
                   jax.ShapeDtypeStruct((nb, nh, 1, hd), F32),
                   jax.ShapeDtypeStruct((nb, nh, 1, 1), F32)],
        input_output_aliases=aliases,
        compiler_params=_cparams("parallel"),
        name="mlstm",
    )(q, k, v, so, col, rowt, g, c0, n0.reshape(n0.shape[:3] + (1, hd)), m0.reshape(m0.shape[:3] + (1, 1)), c_prev)


C_EXT = M_HEAD_DIM + 16


def _mlstm_t_kernel(q_ref, k_ref, vt_ref, so_ref, col_ref, row_ref, g_ref, _,
                    y_ref, c1_ref, n1_ref, m1_ref, *, t, chunk):
    nh, hd = M_HEADS, M_HEAD_DIM
    s_idx = lax.broadcasted_iota(jnp.int32, (chunk, chunk), 0)
    l_idx = lax.broadcasted_iota(jnp.int32, (chunk, chunk), 1)
    causal = s_idx <= l_idx
    heads = range(nh)
    hs = lambda h: slice(h * hd, (h + 1) * hd)
    last = slice(chunk - 1, chunk)
    pad_rows = jnp.zeros((C_EXT - hd - 1, chunk), F32)

    def chunk_body(ci, carry):
        cs, ms = carry
        r0 = pl.multiple_of(ci * chunk, chunk)
        qa = q_ref[pl.ds(r0, chunk), :]
        ka = k_ref[pl.ds(r0, chunk), :]
        vta = vt_ref[:, pl.ds(r0, chunk)]
        colv = col_ref[pl.ds(r0, chunk), :]
        rows = row_ref[:, pl.ds(r0, chunk)]
        qk = [lax.dot_general(ka[:, hs(h)], qa[:, hs(h)], NT, preferred_element_type=F32) for h in heads]
        cq = [lax.dot_general(cs[h].astype(BF16), qa[:, hs(h)], NT, preferred_element_type=F32) for h in heads]
        wqk, a_, m_t_, wl_ = [], [], [], []
        for h in heads:
            f_row = rows[SM_MF + h:SM_MF + h + 1, :]
            i_row = rows[SM_MI + h:SM_MI + h + 1, :]
            g_max = rows[SM_MG + h:SM_MG + h + 1, :]
            g_col = colv[:, SM_MI + h:SM_MI + h + 1] - colv[:, SM_MF + h:SM_MF + h + 1]
            mx = jnp.maximum(ms[h], g_max)
            m_t = f_row + mx
            d = jnp.where(causal, f_row + g_col, -jnp.inf)
            wqk.append(jnp.exp(d - m_t) * qk[h])
            a_.append(jnp.exp(ms[h] - mx))
            m_t_.append(m_t)
            wl_.append(jnp.exp((i_row - f_row) + (f_row[:, last] - m_t[:, last])))
        pv = [_dot(vta[hs(h), :], wqk[h].astype(BF16)) for h in heads]
        ys, cs_new, ms_new = [], [], []
        for h in heads:
            a, m_t = a_[h], m_t_[h]
            num = a * cq[h][0:hd, :] + pv[h]
            den = a * cq[h][hd:hd + 1, :] + jnp.sum(wqk[h], axis=0, keepdims=True)
            h_t = num / jnp.maximum(jnp.abs(den), jnp.exp(-m_t))
            ys.append((h_t * lax.rsqrt(jnp.mean(h_t * h_t, axis=0, keepdims=True) + EPS)).T)
            vw = jnp.concatenate([vta[hs(h), :].astype(F32) * wl_[h], wl_[h], pad_rows], axis=0)
            cs_new.append(a[:, last] * cs[h] + _dot(vw.astype(BF16), ka[:, hs(h)]))
            ms_new.append(m_t[:, last])
        hn = jnp.concatenate(ys, axis=1) * g_ref[...]
        y_ref[pl.ds(r0, chunk), :] = (so_ref[pl.ds(r0, chunk), :] * hn).astype(y_ref.dtype)
        return cs_new, ms_new

    init = ([jnp.zeros((C_EXT, hd), F32)] * nh, [jnp.zeros((1, 1), F32)] * nh)
    cs, ms = lax.fori_loop(0, t // chunk, chunk_body, init)
    for h in heads:
        c1_ref[h] = cs[h][0:hd, :]
        n1_ref[h] = cs[h][hd:hd + 1, :]
        m1_ref[h] = ms[h]


def _mlstm_t(q, k, vt, so, col, rowt, g, c_prev, layer_out, depth, nb, t, chunk):
    n = q.shape[0]
    nh, hd = M_HEADS, M_HEAD_DIM
    seq = lambda w: pl.BlockSpec((t, w), lambda i: (i, 0))
    st_out = lambda a, b_: pl.BlockSpec((None, nh, a, b_), lambda i: (i, 0, 0, 0))
    if c_prev is None:
        c_prev = jnp.zeros((SUBLANES, LANES), F32)
        aliases = {}
    else:
        aliases = {7: 1}
    return pl.pallas_call(
        functools.partial(_mlstm_t_kernel, t=t, chunk=chunk),
        grid=(nb,),
        in_specs=[seq(M_WIDTH), seq(M_WIDTH), pl.BlockSpec((None, M_WIDTH, t), lambda i: (i, 0, 0)),
                  seq(M_WIDTH), seq(LANES), pl.BlockSpec((None, SM_ROWS, t), lambda i: (i, 0, 0)),
                  _const_spec((1, M_WIDTH)), pl.BlockSpec(memory_space=pl.ANY)],
        out_specs=[seq(M_WIDTH),
                   pl.BlockSpec((None, None, nh, hd, hd), lambda i: (i, layer_out, 0, 0, 0)),
                   st_out(1, hd), st_out(1, 1)],
        out_shape=[jax.ShapeDtypeStruct((n, M_WIDTH), BF16),
                   jax.ShapeDtypeStruct((nb, depth, nh, hd, hd), F32),
                   jax.ShapeDtypeStruct((nb, nh, 1, hd), F32),
                   jax.ShapeDtypeStruct((nb, nh, 1, 1), F32)],
        input_output_aliases=aliases,
        compiler_params=_cparams("parallel"),
        name="mlstm_t",
    )(q, k, vt, so, col, rowt, g, c_prev)


def _conv_carry_init(state_ref, carry_ref, tiles_per_seq):
    @pl.when((pl.program_id(0) % tiles_per_seq) == 0)
    def _():
        carry_ref[...] = state_ref[...]


def _conv_long(u, w_ref, c0, carry_ref):
    tm, ch = u.shape
    prev = carry_ref[:, c0:c0 + ch]
    row = lax.broadcasted_iota(jnp.int32, (tm, 1), 0)
    p1 = jnp.where(row == 0, prev[1:2, :], pltpu.roll(u, 1, axis=0))
    p2 = jnp.where(row == 0, prev[0:1, :], jnp.where(row == 1, prev[1:2, :], pltpu.roll(u, 2, axis=0)))
    carry_ref[:, c0:c0 + ch] = u[tm - 2:tm, :]
    return p2 * w_ref[0:1, c0:c0 + ch] + p1 * w_ref[1:2, c0:c0 + ch] + u * w_ref[2:3, c0:c0 + ch]


def _conv_short(u, w_ref, c0, fill_ref, s_len):
    tm, ch = u.shape
    fill = fill_ref[:, c0:c0 + ch]
    rmod = lax.broadcasted_iota(jnp.int32, (tm, 1), 0) & (s_len - 1)
    p1 = jnp.where(rmod >= 1, pltpu.roll(u, 1, axis=0), pltpu.roll(fill, tm - 1, axis=0))
    p2 = jnp.where(rmod >= 2, pltpu.roll(u, 2, axis=0), fill)
    return p2 * w_ref[0:1, c0:c0 + ch] + p1 * w_ref[1:2, c0:c0 + ch] + u * w_ref[2:3, c0:c0 + ch]


def _merge_kernel(x_ref, g_ref, wc_ref, bc_ref, wg_ref, bg_ref, ws_ref, ya_ref, ym_ref,
                  woa_ref, wom_ref, woc_ref, wo_ref, st_ref, x1_ref, cst_ref, *scratch,
                  long_seq, tiles_per_seq, s_len):
    x = x_ref[...]
    h = _rms(x, g_ref[...]).astype(BF16)
    cw = C_WIDTH
    cb = _dot(h, wc_ref[:, 0:cw]) + bc_ref[:, 0:cw]
    u = (_dot(h, wc_ref[:, cw:2 * cw]) + bc_ref[:, cw:2 * cw]) * (_dot(h, wc_ref[:, 2 * cw:3 * cw]) + bc_ref[:, 2 * cw:3 * cw])
    if long_seq:
        _conv_carry_init(st_ref, scratch[0], tiles_per_seq)
        uc = _conv_long(u, ws_ref, 0, scratch[0])
        cst_ref[...] = u[u.shape[0] - 2:, :]
    else:
        uc = _conv_short(u, ws_ref, 0, st_ref, s_len)
        cst_ref[...] = u
    yc = (cb * uc).astype(BF16)
    d = D_MODEL

    def gate(i):
        return _sigmoid(_dot(h, wg_ref[:, i * d:(i + 1) * d]) + bg_ref[:, i * d:(i + 1) * d])

    merged = gate(0) * _dot(ya_ref[...], woa_ref[...])
    merged = merged + gate(1) * _dot(ym_ref[...], wom_ref[...])
    merged = merged + gate(2) * _dot(yc, woc_ref[...])
    x1_ref[...] = x + _dot(merged.astype(BF16), wo_ref[...])


def _merge(x, g, wc, bc, wg, bg, ws, ya, ym, woa, wom, woc, wo, st, nb, t, tm):
    n = x.shape[0]
    long_seq = t >= tm
    tiles_per_seq = max(t // tm, 1)
    row = lambda c: pl.BlockSpec((tm, c), lambda i: (i, 0))
    if long_seq:
        st_spec = pl.BlockSpec((None, 2, C_WIDTH), lambda i: (i // tiles_per_seq, 0, 0))
        cst_spec = pl.BlockSpec((None, 2, C_WIDTH), lambda i: (i // tiles_per_seq, 0, 0))
        cst_shape = jax.ShapeDtypeStruct((nb, 2, C_WIDTH), F32)
        scratch = [pltpu.VMEM((2, C_WIDTH), F32)]
    else:
        st_spec = row(C_WIDTH)
        cst_spec = row(C_WIDTH)
        cst_shape = jax.ShapeDtypeStruct((n, C_WIDTH), F32)
        scratch = []
    return pl.pallas_call(
        functools.partial(_merge_kernel, long_seq=long_seq, tiles_per_seq=tiles_per_seq, s_len=t),
        grid=(n // tm,),
        in_specs=[row(D_MODEL), _const_spec((1, D_MODEL)),
                  _const_spec(wc.shape), _const_spec(bc.shape), _const_spec(wg.shape), _const_spec(bg.shape),
                  _const_spec(ws.shape), row(A_WIDTH), row(M_WIDTH),
                  _const_spec(woa.shape), _const_spec(wom.shape), _const_spec(woc.shape), _const_spec(wo.shape),
                  st_spec],
        out_specs=[row(D_MODEL), cst_spec],
        out_shape=[jax.ShapeDtypeStruct((n, D_MODEL), F32), cst_shape],
        scratch_shapes=scratch,
        compiler_params=_cparams("arbitrary"),
        name="merge",
    )(x, g, wc, bc, wg, bg, ws, ya, ym, woa, wom, woc, wo, st)


FF_CHUNK = D_FF // 2


def _ffn_kernel(x_ref, pe_ref, gf_ref, wup_ref, wfc_ref, bfc_ref, wdn_ref, gp_ref, wpg_ref, wpp_ref, gfin_ref,
                st_ref, xo_ref, fst_ref, *scratch, long_seq, tiles_per_seq, s_len, final):
    x = x_ref[...]
    tm = x.shape[0]
    h2 = _rms(x, gf_ref[...]).astype(BF16)
    acc = jnp.zeros((tm, D_MODEL), F32)
    if long_seq:
        _conv_carry_init(st_ref, scratch[0], tiles_per_seq)
    for half in range(D_FF // FF_CHUNK):
        c0 = half * FF_CHUNK
        ua = _dot(h2, wup_ref[:, c0:c0 + FF_CHUNK])
        ub = _dot(h2, wup_ref[:, D_FF + c0:D_FF + c0 + FF_CHUNK])
        if long_seq:
            uac = _conv_long(ua, wfc_ref, c0, scratch[0])
            fst_ref[:, c0:c0 + FF_CHUNK] = ua[tm - 2:, :]
        else:
            uac = _conv_short(ua, wfc_ref, c0, st_ref, s_len)
            fst_ref[:, c0:c0 + FF_CHUNK] = ua
        z = uac + bfc_ref[:, c0:c0 + FF_CHUNK]
        act = (z * _sigmoid(z) * ub).astype(BF16)
        acc = acc + _dot(act, wdn_ref[c0:c0 + FF_CHUNK, :])
    x2 = x + acc
    gate = _sigmoid(_dot(_rms(x2, gp_ref[...]).astype(BF16), wpg_ref[...]))
    x3 = x2 + gate * _dot(pe_ref[...].astype(BF16), wpp_ref[...])
    if final:
        x3 = _rms(x3, gfin_ref[...])
    xo_ref[...] = x3


def _ffn(x, pe, gf, wup, wfc, bfc, wdn, gp, wpg, wpp, gfin, st, nb, t, tm, final):
    n = x.shape[0]
    long_seq = t >= tm
    tiles_per_seq = max(t // tm, 1)
    row = lambda c: pl.BlockSpec((tm, c), lambda i: (i, 0))
    if long_seq:
        st_spec = pl.BlockSpec((None, 2, D_FF), lambda i: (i // tiles_per_seq, 0, 0))
        fst_spec = pl.BlockSpec((None, 2, D_FF), lambda i: (i // tiles_per_seq, 0, 0))
        fst_shape = jax.ShapeDtypeStruct((nb, 2, D_FF), F32)
        scratch = [pltpu.VMEM((2, D_FF), F32)]
    else:
        st_spec = row(D_FF)
        fst_spec = row(D_FF)
        fst_shape = jax.ShapeDtypeStruct((n, D_FF), F32)
        scratch = []
    return pl.pallas_call(
        functools.partial(_ffn_kernel, long_seq=long_seq, tiles_per_seq=tiles_per_seq, s_len=t, final=final),
        grid=(n // tm,),
        in_specs=[row(D_MODEL), row(pe.shape[1]), _const_spec((1, D_MODEL)),
                  _const_spec(wup.shape), _const_spec(wfc.shape), _const_spec(bfc.shape), _const_spec(wdn.shape),
                  _const_spec((1, D_MODEL)), _const_spec(wpg.shape), _const_spec(wpp.shape),
                  _const_spec((1, D_MODEL)), st_spec],
        out_specs=[row(D_MODEL), fst_spec],
        out_shape=[jax.ShapeDtypeStruct((n, D_MODEL), F32), fst_shape],
        scratch_shapes=scratch,
        compiler_params=_cparams("arbitrary"),
        name="ffn",
    )(x, pe, gf, wup, wfc, bfc, wdn, gp, wpg, wpp, gfin, st)


def _row_form(col, nb, t):
    return col[:, :SM_ROWS].reshape(nb, t, SM_ROWS).transpose(0, 2, 1)


def _short_fill(state, s_len):
    nb, _, ch = state.shape
    return jnp.pad(state, ((0, 0), (0, s_len - 2), (0, 0))).reshape(nb * s_len, ch)


def _pick(n, pref):
    return pref if n % pref == 0 else n


def _layer(x, pe, w, nb, t, chunk, past, final, carry):
    n = x.shape[0]
    layer, depth = carry["layer"], carry["depth"]
    tm_in = _pick(n, 512)
    tm_merge = _pick(n, 512)
    tm_ffn = _pick(n, 256)
    if past is None:
        kvp = dict(layer=layer, depth=depth, prev=carry["kvt"], nb=nb, t=t, wt=w["w_kvt"], bt=w["b_kvt"])
        q, k, mq, mk, _, so, sm, kt, vt, mvt = _inproj(x, w["g_mix"], w["w_am"], w["b_am"], tm_in, kvp)
        col, logf = _gates(sm, t, t, chunk)
        rowt = _row_form(col, nb, t)
        ya = _fox_prompt(q, k, vt, col, layer, nb, t, _pick(t, 256))
        k, v = kt, vt
        ym, c1, n1, m1 = _mlstm_t(mq, mk, mvt, so, col, rowt, w["g_mhead"], carry["c"], layer, depth, nb, t, chunk)
        conv0 = jnp.zeros((nb, 2, C_WIDTH), F32)
        ffn0 = jnp.zeros((nb, 2, D_FF), F32)
    else:
        q, k, v, mq, mk, mv, so, sm = _inproj(x, w["g_mix"], w["w_am"], w["b_am"], tm_in)
        col, logf = _gates(sm, n, t, t)
        rowt = _row_form(col, nb, t)
        ya = _fox_sample(past["page_table"], q, k, v, col, rowt, past["cache_k"], past["cache_v"],
                         past["lc"], layer, nb, t)
        conv0, ffn0 = past["conv0"], past["ffn0"]
        ym, c1, n1, m1 = _mlstm(mq, mk, mv, so, col, rowt, w["g_mhead"], past["c0"], past["n0"], past["m0"], layer,
                                carry["c"], layer, depth, nb, t, chunk, _pick(nb, SUBLANES))
    long_merge = t >= tm_merge
    long_ffn = t >= tm_ffn
    x1, conv1 = _merge(x, w["g_mix"], w["w_c"], w["b_c"], w["w_g"], w["b_g"], w["w_sconv"], ya, ym,
                       w["w_oa"], w["w_om"], w["w_oc"], w["w_o"],
                       conv0 if long_merge else _short_fill(conv0, t), nb, t, tm_merge)
    x3, ffn1 = _ffn(x1, pe, w["g_ffn"], w["w_up"], w["w_fconv"], w["b_fconv"], w["w_down"],
                    w["g_ple"], w["w_ple_gate"], w["w_ple"], w["g_final"],
                    ffn0 if long_ffn else _short_fill(ffn0, t), nb, t, tm_ffn, final)
    if not long_merge:
        conv1 = conv1.reshape(nb, t, C_WIDTH)[:, t - 2:, :]
    if not long_ffn:
        ffn1 = ffn1.reshape(nb, t, D_FF)[:, t - 2:, :]
    if past is not None:
        k = k.reshape(nb, t, A_HEADS, A_HEAD_DIM)
        v = v.reshape(nb, t, A_HEADS, A_HEAD_DIM)
    state = (k, v, logf.reshape(nb, t, A_HEADS), c1, n1.reshape(nb, M_HEADS, M_HEAD_DIM),
             m1.reshape(nb, M_HEADS), conv1, ffn1)
    return x3, state


def _layer_weights(l, w_in, b_in, g_mix, g_mhead, w_sconv, w_oa, w_om, w_oc, w_o, g_ffn, w_up,
                   w_fconv, b_fconv, w_down, g_ple, w_ple_gate, w_ple, g_final):
    a0 = 0
    af0 = 3 * A_WIDTH
    m0 = af0 + A_HEADS
    mi0 = m0 + 3 * M_WIDTH
    mo0 = mi0 + 2 * M_HEADS
    c0 = mo0 + M_WIDTH
    g0 = c0 + 3 * C_WIDTH
    wl, bl = w_in[l], b_in[l]
    pad = LANES - SM_USED

    def cols(a, lo, hi):
        return a[..., lo:hi]

    def am(a):
        return jnp.concatenate(
            [cols(a, a0, af0), cols(a, m0, mi0), cols(a, mo0, c0),
             cols(a, af0, m0), cols(a, mi0, mo0), jnp.zeros(a.shape[:-1] + (pad,), a.dtype)], axis=-1)

    row = lambda a: a.reshape(1, -1)
    return {
        "w_am": am(wl).astype(BF16), "b_am": row(am(bl)),
        "w_kvt": jnp.concatenate([cols(wl, A_WIDTH, af0), cols(wl, m0 + 2 * M_WIDTH, mi0)], axis=-1).T.astype(BF16),
        "b_kvt": jnp.concatenate([cols(bl, A_WIDTH, af0), cols(bl, m0 + 2 * M_WIDTH, mi0)], axis=-1).reshape(-1, 1),
        "w_c": cols(wl, c0, g0).astype(BF16), "b_c": row(cols(bl, c0, g0)),
        "w_g": cols(wl, g0, g0 + 3 * D_MODEL).astype(BF16), "b_g": row(cols(bl, g0, g0 + 3 * D_MODEL)),
        "g_mix": row(g_mix[l]), "g_mhead": row(g_mhead[l]), "w_sconv": w_sconv[l],
        "w_oa": w_oa[l].astype(BF16), "w_om": w_om[l].astype(BF16), "w_oc": w_oc[l].astype(BF16),
        "w_o": w_o[l].astype(BF16), "g_ffn": row(g_ffn[l]), "w_up": w_up[l].astype(BF16),
        "w_fconv": w_fconv[l], "b_fconv": row(b_fconv[l]), "w_down": w_down[l].astype(BF16),
        "g_ple": row(g_ple[l]), "w_ple_gate": w_ple_gate[l].astype(BF16), "w_ple": w_ple[l].astype(BF16),
        "g_final": row(g_final),
    }


def kernel(x_prompt, x_sample, cache_k, cache_v, cache_logf, state_mlstm_C, state_mlstm_n, state_mlstm_m, state_conv, state_ffn_conv, page_table, p_prompt, p_sample, w_in, b_in, g_mix, g_mhead, w_sconv, w_oa, w_om, w_oc, w_o, g_ffn, w_up, w_fconv, b_fconv, w_down, g_ple, w_ple_gate, w_ple, g_final):
    nbp, tp, _ = x_prompt.shape
    nbs, ts, _ = x_sample.shape
    depth = w_in.shape[0]
    n_phys, page = cache_logf.shape[0], cache_logf.shape[1]
    assert page == LANES and ts == SUBLANES and depth * A_HEADS == SM_USED

    lf_t = cache_logf.reshape(n_phys, page, depth * A_HEADS).transpose(0, 2, 1).reshape(-1, LANES)
    lc = _pool_scan(lf_t, _pick(lf_t.shape[0], 2048)).reshape(n_phys, depth * A_HEADS, LANES)

    xp = x_prompt.reshape(nbp * tp, D_MODEL)
    xs = x_sample.reshape(nbs * ts, D_MODEL)
    chunk_p = _pick(tp, 256)
    new_p, new_s = [], []
    kvt = c_p = c_s = None
    for l in range(depth):
        w = _layer_weights(l, w_in, b_in, g_mix, g_mhead, w_sconv, w_oa, w_om, w_oc, w_o, g_ffn, w_up,
                           w_fconv, b_fconv, w_down, g_ple, w_ple_gate, w_ple, g_final)
        final = l == depth - 1
        xp, st_p = _layer(xp, p_prompt[l].reshape(nbp * tp, -1), w, nbp, tp, chunk_p, None, final,
                          {"layer": l, "depth": depth, "kvt": kvt, "c": c_p})
        kvt, c_p = st_p[:2], st_p[3]
        past = {"page_table": page_table, "cache_k": cache_k, "cache_v": cache_v, "lc": lc,
                "c0": state_mlstm_C, "n0": state_mlstm_n, "m0": state_mlstm_m,
                "conv0": state_conv[:, l], "ffn0": state_ffn_conv[:, l]}
        xs, st_s = _layer(xs, p_sample[l].reshape(nbs * ts, -1), w, nbs, ts, ts, past, final,
                          {"layer": l, "depth": depth, "kvt": None, "c": c_s})
        c_s = st_s[3]
        new_p.append(st_p)
        new_s.append(st_s)

    def stack(per_layer, c_all, first):
        rows = [jnp.stack([s[i] for s in per_layer], axis=2) for i in range(first, 3)]
        states = [jnp.stack([s[i] for s in per_layer], axis=1) for i in range(4, 8)]
        return tuple(rows + [c_all] + states)

    kv_p = tuple(a.reshape(nbp, depth, A_HEADS, A_HEAD_DIM, tp).transpose(0, 4, 1, 2, 3) for a in kvt)
    return ((xp.reshape(nbp, tp, D_MODEL), xs.reshape(nbs, ts, D_MODEL)) + kv_p + stack(new_p, c_p, 2)
            + stack(new_s, c_s, 0))
```

```python
import functools

import jax
import jax.numpy as jnp
from jax import lax
from jax.experimental import pallas as pl
from jax.experimental.pallas import tpu as pltpu

F32 = jnp.float32
BF16 = jnp.bfloat16

D_MODEL = 1024
A_HEADS = 8
A_HEAD_DIM = 64
A_WIDTH = A_HEADS * A_HEAD_DIM
M_HEADS = 4
M_HEAD_DIM = 128
M_WIDTH = M_HEADS * M_HEAD_DIM
C_WIDTH = 512
D_FF = 2816
EPS = 1e-6
LANES = 128
SUBLANES = 8
VMEM_LIMIT = 56 * 1024 * 1024

SM_AF = 0
SM_MI = 8
SM_MF = 12
SM_USED = 16
SM_MG = 16
SM_ROWS = 24

NT = (((1,), (1,)), ((), ()))


def _cparams(*sem):
    return pltpu.CompilerParams(dimension_semantics=sem, vmem_limit_bytes=VMEM_LIMIT)


def _const_spec(shape):
    nd = len(shape)
    return pl.BlockSpec(shape, lambda *_: (0,) * nd, pipeline_mode=pl.Buffered(1))


def _rms(x, g):
    return x * lax.rsqrt(jnp.mean(x * x, axis=-1, keepdims=True) + EPS) * g


def _log_sigmoid(x):
    return jnp.minimum(x, 0.0) - jnp.log1p(jnp.exp(-jnp.abs(x)))


def _sigmoid(x):
    return 1.0 / (1.0 + jnp.exp(-x))


def _dot(a, b):
    return jnp.dot(a, b, preferred_element_type=F32)


def _inproj_kernel(*refs, transposed_kv):
    if transposed_kv:
        (x_ref, g_ref, w_ref, b_ref, wt_ref, bt_ref, _, _,
         q_ref, k_ref, mq_ref, mk_ref, so_ref, sm_ref, kt_ref, vt_ref, mvt_ref) = refs
    else:
        x_ref, g_ref, w_ref, b_ref, q_ref, k_ref, v_ref, mq_ref, mk_ref, mv_ref, so_ref, sm_ref = refs
    h = _rms(x_ref[...], g_ref[...]).astype(BF16)

    def proj(c0, n):
        return _dot(h, w_ref[:, c0:c0 + n]) + b_ref[:, c0:c0 + n]

    q_ref[...] = (proj(0, A_WIDTH) * (A_HEAD_DIM ** -0.5)).astype(BF16)
    if transposed_kv:
        k_ref[...] = proj(A_WIDTH, A_WIDTH).astype(BF16)
        kt_ref[...] = (lax.dot_general(wt_ref[0:A_WIDTH, :], h, NT, preferred_element_type=F32)
                       + bt_ref[0:A_WIDTH, :])
        vt_ref[...] = (lax.dot_general(wt_ref[A_WIDTH:2 * A_WIDTH, :], h, NT, preferred_element_type=F32)
                       + bt_ref[A_WIDTH:2 * A_WIDTH, :])
        mvt_ref[...] = (lax.dot_general(wt_ref[2 * A_WIDTH:, :], h, NT, preferred_element_type=F32)
                        + bt_ref[2 * A_WIDTH:, :]).astype(BF16)
    else:
        k_ref[...] = proj(A_WIDTH, A_WIDTH)
        v_ref[...] = proj(2 * A_WIDTH, A_WIDTH)
    o = 3 * A_WIDTH
    mq_ref[...] = proj(o, M_WIDTH).astype(BF16)
    mk_ref[...] = (proj(o + M_WIDTH, M_WIDTH) * (M_HEAD_DIM ** -0.5)).astype(BF16)
    if not transposed_kv:
        mv_ref[...] = proj(o + 2 * M_WIDTH, M_WIDTH).astype(BF16)
    so_ref[...] = _sigmoid(proj(o + 3 * M_WIDTH, M_WIDTH))
    sm_ref[...] = proj(o + 4 * M_WIDTH, LANES)


def _inproj(x, g, w, b, tm, kv=None):
    n = x.shape[0]
    wcols = w.shape[1]
    row = lambda c: pl.BlockSpec((tm, c), lambda i: (i, 0))
    common = [
        jax.ShapeDtypeStruct((n, M_WIDTH), BF16),
        jax.ShapeDtypeStruct((n, M_WIDTH), BF16),
        jax.ShapeDtypeStruct((n, M_WIDTH), BF16),
        jax.ShapeDtypeStruct((n, M_WIDTH), F32),
        jax.ShapeDtypeStruct((n, LANES), F32),
    ]
    common_specs = [row(M_WIDTH)] * 4 + [row(LANES)]
    in_specs = [row(D_MODEL), _const_spec((1, D_MODEL)), _const_spec((D_MODEL, wcols)), _const_spec((1, wcols))]
    if kv is None:
        return pl.pallas_call(
            functools.partial(_inproj_kernel, transposed_kv=False),
            grid=(n // tm,),
            in_specs=in_specs,
            out_specs=[row(A_WIDTH)] * 3 + common_specs,
            out_shape=[jax.ShapeDtypeStruct((n, A_WIDTH), BF16), jax.ShapeDtypeStruct((n, A_WIDTH), F32),
                       jax.ShapeDtypeStruct((n, A_WIDTH), F32)] + common,
            compiler_params=_cparams("parallel"),
            name="inproj",
        )(x, g, w, b)
    nb, t, depth, layer = kv["nb"], kv["t"], kv["depth"], kv["layer"]
    tps = t // tm
    t_shape = jax.ShapeDtypeStruct((nb, depth, A_WIDTH, t), F32)
    t_spec = pl.BlockSpec((None, None, A_WIDTH, tm), lambda i: (i // tps, layer, 0, i % tps))
    prev = kv["prev"]
    if prev is None:
        prev = (jnp.zeros((SUBLANES, LANES), F32),) * 2
        aliases = {}
    else:
        aliases = {6: 6, 7: 7}
    any_spec = pl.BlockSpec(memory_space=pl.ANY)
    no_mv = lambda items: items[:2] + items[3:]
    return pl.pallas_call(
        functools.partial(_inproj_kernel, transposed_kv=True),
        grid=(n // tm,),
        in_specs=in_specs + [_const_spec(kv["wt"].shape), _const_spec(kv["bt"].shape), any_spec, any_spec],
        out_specs=[row(A_WIDTH)] * 2 + no_mv(common_specs) + [t_spec, t_spec,
                   pl.BlockSpec((None, M_WIDTH, tm), lambda i: (i // tps, 0, i % tps))],
        out_shape=[jax.ShapeDtypeStruct((n, A_WIDTH), BF16), jax.ShapeDtypeStruct((n, A_WIDTH), BF16)]
                  + no_mv(common) + [t_shape, t_shape, jax.ShapeDtypeStruct((nb, M_WIDTH, t), BF16)],
        input_output_aliases=aliases,
        compiler_params=_cparams("parallel"),
        name="inproj_t",
    )(x, g, w, b, kv["wt"], kv["bt"], *prev)


def _gates_kernel(sm_ref, col_ref, logf_ref, *, seq_a, seg_m):
    x = sm_ref[...]
    rows = x.shape[0]
    lane = lax.broadcasted_iota(jnp.int32, (1, LANES), 1)
    row = lax.broadcasted_iota(jnp.int32, (rows, 1), 0)
    ls = _log_sigmoid(x)
    logf_ref[...] = ls[:, SM_AF:SM_AF + A_HEADS]
    is_a = lane < SM_MI
    is_i = (lane >= SM_MI) & (lane < SM_MF)
    is_f = (lane >= SM_MF) & (lane < SM_USED)
    y = jnp.where(is_i, x, ls)
    rmod_a = row & (seq_a - 1)
    rmod_m = row & (seg_m - 1)
    s = 1
    while s < max(seq_a, seg_m):
        take = jnp.zeros((rows, LANES), jnp.bool_)
        if s < seq_a:
            take = take | (is_a & (rmod_a >= s))
        if s < seg_m:
            take = take | (is_f & (rmod_m >= s))
        y = y + jnp.where(take, pltpu.roll(y, s, axis=0), 0.0)
        s *= 2
    nm = SM_MF - SM_MI
    is_g = (lane >= SM_MG) & (lane < SM_MG + nm)
    g = pltpu.roll(y, SM_MG - SM_MI, axis=1) - pltpu.roll(y, SM_MG - SM_MF, axis=1)
    s = 1
    while s < seg_m:
        g = jnp.where(is_g & (rmod_m >= s), jnp.maximum(g, pltpu.roll(g, s, axis=0)), g)
        s *= 2
    col_ref[...] = jnp.where(is_g, g, y)


def _gates(sm, rows, seq_a, seg_m):
    n = sm.shape[0]
    return pl.pallas_call(
        functools.partial(_gates_kernel, seq_a=seq_a, seg_m=seg_m),
        grid=(n // rows,),
        in_specs=[pl.BlockSpec((rows, LANES), lambda i: (i, 0))],
        out_specs=[pl.BlockSpec((rows, LANES), lambda i: (i, 0)),
                   pl.BlockSpec((rows, A_HEADS), lambda i: (i, 0))],
        out_shape=[jax.ShapeDtypeStruct((n, LANES), F32),
                   jax.ShapeDtypeStruct((n, A_HEADS), F32)],
        compiler_params=_cparams("parallel"),
        name="gates",
    )(sm)


def _fox_kernel(q_ref, k_ref, vt_ref, col_ref, o_ref, vb_ref, ka_ref, sa_ref, sb_ref, pa_ref, pb_ref, *, tq):
    hp = pl.program_id(1)
    qi = pl.program_id(2)
    lane = lax.broadcasted_iota(jnp.int32, (1, LANES), 1)
    hd = A_HEAD_DIM

    aug0 = [hd * (1 - hh) for hh in range(2)]

    @pl.when(qi == 0)
    def _():
        vb_ref[...] = vt_ref[...].astype(BF16)
        colv = col_ref[...]
        kf = k_ref[...]
        for hh in range(2):
            ck = jnp.sum(jnp.where(lane == SM_AF + hp * 2 + hh, colv, 0.0), axis=1, keepdims=True)
            c1 = (-ck).astype(BF16)
            r1 = -ck - c1.astype(F32)
            c2 = r1.astype(BF16)
            c3 = (r1 - c2.astype(F32)).astype(BF16)
            a0 = aug0[hh]
            ka_ref[hh] = jnp.where(lane == a0, c1, jnp.where(lane == a0 + 1, c2, jnp.where(lane == a0 + 2, c3, kf)))

    q2 = q_ref[...]
    zero = jnp.zeros_like(q2)
    one = jnp.ones_like(q2)
    qm = [jnp.where((lane >= aug0[hh]) & (lane < aug0[hh] + 3), one,
                    jnp.where((lane >= hd) == bool(hh), q2, zero)) for hh in range(2)]

    def scores(j, s_out):
        start = pl.multiple_of(j * tq, tq)
        for hh in range(2):
            s_out[hh] = lax.dot_general(ka_ref[hh, pl.ds(start, tq), :], qm[hh], NT, preferred_element_type=F32)

    def values(j, p_in):
        start = pl.multiple_of(j * tq, tq)
        return [_dot(vb_ref[hh * hd:(hh + 1) * hd, pl.ds(start, tq)], p_in[hh]) for hh in range(2)]

    def softmax(s_in, p_out, ms, ls, diag):
        m_out, l_out, alphas = [], [], []
        for hh in range(2):
            s = s_in[hh]
            if diag:
                kpos = lax.broadcasted_iota(jnp.int32, (tq, tq), 0)
                qpos = lax.broadcasted_iota(jnp.int32, (tq, tq), 1)
                s = jnp.where(kpos <= qpos, s, -jnp.inf)
            m_new = jnp.maximum(ms[hh], jnp.max(s, axis=0, keepdims=True))
            alpha = jnp.exp(ms[hh] - m_new)
            p = jnp.exp(s - m_new)
            l_out.append(alpha * ls[hh] + jnp.sum(p, axis=0, keepdims=True))
            m_out.append(m_new)
            alphas.append(alpha)
            p_out[hh] = p.astype(BF16)
        return m_out, l_out, alphas

    def step(j, cur, nxt, carry):
        (s_cur, p_cur), (s_nxt, p_nxt) = cur, nxt
        a_prev, ms, ls, accs = carry
        pv = values(jnp.maximum(j - 1, 0), p_nxt)
        scores(j + 1, s_nxt)
        ms, ls, alphas = softmax(s_cur, p_cur, ms, ls, False)
        accs = [a_prev[hh] * accs[hh] + pv[hh] for hh in range(2)]
        return alphas, ms, ls, accs

    slot0, slot1 = (sa_ref, pa_ref), (sb_ref, pb_ref)
    odd = (qi & 1) == 1
    pa_ref[...] = jnp.zeros(pa_ref.shape, BF16)
    pb_ref[...] = jnp.zeros(pb_ref.shape, BF16)

    @pl.when(odd)
    def _():
        scores(0, sb_ref)

    @pl.when(jnp.logical_not(odd))
    def _():
        scores(0, sa_ref)

    two = lambda a: [a, a]
    carry = (two(jnp.ones((1, tq), F32)), two(jnp.full((1, tq), -jnp.inf, F32)), two(jnp.zeros((1, tq), F32)),
             two(jnp.zeros((hd, tq), F32)))
    carry = lax.cond(odd, lambda c: step(0, slot1, slot0, c), lambda c: c, carry)
    j0 = qi & 1

    def pair(jj, c):
        j = j0 + 2 * jj
        return step(j + 1, slot1, slot0, step(j, slot0, slot1, c))

    a_prev, ms, ls, accs = lax.fori_loop(0, qi // 2, pair, carry)
    pv = values(jnp.maximum(qi - 1, 0), pb_ref)
    ms, ls, alphas = softmax(sa_ref, pa_ref, ms, ls, True)
    pd = values(qi, pa_ref)
    yt = [(alphas[hh] * (a_prev[hh] * accs[hh] + pv[hh]) + pd[hh]) / ls[hh] for hh in range(2)]
    o_ref[...] = jnp.concatenate(yt, axis=0).T.astype(o_ref.dtype)


def _fox_prompt(q, k, vt, col, layer, nb, t, tq):
    n = q.shape[0]
    nq = t // tq
    pairs = A_WIDTH // LANES
    return pl.pallas_call(
        functools.partial(_fox_kernel, tq=tq),
        grid=(nb, pairs, nq),
        in_specs=[
            pl.BlockSpec((tq, LANES), lambda b, hp, qi: (b * nq + qi, hp)),
            pl.BlockSpec((t, LANES), lambda b, hp, qi: (b, hp)),
            pl.BlockSpec((None, None, LANES, t), lambda b, hp, qi: (b, layer, hp, 0)),
            pl.BlockSpec((t, LANES), lambda b, hp, qi: (b, 0)),
        ],
        out_specs=pl.BlockSpec((tq, LANES), lambda b, hp, qi: (b * nq + qi, hp)),
        out_shape=jax.ShapeDtypeStruct((n, A_WIDTH), BF16),
        scratch_shapes=[pltpu.VMEM((LANES, t), BF16), pltpu.VMEM((2, t, LANES), BF16),
                        pltpu.VMEM((2, tq, tq), F32), pltpu.VMEM((2, tq, tq), F32),
                        pltpu.VMEM((2, tq, tq), BF16), pltpu.VMEM((2, tq, tq), BF16)],
        compiler_params=_cparams("parallel", "parallel", "arbitrary"),
        name="fox_prompt",
    )(q, k, vt, col)


def _pool_scan_kernel(x_ref, o_ref):
    y = x_ref[...]
    lane = lax.broadcasted_iota(jnp.int32, (1, LANES), 1)
    s = 1
    while s < LANES:
        y = y + jnp.where(lane >= s, pltpu.roll(y, s, axis=1), 0.0)
        s *= 2
    o_ref[...] = y


def _pool_scan(x, rows):
    n = x.shape[0]
    return pl.pallas_call(
        _pool_scan_kernel,
        grid=(n // rows,),
        in_specs=[pl.BlockSpec((rows, LANES), lambda i: (i, 0))],
        out_specs=pl.BlockSpec((rows, LANES), lambda i: (i, 0)),
        out_shape=jax.ShapeDtypeStruct((n, LANES), F32),
        compiler_params=_cparams("parallel"),
        name="pool_scan",
    )(x)


def _fox_sample_kernel(pt_ref, q_ref, kn_ref, vn_ref, col_ref, rown_ref, *rest,
                       n_pages, layer, s_len):
    k_refs = rest[:n_pages]
    v_refs = rest[n_pages:2 * n_pages]
    lc_ref = rest[2 * n_pages]
    o_ref = rest[2 * n_pages + 1]
    s_ref = rest[2 * n_pages + 2]
    b = pl.program_id(0)
    lc_refs = [lc_ref.at[pt_ref[b, p]] for p in range(n_pages)]
    nh = A_HEADS
    rows = nh * s_len
    lane_w = lax.broadcasted_iota(jnp.int32, (1, A_WIDTH), 1)
    rid = lax.broadcasted_iota(jnp.int32, (rows, 1), 0)
    own = jnp.right_shift(lane_w, 6) == jnp.right_shift(rid, 3)
    q = q_ref[...].astype(F32)
    qbd = jnp.where(own, jnp.concatenate([q] * nh, axis=0), 0.0)
    colv = col_ref[...]
    rown = rown_ref[...]
    lo = layer * nh

    tot = [lc_refs[p][lo:lo + nh, LANES - 1:LANES] for p in range(n_pages)]
    base = [None] * n_pages
    suf = jnp.zeros((nh, 1), F32)
    for p in range(n_pages - 1, -1, -1):
        suf = suf + tot[p]
        base[p] = suf

    m_run = [None] * nh
    for p in range(n_pages):
        sp = _dot(qbd, k_refs[p][...])
        for h in range(nh):
            sl = slice(h * s_len, (h + 1) * s_len)
            cn = colv[:, SM_AF + h:SM_AF + h + 1]
            bias = (base[p][h:h + 1, :] - lc_refs[p][lo + h:lo + h + 1, :]) + cn
            sh = sp[sl, :] + bias
            s_ref[sl, p * LANES:(p + 1) * LANES] = sh
            mx = jnp.max(sh, axis=1, keepdims=True)
            m_run[h] = mx if m_run[h] is None else jnp.maximum(m_run[h], mx)
    sn = lax.dot_general(qbd, kn_ref[...], NT, preferred_element_type=F32)
    r = lax.broadcasted_iota(jnp.int32, (s_len, s_len), 0)
    c = lax.broadcasted_iota(jnp.int32, (s_len, s_len), 1)
    pn, m_all = [], []
    for h in range(nh):
        sl = slice(h * s_len, (h + 1) * s_len)
        cn = colv[:, SM_AF + h:SM_AF + h + 1]
        sh = sn[sl, :] + (cn - rown[SM_AF + h:SM_AF + h + 1, :])
        sh = jnp.where(c <= r, sh, -jnp.inf)
        pn.append(sh)
        m_all.append(jnp.maximum(m_run[h], jnp.max(sh, axis=1, keepdims=True)))
    m = jnp.concatenate(m_all, axis=0)
    p_new = jnp.exp(jnp.concatenate(pn, axis=0) - m)
    l = jnp.sum(p_new, axis=1, keepdims=True)
    acc = _dot(p_new, vn_ref[...])
    for p in range(n_pages):
        pp = jnp.exp(s_ref[:, p * LANES:(p + 1) * LANES] - m)
        l = l + jnp.sum(pp, axis=1, keepdims=True)
        acc = acc + lax.dot_general(pp, v_refs[p][...], NT, preferred_element_type=F32)
    acc = jnp.where(own, acc / l, 0.0)
    y = acc[0:s_len, :]
    for h in range(1, nh):
        y = y + acc[h * s_len:(h + 1) * s_len, :]
    o_ref[...] = y.astype(o_ref.dtype)


def _fox_sample(page_table, q, k_new, v_new, col, rown, cache_k, cache_v, lc, layer, nb, s_len):
    n_pages = page_table.shape[1]
    n_phys, page = cache_k.shape[0], cache_k.shape[1]
    depth = cache_k.shape[2]
    ck = cache_k.transpose(0, 2, 3, 4, 1).reshape(n_phys, depth, A_WIDTH, page)
    cv = cache_v.transpose(0, 2, 3, 4, 1).reshape(n_phys, depth, A_WIDTH, page)
    tok = lambda c: pl.BlockSpec((s_len, c), lambda b, pt: (b, 0))
    page_specs = [pl.BlockSpec((None, None, A_WIDTH, page),
                               lambda b, pt, j=j: (pt[b, j], layer, 0, 0)) for j in range(n_pages)]
    lc_spec = pl.BlockSpec(lc.shape, lambda b, pt: (0, 0, 0), pipeline_mode=pl.Buffered(1))
    grid_spec = pltpu.PrefetchScalarGridSpec(
        num_scalar_prefetch=1,
        grid=(nb,),
        in_specs=[tok(A_WIDTH), tok(A_WIDTH), tok(A_WIDTH), tok(LANES),
                  pl.BlockSpec((None, SM_ROWS, s_len), lambda b, pt: (b, 0, 0))]
                 + page_specs + page_specs + [lc_spec],
        out_specs=tok(A_WIDTH),
        scratch_shapes=[pltpu.VMEM((A_HEADS * s_len, n_pages * LANES), F32)],
    )
    return pl.pallas_call(
        functools.partial(_fox_sample_kernel, n_pages=n_pages, layer=layer, s_len=s_len),
        grid_spec=grid_spec,
        out_shape=jax.ShapeDtypeStruct((nb * s_len, A_WIDTH), BF16),
        compiler_params=_cparams("arbitrary"),
        name="fox_sample",
    )(page_table, q, k_new, v_new, col, rown, *([ck] * n_pages), *([cv] * n_pages), lc)


def _mlstm_kernel(q_ref, k_ref, v_ref, so_ref, col_ref, row_ref, g_ref, c0_ref, n0_ref, m0_ref, _,
                  y_ref, c1_ref, n1_ref, m1_ref, *, t, chunk, bb, lowp):
    nh, hd = M_HEADS, M_HEAD_DIM
    r = lax.broadcasted_iota(jnp.int32, (chunk, chunk), 0)
    c = lax.broadcasted_iota(jnp.int32, (chunk, chunk), 1)
    causal = c <= r
    g = g_ref[...]
    op = (lambda a: a.astype(BF16)) if lowp else (lambda a: a.astype(F32))
    heads = range(nh)
    hs = lambda h: slice(h * hd, (h + 1) * hd)

    def chunk_body(bi, ci, carry):
        cs, ns, ms = carry
        r0 = pl.multiple_of(bi * t + ci * chunk, chunk)
        l0 = pl.multiple_of(ci * chunk, chunk)
        qa = q_ref[pl.ds(r0, chunk), :]
        ka = k_ref[pl.ds(r0, chunk), :]
        va = v_ref[pl.ds(r0, chunk), :]
        colv = col_ref[pl.ds(r0, chunk), :]
        rows = row_ref[bi, :, pl.ds(l0, chunk)]
        qs = [op(qa[:, hs(h)]) for h in heads]
        ks = [op(ka[:, hs(h)]) for h in heads]
        vs = [op(va[:, hs(h)]) for h in heads]
        qk = [lax.dot_general(qs[h], ks[h], NT, preferred_element_type=F32) for h in heads]
        qc = [lax.dot_general(qs[h], op(cs[h]), NT, preferred_element_type=F32) for h in heads]
        wqk, a_, m_t_, fcol_, icol_ = [], [], [], [], []
        for h in heads:
            f_col = colv[:, SM_MF + h:SM_MF + h + 1]
            i_col = colv[:, SM_MI + h:SM_MI + h + 1]
            f_row = rows[SM_MF + h:SM_MF + h + 1, :]
            i_row = rows[SM_MI + h:SM_MI + h + 1, :]
            d = jnp.where(causal, f_col + (i_row - f_row), -jnp.inf)
            inter = ms[h] + f_col
            m_t = jnp.maximum(inter, jnp.max(d, axis=1, keepdims=True))
            wqk.append(jnp.exp(d - m_t) * qk[h])
            a_.append(jnp.exp(inter - m_t))
            m_t_.append(m_t)
            fcol_.append(f_col)
            icol_.append(i_col)
        pv = [_dot(op(wqk[h]), vs[h]) for h in heads]
        ys, cs_new, ns_new, ms_new = [], [], [], []
        for h in heads:
            a, m_t = a_[h], m_t_[h]
            qf = qa[:, hs(h)].astype(F32)
            num = a * qc[h] + pv[h]
            den = a * jnp.sum(qf * ns[h], axis=1, keepdims=True) + jnp.sum(wqk[h], axis=1, keepdims=True)
            h_t = num / jnp.maximum(jnp.abs(den), jnp.exp(-m_t))
            ys.append(h_t * lax.rsqrt(jnp.mean(h_t * h_t, axis=1, keepdims=True) + EPS))
            m_last = m_t[chunk - 1:chunk, :]
            a_last = a[chunk - 1:chunk, :]
            w_last = jnp.exp((icol_[h] - fcol_[h]) + (fcol_[h][chunk - 1:chunk, :] - m_last))
            vw = va[:, hs(h)].astype(F32) * w_last
            cs_new.append(a_last * cs[h] + _dot(op(vw.T), ks[h]))
            ns_new.append(a_last * ns[h] + jnp.sum(ka[:, hs(h)].astype(F32) * w_last, axis=0, keepdims=True))
            ms_new.append(m_last)
        hn = jnp.concatenate(ys, axis=1) * g
        y_ref[pl.ds(r0, chunk), :] = (so_ref[pl.ds(r0, chunk), :] * hn).astype(y_ref.dtype)
        return cs_new, ns_new, ms_new

    def seq_body(bi, _):
        init = ([c0_ref[bi, h] for h in heads], [n0_ref[bi, h] for h in heads], [m0_ref[bi, h] for h in heads])
        if t == chunk:
            cs, ns, ms = chunk_body(bi, 0, init)
        else:
            cs, ns, ms = lax.fori_loop(0, t // chunk, lambda ci, cr: chunk_body(bi, ci, cr), init)
        for h in heads:
            c1_ref[bi, h] = cs[h]
            n1_ref[bi, h] = ns[h]
            m1_ref[bi, h] = ms[h]
        return 0

    if bb == 1:
        seq_body(0, 0)
    else:
        lax.fori_loop(0, bb, seq_body, 0)


def _mlstm(q, k, v, so, col, rowt, g, c0, n0, m0, layer_in, c_prev, layer_out, depth, nb, t, chunk, bb):
    n = q.shape[0]
    nh, hd = M_HEADS, M_HEAD_DIM
    seq = lambda w: pl.BlockSpec((bb * t, w), lambda i: (i, 0))
    st_in = lambda a, b_: pl.BlockSpec((bb, None, nh, a, b_), lambda i: (i, layer_in, 0, 0, 0))
    st_out = lambda a, b_: pl.BlockSpec((bb, nh, a, b_), lambda i: (i, 0, 0, 0))
    if c_prev is None:
        c_prev = jnp.zeros((SUBLANES, LANES), F32)
        aliases = {}
    else:
        aliases = {10: 1}
    return pl.pallas_call(
        functools.partial(_mlstm_kernel, t=t, chunk=chunk, bb=bb, lowp=chunk >= 16),
        grid=(nb // bb,),
        in_specs=[seq(M_WIDTH), seq(M_WIDTH), seq(M_WIDTH), seq(M_WIDTH), seq(LANES),
                  pl.BlockSpec((bb, SM_ROWS, t), lambda i: (i, 0, 0)),
                  _const_spec((1, M_WIDTH)),
                  st_in(hd, hd), st_in(1, hd), st_in(1, 1),
                  pl.BlockSpec(memory_space=pl.ANY)],
        out_specs=[seq(M_WIDTH),
                   pl.BlockSpec((bb, None, nh, hd, hd), lambda i: (i, layer_out, 0, 0, 0)),
                   st_out(1, hd), st_out(1, 1)],
        out_shape=[jax.ShapeDtypeStruct((n, M_WIDTH), BF16),
                   jax.ShapeDtypeStruct((nb, depth, nh, hd, hd), F32),
                   jax.ShapeDtypeStruct((nb, nh, 1, hd), F32),
                   jax.ShapeDtypeStruct((nb, nh, 1, 1), F32)],
        input_output_aliases=aliases,
        compiler_params=_cparams("parallel"),
        name="mlstm",
    )(q, k, v, so, col, rowt, g, c0, n0.reshape(n0.shape[:3] + (1, hd)), m0.reshape(m0.shape[:3] + (1, 1)), c_prev)


C_EXT = M_HEAD_DIM + 16


def _mlstm_t_kernel(q_ref, k_ref, vt_ref, so_ref, col_ref, row_ref, g_ref, _,
                    y_ref, c1_ref, n1_ref, m1_ref, *, t, chunk):
    nh, hd = M_HEADS, M_HEAD_DIM
    s_idx = lax.broadcasted_iota(jnp.int32, (chunk, chunk), 0)
    l_idx = lax.broadcasted_iota(jnp.int32, (chunk, chunk), 1)
    causal = s_idx <= l_idx
    heads = range(nh)
    hs = lambda h: slice(h * hd, (h + 1) * hd)
    last = slice(chunk - 1, chunk)
    pad_rows = jnp.zeros((C_EXT - hd - 1, chunk), F32)

    def chunk_body(ci, carry):
        cs, ms = carry
        r0 = pl.multiple_of(ci * chunk, chunk)
        qa = q_ref[pl.ds(r0, chunk), :]
        ka = k_ref[pl.ds(r0, chunk), :]
        vta = vt_ref[:, pl.ds(r0, chunk)]
        colv = col_ref[pl.ds(r0, chunk), :]
        rows = row_ref[:, pl.ds(r0, chunk)]
        qk = [lax.dot_general(ka[:, hs(h)], qa[:, hs(h)], NT, preferred_element_type=F32) for h in heads]
        cq = [lax.dot_general(cs[h].astype(BF16), qa[:, hs(h)], NT, preferred_element_type=F32) for h in heads]
        wqk, a_, m_t_, wl_ = [], [], [], []
        for h in heads:
            f_row = rows[SM_MF + h:SM_MF + h + 1, :]
            i_row = rows[SM_MI + h:SM_MI + h + 1, :]
            g_max = rows[SM_MG + h:SM_MG + h + 1, :]
            g_col = colv[:, SM_MI + h:SM_MI + h + 1] - colv[:, SM_MF + h:SM_MF + h + 1]
            mx = jnp.maximum(ms[h], g_max)
            m_t = f_row + mx
            d = jnp.where(causal, f_row + g_col, -jnp.inf)
            wqk.append(jnp.exp(d - m_t) * qk[h])
            a_.append(jnp.exp(ms[h] - mx))
            m_t_.append(m_t)
            wl_.append(jnp.exp((i_row - f_row) + (f_row[:, last] - m_t[:, last])))
        pv = [_dot(vta[hs(h), :], wqk[h].astype(BF16)) for h in heads]
        ys, cs_new, ms_new = [], [], []
        for h in heads:
            a, m_t = a_[h], m_t_[h]
            num = a * cq[h][0:hd, :] + pv[h]
            den = a * cq[h][hd:hd + 1, :] + jnp.sum(wqk[h], axis=0, keepdims=True)
            h_t = num / jnp.maximum(jnp.abs(den), jnp.exp(-m_t))
            ys.append((h_t * lax.rsqrt(jnp.mean(h_t * h_t, axis=0, keepdims=True) + EPS)).T)
            vw = jnp.concatenate([vta[hs(h), :].astype(F32) * wl_[h], wl_[h], pad_rows], axis=0)
            cs_new.append(a[:, last] * cs[h] + _dot(vw.astype(BF16), ka[:, hs(h)]))
            ms_new.append(m_t[:, last])
        hn = jnp.concatenate(ys, axis=1) * g_ref[...]
        y_ref[pl.ds(r0, chunk), :] = (so_ref[pl.ds(r0, chunk), :] * hn).astype(y_ref.dtype)
        return cs_new, ms_new

    init = ([jnp.zeros((C_EXT, hd), F32)] * nh, [jnp.zeros((1, 1), F32)] * nh)
    cs, ms = lax.fori_loop(0, t // chunk, chunk_body, init)
    for h in heads:
        c1_ref[h] = cs[h][0:hd, :]
        n1_ref[h] = cs[h][hd:hd + 1, :]
        m1_ref[h] = ms[h]


def _mlstm_t(q, k, vt, so, col, rowt, g, c_prev, layer_out, depth, nb, t, chunk):
    n = q.shape[0]
    nh, hd = M_HEADS, M_HEAD_DIM
    seq = lambda w: pl.BlockSpec((t, w), lambda i: (i, 0))
    st_out = lambda a, b_: pl.BlockSpec((None, nh, a, b_), lambda i: (i, 0, 0, 0))
    if c_prev is None:
        c_prev = jnp.zeros((SUBLANES, LANES), F32)
        aliases = {}
    else:
        aliases = {7: 1}
    return pl.pallas_call(
        functools.partial(_mlstm_t_kernel, t=t, chunk=chunk),
        grid=(nb,),
        in_specs=[seq(M_WIDTH), seq(M_WIDTH), pl.BlockSpec((None, M_WIDTH, t), lambda i: (i, 0, 0)),
                  seq(M_WIDTH), seq(LANES), pl.BlockSpec((None, SM_ROWS, t), lambda i: (i, 0, 0)),
                  _const_spec((1, M_WIDTH)), pl.BlockSpec(memory_space=pl.ANY)],
        out_specs=[seq(M_WIDTH),
                   pl.BlockSpec((None, None, nh, hd, hd), lambda i: (i, layer_out, 0, 0, 0)),
                   st_out(1, hd), st_out(1, 1)],
        out_shape=[jax.ShapeDtypeStruct((n, M_WIDTH), BF16),
                   jax.ShapeDtypeStruct((nb, depth, nh, hd, hd), F32),
                   jax.ShapeDtypeStruct((nb, nh, 1, hd), F32),
                   jax.ShapeDtypeStruct((nb, nh, 1, 1), F32)],
        input_output_aliases=aliases,
        compiler_params=_cparams("parallel"),
        name="mlstm_t",
    )(q, k, vt, so, col, rowt, g, c_prev)


def _conv_carry_init(state_ref, carry_ref, tiles_per_seq):
    @pl.when((pl.program_id(0) % tiles_per_seq) == 0)
    def _():
        carry_ref[...] = state_ref[...]


def _conv_long(u, w_ref, c0, carry_ref):
    tm, ch = u.shape
    prev = carry_ref[:, c0:c0 + ch]
    row = lax.broadcasted_iota(jnp.int32, (tm, 1), 0)
    p1 = jnp.where(row == 0, prev[1:2, :], pltpu.roll(u, 1, axis=0))
    p2 = jnp.where(row == 0, prev[0:1, :], jnp.where(row == 1, prev[1:2, :], pltpu.roll(u, 2, axis=0)))
    carry_ref[:, c0:c0 + ch] = u[tm - 2:tm, :]
    return p2 * w_ref[0:1, c0:c0 + ch] + p1 * w_ref[1:2, c0:c0 + ch] + u * w_ref[2:3, c0:c0 + ch]


def _conv_short(u, w_ref, c0, fill_ref, s_len):
    tm, ch = u.shape
    fill = fill_ref[:, c0:c0 + ch]
    rmod = lax.broadcasted_iota(jnp.int32, (tm, 1), 0) & (s_len - 1)
    p1 = jnp.where(rmod >= 1, pltpu.roll(u, 1, axis=0), pltpu.roll(fill, tm - 1, axis=0))
    p2 = jnp.where(rmod >= 2, pltpu.roll(u, 2, axis=0), fill)
    return p2 * w_ref[0:1, c0:c0 + ch] + p1 * w_ref[1:2, c0:c0 + ch] + u * w_ref[2:3, c0:c0 + ch]


def _merge_kernel(x_ref, g_ref, wc_ref, bc_ref, wg_ref, bg_ref, ws_ref, ya_ref, ym_ref,
                  woa_ref, wom_ref, woc_ref, wo_ref, st_ref, x1_ref, cst_ref, *scratch,
                  long_seq, tiles_per_seq, s_len):
    x = x_ref[...]
    h = _rms(x, g_ref[...]).astype(BF16)
    cw = C_WIDTH
    cb = _dot(h, wc_ref[:, 0:cw]) + bc_ref[:, 0:cw]
    u = (_dot(h, wc_ref[:, cw:2 * cw]) + bc_ref[:, cw:2 * cw]) * (_dot(h, wc_ref[:, 2 * cw:3 * cw]) + bc_ref[:, 2 * cw:3 * cw])
    if long_seq:
        _conv_carry_init(st_ref, scratch[0], tiles_per_seq)
        uc = _conv_long(u, ws_ref, 0, scratch[0])
        cst_ref[...] = u[u.shape[0] - 2:, :]
    else:
        uc = _conv_short(u, ws_ref, 0, st_ref, s_len)
        cst_ref[...] = u
    yc = (cb * uc).astype(BF16)
    d = D_MODEL

    def gate(i):
        return _sigmoid(_dot(h, wg_ref[:, i * d:(i + 1) * d]) + bg_ref[:, i * d:(i + 1) * d])

    merged = gate(0) * _dot(ya_ref[...], woa_ref[...])
    merged = merged + gate(1) * _dot(ym_ref[...], wom_ref[...])
    merged = merged + gate(2) * _dot(yc, woc_ref[...])
    x1_ref[...] = x + _dot(merged.astype(BF16), wo_ref[...])


def _merge(x, g, wc, bc, wg, bg, ws, ya, ym, woa, wom, woc, wo, st, nb, t, tm):
    n = x.shape[0]
    long_seq = t >= tm
    tiles_per_seq = max(t // tm, 1)
    row = lambda c: pl.BlockSpec((tm, c), lambda i: (i, 0))
    if long_seq:
        st_spec = pl.BlockSpec((None, 2, C_WIDTH), lambda i: (i // tiles_per_seq, 0, 0))
        cst_spec = pl.BlockSpec((None, 2, C_WIDTH), lambda i: (i // tiles_per_seq, 0, 0))
        cst_shape = jax.ShapeDtypeStruct((nb, 2, C_WIDTH), F32)
        scratch = [pltpu.VMEM((2, C_WIDTH), F32)]
    else:
        st_spec = row(C_WIDTH)
        cst_spec = row(C_WIDTH)
        cst_shape = jax.ShapeDtypeStruct((n, C_WIDTH), F32)
        scratch = []
    return pl.pallas_call(
        functools.partial(_merge_kernel, long_seq=long_seq, tiles_per_seq=tiles_per_seq, s_len=t),
        grid=(n // tm,),
        in_specs=[row(D_MODEL), _const_spec((1, D_MODEL)),
                  _const_spec(wc.shape), _const_spec(bc.shape), _const_spec(wg.shape), _const_spec(bg.shape),
                  _const_spec(ws.shape), row(A_WIDTH), row(M_WIDTH),
                  _const_spec(woa.shape), _const_spec(wom.shape), _const_spec(woc.shape), _const_spec(wo.shape),
                  st_spec],
        out_specs=[row(D_MODEL), cst_spec],
        out_shape=[jax.ShapeDtypeStruct((n, D_MODEL), F32), cst_shape],
        scratch_shapes=scratch,
        compiler_params=_cparams("arbitrary"),
        name="merge",
    )(x, g, wc, bc, wg, bg, ws, ya, ym, woa, wom, woc, wo, st)


FF_CHUNK = D_FF // 2


def _ffn_kernel(x_ref, pe_ref, gf_ref, wup_ref, wfc_ref, bfc_ref, wdn_ref, gp_ref, wpg_ref, wpp_ref, gfin_ref,
                st_ref, xo_ref, fst_ref, *scratch, long_seq, tiles_per_seq, s_len, final):
    x = x_ref[...]
    tm = x.shape[0]
    h2 = _rms(x, gf_ref[...]).astype(BF16)
    acc = jnp.zeros((tm, D_MODEL), F32)
    if long_seq:
        _conv_carry_init(st_ref, scratch[0], tiles_per_seq)
    for half in range(D_FF // FF_CHUNK):
        c0 = half * FF_CHUNK
        ua = _dot(h2, wup_ref[:, c0:c0 + FF_CHUNK])
        ub = _dot(h2, wup_ref[:, D_FF + c0:D_FF + c0 + FF_CHUNK])
        if long_seq:
            uac = _conv_long(ua, wfc_ref, c0, scratch[0])
            fst_ref[:, c0:c0 + FF_CHUNK] = ua[tm - 2:, :]
        else:
            uac = _conv_short(ua, wfc_ref, c0, st_ref, s_len)
            fst_ref[:, c0:c0 + FF_CHUNK] = ua
        z = uac + bfc_ref[:, c0:c0 + FF_CHUNK]
        act = (z * _sigmoid(z) * ub).astype(BF16)
        acc = acc + _dot(act, wdn_ref[c0:c0 + FF_CHUNK, :])
    x2 = x + acc
    gate = _sigmoid(_dot(_rms(x2, gp_ref[...]).astype(BF16), wpg_ref[...]))
    x3 = x2 + gate * _dot(pe_ref[...].astype(BF16), wpp_ref[...])
    if final:
        x3 = _rms(x3, gfin_ref[...])
    xo_ref[...] = x3


def _ffn(x, pe, gf, wup, wfc, bfc, wdn, gp, wpg, wpp, gfin, st, nb, t, tm, final):
    n = x.shape[0]
    long_seq = t >= tm
    tiles_per_seq = max(t // tm, 1)
    row = lambda c: pl.BlockSpec((tm, c), lambda i: (i, 0))
    if long_seq:
        st_spec = pl.BlockSpec((None, 2, D_FF), lambda i: (i // tiles_per_seq, 0, 0))
        fst_spec = pl.BlockSpec((None, 2, D_FF), lambda i: (i // tiles_per_seq, 0, 0))
        fst_shape = jax.ShapeDtypeStruct((nb, 2, D_FF), F32)
        scratch = [pltpu.VMEM((2, D_FF), F32)]
    else:
        st_spec = row(D_FF)
        fst_spec = row(D_FF)
        fst_shape = jax.ShapeDtypeStruct((n, D_FF), F32)
        scratch = []
    return pl.pallas_call(
        functools.partial(_ffn_kernel, long_seq=long_seq, tiles_per_seq=tiles_per_seq, s_len=t, final=final),
        grid=(n // tm,),
        in_specs=[row(D_MODEL), row(pe.shape[1]), _const_spec((1, D_MODEL)),
                  _const_spec(wup.shape), _const_spec(wfc.shape), _const_spec(bfc.shape), _const_spec(wdn.shape),
                  _const_spec((1, D_MODEL)), _const_spec(wpg.shape), _const_spec(wpp.shape),
                  _const_spec((1, D_MODEL)), st_spec],
        out_specs=[row(D_MODEL), fst_spec],
        out_shape=[jax.ShapeDtypeStruct((n, D_MODEL), F32), fst_shape],
        scratch_shapes=scratch,
        compiler_params=_cparams("arbitrary"),
        name="ffn",
    )(x, pe, gf, wup, wfc, bfc, wdn, gp, wpg, wpp, gfin, st)


def _row_form(col, nb, t):
    return col[:, :SM_ROWS].reshape(nb, t, SM_ROWS).transpose(0, 2, 1)


def _short_fill(state, s_len):
    nb, _, ch = state.shape
    return jnp.pad(state, ((0, 0), (0, s_len - 2), (0, 0))).reshape(nb * s_len, ch)


def _pick(n, pref):
    return pref if n % pref == 0 else n


def _layer(x, pe, w, nb, t, chunk, past, final, carry):
    n = x.shape[0]
    layer, depth = carry["layer"], carry["depth"]
    tm_in = _pick(n, 512)
    tm_merge = _pick(n, 512)
    tm_ffn = _pick(n, 256)
    if past is None:
        kvp = dict(layer=layer, depth=depth, prev=carry["kvt"], nb=nb, t=t, wt=w["w_kvt"], bt=w["b_kvt"])
        q, k, mq, mk, so, sm, kt, vt, mvt = _inproj(x, w["g_mix"], w["w_am"], w["b_am"], tm_in, kvp)
        col, logf = _gates(sm, t, t, chunk)
        rowt = _row_form(col, nb, t)
        ya = _fox_prompt(q, k, vt, col, layer, nb, t, _pick(t, 512))
        k, v = kt, vt
        ym, c1, n1, m1 = _mlstm_t(mq, mk, mvt, so, col, rowt, w["g_mhead"], carry["c"], layer, depth, nb, t, chunk)
        conv0 = jnp.zeros((nb, 2, C_WIDTH), F32)
        ffn0 = jnp.zeros((nb, 2, D_FF), F32)
    else:
        q, k, v, mq, mk, mv, so, sm = _inproj(x, w["g_mix"], w["w_am"], w["b_am"], tm_in)
        col, logf = _gates(sm, n, t, t)
        rowt = _row_form(col, nb, t)
        ya = _fox_sample(past["page_table"], q, k, v, col, rowt, past["cache_k"], past["cache_v"],
                         past["lc"], layer, nb, t)
        conv0, ffn0 = past["conv0"], past["ffn0"]
        ym, c1, n1, m1 = _mlstm(mq, mk, mv, so, col, rowt, w["g_mhead"], past["c0"], past["n0"], past["m0"], layer,
                                carry["c"], layer, depth, nb, t, chunk, _pick(nb, SUBLANES))
    long_merge = t >= tm_merge
    long_ffn = t >= tm_ffn
    x1, conv1 = _merge(x, w["g_mix"], w["w_c"], w["b_c"], w["w_g"], w["b_g"], w["w_sconv"], ya, ym,
                       w["w_oa"], w["w_om"], w["w_oc"], w["w_o"],
                       conv0 if long_merge else _short_fill(conv0, t), nb, t, tm_merge)
    x3, ffn1 = _ffn(x1, pe, w["g_ffn"], w["w_up"], w["w_fconv"], w["b_fconv"], w["w_down"],
                    w["g_ple"], w["w_ple_gate"], w["w_ple"], w["g_final"],
                    ffn0 if long_ffn else _short_fill(ffn0, t), nb, t, tm_ffn, final)
    if not long_merge:
        conv1 = conv1.reshape(nb, t, C_WIDTH)[:, t - 2:, :]
    if not long_ffn:
        ffn1 = ffn1.reshape(nb, t, D_FF)[:, t - 2:, :]
    if past is not None:
        k = k.reshape(nb, t, A_HEADS, A_HEAD_DIM)
        v = v.reshape(nb, t, A_HEADS, A_HEAD_DIM)
    state = (k, v, logf.reshape(nb, t, A_HEADS), c1, n1.reshape(nb, M_HEADS, M_HEAD_DIM),
             m1.reshape(nb, M_HEADS), conv1, ffn1)
    return x3, state


def _layer_weights(l, w_in, b_in, g_mix, g_mhead, w_sconv, w_oa, w_om, w_oc, w_o, g_ffn, w_up,
                   w_fconv, b_fconv, w_down, g_ple, w_ple_gate, w_ple, g_final):
    a0 = 0
    af0 = 3 * A_WIDTH
    m0 = af0 + A_HEADS
    mi0 = m0 + 3 * M_WIDTH
    mo0 = mi0 + 2 * M_HEADS
    c0 = mo0 + M_WIDTH
    g0 = c0 + 3 * C_WIDTH
    wl, bl = w_in[l], b_in[l]
    pad = LANES - SM_USED

    def cols(a, lo, hi):
        return a[..., lo:hi]

    def am(a):
        return jnp.concatenate(
            [cols(a, a0, af0), cols(a, m0, mi0), cols(a, mo0, c0),
             cols(a, af0, m0), cols(a, mi0, mo0), jnp.zeros(a.shape[:-1] + (pad,), a.dtype)], axis=-1)

    row = lambda a: a.reshape(1, -1)
    return {
        "w_am": am(wl).astype(BF16), "b_am": row(am(bl)),
        "w_kvt": jnp.concatenate([cols(wl, A_WIDTH, af0), cols(wl, m0 + 2 * M_WIDTH, mi0)], axis=-1).T.astype(BF16),
        "b_kvt": jnp.concatenate([cols(bl, A_WIDTH, af0), cols(bl, m0 + 2 * M_WIDTH, mi0)], axis=-1).reshape(-1, 1),
        "w_c": cols(wl, c0, g0).astype(BF16), "b_c": row(cols(bl, c0, g0)),
        "w_g": cols(wl, g0, g0 + 3 * D_MODEL).astype(BF16), "b_g": row(cols(bl, g0, g0 + 3 * D_MODEL)),
        "g_mix": row(g_mix[l]), "g_mhead": row(g_mhead[l]), "w_sconv": w_sconv[l],
        "w_oa": w_oa[l].astype(BF16), "w_om": w_om[l].astype(BF16), "w_oc": w_oc[l].astype(BF16),
        "w_o": w_o[l].astype(BF16), "g_ffn": row(g_ffn[l]), "w_up": w_up[l].astype(BF16),
        "w_fconv": w_fconv[l], "b_fconv": row(b_fconv[l]), "w_down": w_down[l].astype(BF16),
        "g_ple": row(g_ple[l]), "w_ple_gate": w_ple_gate[l].astype(BF16), "w_ple": w_ple[l].astype(BF16),
        "g_final": row(g_final),
    }


def kernel(x_prompt, x_sample, cache_k, cache_v, cache_logf, state_mlstm_C, state_mlstm_n, state_mlstm_m, state_conv, state_ffn_conv, page_table, p_prompt, p_sample, w_in, b_in, g_mix, g_mhead, w_sconv, w_oa, w_om, w_oc, w_o, g_ffn, w_up, w_fconv, b_fconv, w_down, g_ple, w_ple_gate, w_ple, g_final):
    nbp, tp, _ = x_prompt.shape
    nbs, ts, _ = x_sample.shape
    depth = w_in.shape[0]
    n_phys, page = cache_logf.shape[0], cache_logf.shape[1]
    assert page == LANES and ts == SUBLANES and depth * A_HEADS == SM_USED

    lf_t = cache_logf.reshape(n_phys, page, depth * A_HEADS).transpose(0, 2, 1).reshape(-1, LANES)
    lc = _pool_scan(lf_t, _pick(lf_t.shape[0], 2048)).reshape(n_phys, depth * A_HEADS, LANES)

    xp = x_prompt.reshape(nbp * tp, D_MODEL)
    xs = x_sample.reshape(nbs * ts, D_MODEL)
    chunk_p = _pick(tp, 256)
    new_p, new_s = [], []
    kvt = c_p = c_s = None
    for l in range(depth):
        w = _layer_weights(l, w_in, b_in, g_mix, g_mhead, w_sconv, w_oa, w_om, w_oc, w_o, g_ffn, w_up,
                           w_fconv, b_fconv, w_down, g_ple, w_ple_gate, w_ple, g_final)
        final = l == depth - 1
        xp, st_p = _layer(xp, p_prompt[l].reshape(nbp * tp, -1), w, nbp, tp, chunk_p, None, final,
                          {"layer": l, "depth": depth, "kvt": kvt, "c": c_p})
        kvt, c_p = st_p[:2], st_p[3]
        past = {"page_table": page_table, "cache_k": cache_k, "cache_v": cache_v, "lc": lc,
                "c0": state_mlstm_C, "n0": state_mlstm_n, "m0": state_mlstm_m,
                "conv0": state_conv[:, l], "ffn0": state_ffn_conv[:, l]}
        xs, st_s = _layer(xs, p_sample[l].reshape(nbs * ts, -1), w, nbs, ts, ts, past, final,
                          {"layer": l, "depth": depth, "kvt": None, "c": c_s})
        c_s = st_s[3]
        new_p.append(st_p)
        new_s.append(st_s)

    def stack(per_layer, c_all, first):
        rows = [jnp.stack([s[i] for s in per_layer], axis=2) for i in range(first, 3)]
        states = [jnp.stack([s[i] for s in per_layer], axis=1) for i in range(4, 8)]
        return tuple(rows + [c_all] + states)

    kv_p = tuple(a.reshape(nbp, depth, A_HEADS, A_HEAD_DIM, tp).transpose(0, 4, 1, 2, 3) for a in kvt)
    return ((xp.reshape(nbp, tp, D_MODEL), xs.reshape(nbs, ts, D_MODEL)) + kv_p + stack(new_p, c_p, 2)
            + stack(new_s, c_s, 0))
```

```python
import functools

import jax
import jax.numpy as jnp
from jax import lax
from jax.experimental import pallas as pl
from jax.experimental.pallas import tpu as pltpu

F32 = jnp.float32
BF16 = jnp.bfloat16

D_MODEL = 1024
A_HEADS = 8
A_HEAD_DIM = 64
A_WIDTH = A_HEADS * A_HEAD_DIM
M_HEADS = 4
M_HEAD_DIM = 128
M_WIDTH = M_HEADS * M_HEAD_DIM
C_WIDTH = 512
D_FF = 2816
EPS = 1e-6
LANES = 128
SUBLANES = 8
VMEM_LIMIT = 56 * 1024 * 1024

SM_AF = 0
SM_MI = 8
SM_MF = 12
SM_USED = 16
SM_MG = 16
SM_ROWS = 24

NT = (((1,), (1,)), ((), ()))
LOG2E = 1.4426950408889634


def _cparams(*sem):
    return pltpu.CompilerParams(dimension_semantics=sem, vmem_limit_bytes=VMEM_LIMIT)


def _const_spec(shape):
    nd = len(shape)
    return pl.BlockSpec(shape, lambda *_: (0,) * nd, pipeline_mode=pl.Buffered(1))


def _rms(x, g):
    return x * lax.rsqrt(jnp.mean(x * x, axis=-1, keepdims=True) + EPS) * g


def _log_sigmoid(x):
    return jnp.minimum(x, 0.0) - jnp.log1p(jnp.exp(-jnp.abs(x)))


def _sigmoid(x):
    return 1.0 / (1.0 + jnp.exp(-x))


def _dot(a, b):
    return jnp.dot(a, b, preferred_element_type=F32)


def _inproj_kernel(*refs, transposed_kv):
    if transposed_kv:
        (x_ref, g_ref, w_ref, b_ref, wt_ref, bt_ref, _, _,
         q_ref, k_ref, mq_ref, mk_ref, so_ref, sm_ref, kt_ref, vt_ref, mvt_ref) = refs
    else:
        x_ref, g_ref, w_ref, b_ref, q_ref, k_ref, v_ref, mq_ref, mk_ref, mv_ref, so_ref, sm_ref = refs
    h = _rms(x_ref[...], g_ref[...]).astype(BF16)

    def proj(c0, n):
        return _dot(h, w_ref[:, c0:c0 + n]) + b_ref[:, c0:c0 + n]

    q_scale = A_HEAD_DIM ** -0.5 * (LOG2E if transposed_kv else 1.0)
    q_ref[...] = (proj(0, A_WIDTH) * q_scale).astype(BF16)
    if transposed_kv:
        k_ref[...] = proj(A_WIDTH, A_WIDTH).astype(BF16)
        kt_ref[...] = (lax.dot_general(wt_ref[0:A_WIDTH, :], h, NT, preferred_element_type=F32)
                       + bt_ref[0:A_WIDTH, :])
        vt_ref[...] = (lax.dot_general(wt_ref[A_WIDTH:2 * A_WIDTH, :], h, NT, preferred_element_type=F32)
                       + bt_ref[A_WIDTH:2 * A_WIDTH, :])
        mvt_ref[...] = (lax.dot_general(wt_ref[2 * A_WIDTH:, :], h, NT, preferred_element_type=F32)
                        + bt_ref[2 * A_WIDTH:, :]).astype(BF16)
    else:
        k_ref[...] = proj(A_WIDTH, A_WIDTH)
        v_ref[...] = proj(2 * A_WIDTH, A_WIDTH)
    o = 3 * A_WIDTH
    mq_ref[...] = proj(o, M_WIDTH).astype(BF16)
    mk_ref[...] = (proj(o + M_WIDTH, M_WIDTH) * (M_HEAD_DIM ** -0.5)).astype(BF16)
    if not transposed_kv:
        mv_ref[...] = proj(o + 2 * M_WIDTH, M_WIDTH).astype(BF16)
    so_ref[...] = _sigmoid(proj(o + 3 * M_WIDTH, M_WIDTH))
    sm_ref[...] = proj(o + 4 * M_WIDTH, LANES)


def _inproj(x, g, w, b, tm, kv=None):
    n = x.shape[0]
    wcols = w.shape[1]
    row = lambda c: pl.BlockSpec((tm, c), lambda i: (i, 0))
    common = [
        jax.ShapeDtypeStruct((n, M_WIDTH), BF16),
        jax.ShapeDtypeStruct((n, M_WIDTH), BF16),
        jax.ShapeDtypeStruct((n, M_WIDTH), BF16),
        jax.ShapeDtypeStruct((n, M_WIDTH), F32),
        jax.ShapeDtypeStruct((n, LANES), F32),
    ]
    common_specs = [row(M_WIDTH)] * 4 + [row(LANES)]
    in_specs = [row(D_MODEL), _const_spec((1, D_MODEL)), _const_spec((D_MODEL, wcols)), _const_spec((1, wcols))]
    if kv is None:
        return pl.pallas_call(
            functools.partial(_inproj_kernel, transposed_kv=False),
            grid=(n // tm,),
            in_specs=in_specs,
            out_specs=[row(A_WIDTH)] * 3 + common_specs,
            out_shape=[jax.ShapeDtypeStruct((n, A_WIDTH), BF16), jax.ShapeDtypeStruct((n, A_WIDTH), F32),
                       jax.ShapeDtypeStruct((n, A_WIDTH), F32)] + common,
            compiler_params=_cparams("parallel"),
            name="inproj",
        )(x, g, w, b)
    nb, t, depth, layer = kv["nb"], kv["t"], kv["depth"], kv["layer"]
    tps = t // tm
    t_shape = jax.ShapeDtypeStruct((nb, depth, A_WIDTH, t), F32)
    t_spec = pl.BlockSpec((None, None, A_WIDTH, tm), lambda i: (i // tps, layer, 0, i % tps))
    prev = kv["prev"]
    if prev is None:
        prev = (jnp.zeros((SUBLANES, LANES), F32),) * 2
        aliases = {}
    else:
        aliases = {6: 6, 7: 7}
    any_spec = pl.BlockSpec(memory_space=pl.ANY)
    no_mv = lambda items: items[:2] + items[3:]
    return pl.pallas_call(
        functools.partial(_inproj_kernel, transposed_kv=True),
        grid=(n // tm,),
        in_specs=in_specs + [_const_spec(kv["wt"].shape), _const_spec(kv["bt"].shape), any_spec, any_spec],
        out_specs=[row(A_WIDTH)] * 2 + no_mv(common_specs) + [t_spec, t_spec,
                   pl.BlockSpec((None, M_WIDTH, tm), lambda i: (i // tps, 0, i % tps))],
        out_shape=[jax.ShapeDtypeStruct((n, A_WIDTH), BF16), jax.ShapeDtypeStruct((n, A_WIDTH), BF16)]
                  + no_mv(common) + [t_shape, t_shape, jax.ShapeDtypeStruct((nb, M_WIDTH, t), BF16)],
        input_output_aliases=aliases,
        compiler_params=_cparams("parallel"),
        name="inproj_t",
    )(x, g, w, b, kv["wt"], kv["bt"], *prev)


def _gates_kernel(sm_ref, col_ref, logf_ref, *, seq_a, seg_m):
    x = sm_ref[...]
    rows = x.shape[0]
    lane = lax.broadcasted_iota(jnp.int32, (1, LANES), 1)
    row = lax.broadcasted_iota(jnp.int32, (rows, 1), 0)
    ls = _log_sigmoid(x)
    logf_ref[...] = ls[:, SM_AF:SM_AF + A_HEADS]
    is_a = lane < SM_MI
    is_i = (lane >= SM_MI) & (lane < SM_MF)
    is_f = (lane >= SM_MF) & (lane < SM_USED)
    y = jnp.where(is_i, x, ls)
    rmod_a = row & (seq_a - 1)
    rmod_m = row & (seg_m - 1)
    s = 1
    while s < max(seq_a, seg_m):
        take = jnp.zeros((rows, LANES), jnp.bool_)
        if s < seq_a:
            take = take | (is_a & (rmod_a >= s))
        if s < seg_m:
            take = take | (is_f & (rmod_m >= s))
        y = y + jnp.where(take, pltpu.roll(y, s, axis=0), 0.0)
        s *= 2
    nm = SM_MF - SM_MI
    is_g = (lane >= SM_MG) & (lane < SM_MG + nm)
    g = pltpu.roll(y, SM_MG - SM_MI, axis=1) - pltpu.roll(y, SM_MG - SM_MF, axis=1)
    s = 1
    while s < seg_m:
        g = jnp.where(is_g & (rmod_m >= s), jnp.maximum(g, pltpu.roll(g, s, axis=0)), g)
        s *= 2
    col_ref[...] = jnp.where(is_g, g, y)


def _gates(sm, rows, seq_a, seg_m):
    n = sm.shape[0]
    return pl.pallas_call(
        functools.partial(_gates_kernel, seq_a=seq_a, seg_m=seg_m),
        grid=(n // rows,),
        in_specs=[pl.BlockSpec((rows, LANES), lambda i: (i, 0))],
        out_specs=[pl.BlockSpec((rows, LANES), lambda i: (i, 0)),
                   pl.BlockSpec((rows, A_HEADS), lambda i: (i, 0))],
        out_shape=[jax.ShapeDtypeStruct((n, LANES), F32),
                   jax.ShapeDtypeStruct((n, A_HEADS), F32)],
        compiler_params=_cparams("parallel"),
        name="gates",
    )(sm)


def _fox_kernel(q_ref, k_ref, vt_ref, col_ref, o_ref, vb_ref, ka_ref, sa_ref, sb_ref, pa_ref, pb_ref, *, tq, t):
    hp = pl.program_id(1)
    lane = lax.broadcasted_iota(jnp.int32, (1, LANES), 1)
    hd = A_HEAD_DIM
    aug0 = [hd * (1 - hh) for hh in range(2)]

    vb_ref[...] = vt_ref[...].astype(BF16)
    colv = col_ref[...]
    kf = k_ref[...]
    for hh in range(2):
        ck = jnp.sum(jnp.where(lane == SM_AF + hp * 2 + hh, colv, 0.0), axis=1, keepdims=True) * (-LOG2E)
        c1 = ck.astype(BF16)
        r1 = ck - c1.astype(F32)
        c2 = r1.astype(BF16)
        c3 = (r1 - c2.astype(F32)).astype(BF16)
        a0 = aug0[hh]
        ka_ref[hh] = jnp.where(lane == a0, c1, jnp.where(lane == a0 + 1, c2, jnp.where(lane == a0 + 2, c3, kf)))

    slots = ((sa_ref, pa_ref), (sb_ref, pb_ref))
    kpos = lax.broadcasted_iota(jnp.int32, (tq, tq), 0)
    qpos = lax.broadcasted_iota(jnp.int32, (tq, tq), 1)

    def values(j, p_in):
        return [_dot(vb_ref[hh * hd:(hh + 1) * hd, j * tq:(j + 1) * tq], p_in[hh]) for hh in range(2)]

    for qi in range(t // tq):
        q2 = q_ref[qi * tq:(qi + 1) * tq, :]
        zero = jnp.zeros_like(q2)
        one = jnp.ones_like(q2)
        qm = [jnp.where((lane >= aug0[hh]) & (lane < aug0[hh] + 3), one,
                        jnp.where((lane >= hd) == bool(hh), q2, zero)) for hh in range(2)]

        def scores(j, s_out):
            for hh in range(2):
                s_out[hh] = lax.dot_general(ka_ref[hh, j * tq:(j + 1) * tq, :], qm[hh], NT,
                                            preferred_element_type=F32)

        def softmax(s_in, p_out, ms, ls, diag):
            m_out, l_out, alphas = [], [], []
            for hh in range(2):
                s = s_in[hh]
                if diag:
                    s = jnp.where(kpos <= qpos, s, -jnp.inf)
                m_new = jnp.max(s, axis=0, keepdims=True)
                if ms is not None:
                    m_new = jnp.maximum(ms[hh], m_new)
                p = jnp.exp2(s - m_new)
                l_new = jnp.sum(p, axis=0, keepdims=True)
                if ms is not None:
                    alpha = jnp.exp2(ms[hh] - m_new)
                    l_new = alpha * ls[hh] + l_new
                    alphas.append(alpha)
                l_out.append(l_new)
                m_out.append(m_new)
                p_out[hh] = p.astype(BF16)
            return m_out, l_out, alphas

        scores(0, slots[qi % 2][0])
        ms = ls = a_prev = accs = None
        for j in range(qi):
            (s_cur, p_cur), (s_nxt, p_nxt) = slots[(qi - j) % 2], slots[(qi - j - 1) % 2]
            pv = values(j - 1, p_nxt) if j > 0 else None
            scores(j + 1, s_nxt)
            ms, ls, alphas = softmax(s_cur, p_cur, ms, ls, False)
            if j == 1:
                accs = pv
            elif j > 1:
                accs = [a_prev[hh] * accs[hh] + pv[hh] for hh in range(2)]
            a_prev = alphas
        pv = values(qi - 1, pb_ref) if qi > 0 else None
        ms, ls, alphas = softmax(sa_ref, pa_ref, ms, ls, True)
        pd = values(qi, pa_ref)
        if qi == 0:
            yt = [pd[hh] / ls[hh] for hh in range(2)]
        elif qi == 1:
            yt = [(alphas[hh] * pv[hh] + pd[hh]) / ls[hh] for hh in range(2)]
        else:
            yt = [(alphas[hh] * (a_prev[hh] * accs[hh] + pv[hh]) + pd[hh]) / ls[hh] for hh in range(2)]
        o_ref[qi * tq:(qi + 1) * tq, :] = jnp.concatenate(yt, axis=0).T.astype(o_ref.dtype)


def _fox_prompt(q, k, vt, col, layer, nb, t, tq):
    n = q.shape[0]
    nq = t // tq
    pairs = A_WIDTH // LANES
    seq = pl.BlockSpec((t, LANES), lambda b, hp: (b, hp))
    return pl.pallas_call(
        functools.partial(_fox_kernel, tq=tq, t=t),
        grid=(nb, pairs),
        in_specs=[seq, seq,
                  pl.BlockSpec((None, None, LANES, t), lambda b, hp: (b, layer, hp, 0)),
                  pl.BlockSpec((t, LANES), lambda b, hp: (b, 0))],
        out_specs=seq,
        out_shape=jax.ShapeDtypeStruct((n, A_WIDTH), BF16),
        scratch_shapes=[pltpu.VMEM((LANES, t), BF16), pltpu.VMEM((2, t, LANES), BF16),
                        pltpu.VMEM((2, tq, tq), F32), pltpu.VMEM((2, tq, tq), F32),
                        pltpu.VMEM((2, tq, tq), BF16), pltpu.VMEM((2, tq, tq), BF16)],
        compiler_params=_cparams("parallel", "parallel"),
        name="fox_prompt",
    )(q, k, vt, col)


def _pool_scan_kernel(x_ref, o_ref):
    y = x_ref[...]
    lane = lax.broadcasted_iota(jnp.int32, (1, LANES), 1)
    s = 1
    while s < LANES:
        y = y + jnp.where(lane >= s, pltpu.roll(y, s, axis=1), 0.0)
        s *= 2
    o_ref[...] = y


def _pool_scan(x, rows):
    n = x.shape[0]
    return pl.pallas_call(
        _pool_scan_kernel,
        grid=(n // rows,),
        in_specs=[pl.BlockSpec((rows, LANES), lambda i: (i, 0))],
        out_specs=pl.BlockSpec((rows, LANES), lambda i: (i, 0)),
        out_shape=jax.ShapeDtypeStruct((n, LANES), F32),
        compiler_params=_cparams("parallel"),
        name="pool_scan",
    )(x)


def _fox_sample_kernel(pt_ref, q_ref, kn_ref, vn_ref, col_ref, rown_ref, *rest,
                       n_pages, layer, s_len):
    k_refs = rest[:n_pages]
    v_refs = rest[n_pages:2 * n_pages]
    lc_ref = rest[2 * n_pages]
    o_ref = rest[2 * n_pages + 1]
    s_ref = rest[2 * n_pages + 2]
    b = pl.program_id(0)
    lc_refs = [lc_ref.at[pt_ref[b, p]] for p in range(n_pages)]
    nh = A_HEADS
    rows = nh * s_len
    lane_w = lax.broadcasted_iota(jnp.int32, (1, A_WIDTH), 1)
    rid = lax.broadcasted_iota(jnp.int32, (rows, 1), 0)
    own = jnp.right_shift(lane_w, 6) == jnp.right_shift(rid, 3)
    q = q_ref[...].astype(F32)
    qbd = jnp.where(own, jnp.concatenate([q] * nh, axis=0), 0.0)
    colv = col_ref[...]
    rown = rown_ref[...]
    lo = layer * nh

    tot = [lc_refs[p][lo:lo + nh, LANES - 1:LANES] for p in range(n_pages)]
    base = [None] * n_pages
    suf = jnp.zeros((nh, 1), F32)
    for p in range(n_pages - 1, -1, -1):
        suf = suf + tot[p]
        base[p] = suf

    m_run = [None] * nh
    for p in range(n_pages):
        sp = _dot(qbd, k_refs[p][...])
        for h in range(nh):
            sl = slice(h * s_len, (h + 1) * s_len)
            cn = colv[:, SM_AF + h:SM_AF + h + 1]
            bias = (base[p][h:h + 1, :] - lc_refs[p][lo + h:lo + h + 1, :]) + cn
            sh = sp[sl, :] + bias
            s_ref[sl, p * LANES:(p + 1) * LANES] = sh
            mx = jnp.max(sh, axis=1, keepdims=True)
            m_run[h] = mx if m_run[h] is None else jnp.maximum(m_run[h], mx)
    sn = lax.dot_general(qbd, kn_ref[...], NT, preferred_element_type=F32)
    r = lax.broadcasted_iota(jnp.int32, (s_len, s_len), 0)
    c = lax.broadcasted_iota(jnp.int32, (s_len, s_len), 1)
    pn, m_all = [], []
    for h in range(nh):
        sl = slice(h * s_len, (h + 1) * s_len)
        cn = colv[:, SM_AF + h:SM_AF + h + 1]
        sh = sn[sl, :] + (cn - rown[SM_AF + h:SM_AF + h + 1, :])
        sh = jnp.where(c <= r, sh, -jnp.inf)
        pn.append(sh)
        m_all.append(jnp.maximum(m_run[h], jnp.max(sh, axis=1, keepdims=True)))
    m = jnp.concatenate(m_all, axis=0)
    p_new = jnp.exp(jnp.concatenate(pn, axis=0) - m)
    l = jnp.sum(p_new, axis=1, keepdims=True)
    acc = _dot(p_new, vn_ref[...])
    for p in range(n_pages):
        pp = jnp.exp(s_ref[:, p * LANES:(p + 1) * LANES] - m)
        l = l + jnp.sum(pp, axis=1, keepdims=True)
        acc = acc + lax.dot_general(pp, v_refs[p][...], NT, preferred_element_type=F32)
    acc = jnp.where(own, acc / l, 0.0)
    y = acc[0:s_len, :]
    for h in range(1, nh):
        y = y + acc[h * s_len:(h + 1) * s_len, :]
    o_ref[...] = y.astype(o_ref.dtype)


def _fox_sample(page_table, q, k_new, v_new, col, rown, cache_k, cache_v, lc, layer, nb, s_len):
    n_pages = page_table.shape[1]
    n_phys, page = cache_k.shape[0], cache_k.shape[1]
    depth = cache_k.shape[2]
    ck = cache_k.transpose(0, 2, 3, 4, 1).reshape(n_phys, depth, A_WIDTH, page)
    cv = cache_v.transpose(0, 2, 3, 4, 1).reshape(n_phys, depth, A_WIDTH, page)
    tok = lambda c: pl.BlockSpec((s_len, c), lambda b, pt: (b, 0))
    page_specs = [pl.BlockSpec((None, None, A_WIDTH, page),
                               lambda b, pt, j=j: (pt[b, j], layer, 0, 0)) for j in range(n_pages)]
    lc_spec = pl.BlockSpec(lc.shape, lambda b, pt: (0, 0, 0), pipeline_mode=pl.Buffered(1))
    grid_spec = pltpu.PrefetchScalarGridSpec(
        num_scalar_prefetch=1,
        grid=(nb,),
        in_specs=[tok(A_WIDTH), tok(A_WIDTH), tok(A_WIDTH), tok(LANES),
                  pl.BlockSpec((None, SM_ROWS, s_len), lambda b, pt: (b, 0, 0))]
                 + page_specs + page_specs + [lc_spec],
        out_specs=tok(A_WIDTH),
        scratch_shapes=[pltpu.VMEM((A_HEADS * s_len, n_pages * LANES), F32)],
    )
    return pl.pallas_call(
        functools.partial(_fox_sample_kernel, n_pages=n_pages, layer=layer, s_len=s_len),
        grid_spec=grid_spec,
        out_shape=jax.ShapeDtypeStruct((nb * s_len, A_WIDTH), BF16),
        compiler_params=_cparams("arbitrary"),
        name="fox_sample",
    )(page_table, q, k_new, v_new, col, rown, *([ck] * n_pages), *([cv] * n_pages), lc)


def _mlstm_kernel(q_ref, k_ref, v_ref, so_ref, col_ref, row_ref, g_ref, c0_ref, n0_ref, m0_ref, _,
                  y_ref, c1_ref, n1_ref, m1_ref, *, t, chunk, bb, lowp):
    nh, hd = M_HEADS, M_HEAD_DIM
    r = lax.broadcasted_iota(jnp.int32, (chunk, chunk), 0)
    c = lax.broadcasted_iota(jnp.int32, (chunk, chunk), 1)
    causal = c <= r
    g = g_ref[...]
    op = (lambda a: a.astype(BF16)) if lowp else (lambda a: a.astype(F32))
    heads = range(nh)
    hs = lambda h: slice(h * hd, (h + 1) * hd)

    def chunk_body(bi, ci, carry):
        cs, ns, ms = carry
        r0 = pl.multiple_of(bi * t + ci * chunk, chunk)
        l0 = pl.multiple_of(ci * chunk, chunk)
        qa = q_ref[pl.ds(r0, chunk), :]
        ka = k_ref[pl.ds(r0, chunk), :]
        va = v_ref[pl.ds(r0, chunk), :]
        colv = col_ref[pl.ds(r0, chunk), :]
        rows = row_ref[bi, :, pl.ds(l0, chunk)]
        qs = [op(qa[:, hs(h)]) for h in heads]
        ks = [op(ka[:, hs(h)]) for h in heads]
        vs = [op(va[:, hs(h)]) for h in heads]
        qk = [lax.dot_general(qs[h], ks[h], NT, preferred_element_type=F32) for h in heads]
        qc = [lax.dot_general(qs[h], op(cs[h]), NT, preferred_element_type=F32) for h in heads]
        wqk, a_, m_t_, fcol_, icol_ = [], [], [], [], []
        for h in heads:
            f_col = colv[:, SM_MF + h:SM_MF + h + 1]
            i_col = colv[:, SM_MI + h:SM_MI + h + 1]
            f_row = rows[SM_MF + h:SM_MF + h + 1, :]
            i_row = rows[SM_MI + h:SM_MI + h + 1, :]
            d = jnp.where(causal, f_col + (i_row - f_row), -jnp.inf)
            inter = ms[h] + f_col
            m_t = jnp.maximum(inter, jnp.max(d, axis=1, keepdims=True))
            wqk.append(jnp.exp(d - m_t) * qk[h])
            a_.append(jnp.exp(inter - m_t))
            m_t_.append(m_t)
            fcol_.append(f_col)
            icol_.append(i_col)
        pv = [_dot(op(wqk[h]), vs[h]) for h in heads]
        ys, cs_new, ns_new, ms_new = [], [], [], []
        for h in heads:
            a, m_t = a_[h], m_t_[h]
            qf = qa[:, hs(h)].astype(F32)
            num = a * qc[h] + pv[h]
            den = a * jnp.sum(qf * ns[h], axis=1, keepdims=True) + jnp.sum(wqk[h], axis=1, keepdims=True)
            h_t = num / jnp.maximum(jnp.abs(den), jnp.exp(-m_t))
            ys.append(h_t * lax.rsqrt(jnp.mean(h_t * h_t, axis=1, keepdims=True) + EPS))
            m_last = m_t[chunk - 1:chunk, :]
            a_last = a[chunk - 1:chunk, :]
            w_last = jnp.exp((icol_[h] - fcol_[h]) + (fcol_[h][chunk - 1:chunk, :] - m_last))
            vw = va[:, hs(h)].astype(F32) * w_last
            cs_new.append(a_last * cs[h] + _dot(op(vw.T), ks[h]))
            ns_new.append(a_last * ns[h] + jnp.sum(ka[:, hs(h)].astype(F32) * w_last, axis=0, keepdims=True))
            ms_new.append(m_last)
        hn = jnp.concatenate(ys, axis=1) * g
        y_ref[pl.ds(r0, chunk), :] = (so_ref[pl.ds(r0, chunk), :] * hn).astype(y_ref.dtype)
        return cs_new, ns_new, ms_new

    def seq_body(bi, _):
        init = ([c0_ref[bi, h] for h in heads], [n0_ref[bi, h] for h in heads], [m0_ref[bi, h] for h in heads])
        if t == chunk:
            cs, ns, ms = chunk_body(bi, 0, init)
        else:
            cs, ns, ms = lax.fori_loop(0, t // chunk, lambda ci, cr: chunk_body(bi, ci, cr), init)
        for h in heads:
            c1_ref[bi, h] = cs[h]
            n1_ref[bi, h] = ns[h]
            m1_ref[bi, h] = ms[h]
        return 0

    if bb == 1:
        seq_body(0, 0)
    else:
        lax.fori_loop(0, bb, seq_body, 0)


def _mlstm(q, k, v, so, col, rowt, g, c0, n0, m0, layer_in, c_prev, layer_out, depth, nb, t, chunk, bb):
    n = q.shape[0]
    nh, hd = M_HEADS, M_HEAD_DIM
    seq = lambda w: pl.BlockSpec((bb * t, w), lambda i: (i, 0))
    st_in = lambda a, b_: pl.BlockSpec((bb, None, nh, a, b_), lambda i: (i, layer_in, 0, 0, 0))
    st_out = lambda a, b_: pl.BlockSpec((bb, nh, a, b_), lambda i: (i, 0, 0, 0))
    if c_prev is None:
        c_prev = jnp.zeros((SUBLANES, LANES), F32)
        aliases = {}
    else:
        aliases = {10: 1}
    return pl.pallas_call(
        functools.partial(_mlstm_kernel, t=t, chunk=chunk, bb=bb, lowp=chunk >= 16),
        grid=(nb // bb,),
        in_specs=[seq(M_WIDTH), seq(M_WIDTH), seq(M_WIDTH), seq(M_WIDTH), seq(LANES),
                  pl.BlockSpec((bb, SM_ROWS, t), lambda i: (i, 0, 0)),
                  _const_spec((1, M_WIDTH)),
                  st_in(hd, hd), st_in(1, hd), st_in(1, 1),
                  pl.BlockSpec(memory_space=pl.ANY)],
        out_specs=[seq(M_WIDTH),
                   pl.BlockSpec((bb, None, nh, hd, hd), lambda i: (i, layer_out, 0, 0, 0)),
                   st_out(1, hd), st_out(1, 1)],
        out_shape=[jax.ShapeDtypeStruct((n, M_WIDTH), BF16),
                   jax.ShapeDtypeStruct((nb, depth, nh, hd, hd), F32),
                   jax.ShapeDtypeStruct((nb, nh, 1, hd), F32),
                   jax.ShapeDtypeStruct((nb, nh, 1, 1), F32)],
        input_output_aliases=aliases,
        compiler_params=_cparams("parallel"),
        name="mlstm",
    )(q, k, v, so, col, rowt, g, c0, n0.reshape(n0.shape[:3] + (1, hd)), m0.reshape(m0.shape[:3] + (1, 1)), c_prev)


C_EXT = M_HEAD_DIM + 16


def _mlstm_t_kernel(q_ref, k_ref, vt_ref, so_ref, col_ref, row_ref, g_ref, _,
                    y_ref, c1_ref, n1_ref, m1_ref, *, t, chunk):
    nh, hd = M_HEADS, M_HEAD_DIM
    s_idx = lax.broadcasted_iota(jnp.int32, (chunk, chunk), 0)
    l_idx = lax.broadcasted_iota(jnp.int32, (chunk, chunk), 1)
    causal = s_idx <= l_idx
    heads = range(nh)
    hs = lambda h: slice(h * hd, (h + 1) * hd)
    last = slice(chunk - 1, chunk)
    pad_rows = jnp.zeros((C_EXT - hd - 1, chunk), F32)

    def chunk_body(ci, carry):
        cs, ms = carry
        r0 = pl.multiple_of(ci * chunk, chunk)
        qa = q_ref[pl.ds(r0, chunk), :]
        ka = k_ref[pl.ds(r0, chunk), :]
        vta = vt_ref[:, pl.ds(r0, chunk)]
        colv = col_ref[pl.ds(r0, chunk), :]
        rows = row_ref[:, pl.ds(r0, chunk)]
        qk = [lax.dot_general(ka[:, hs(h)], qa[:, hs(h)], NT, preferred_element_type=F32) for h in heads]
        cq = [lax.dot_general(cs[h].astype(BF16), qa[:, hs(h)], NT, preferred_element_type=F32) for h in heads]
        wqk, a_, m_t_, wl_ = [], [], [], []
        for h in heads:
            f_row = rows[SM_MF + h:SM_MF + h + 1, :]
            i_row = rows[SM_MI + h:SM_MI + h + 1, :]
            g_max = rows[SM_MG + h:SM_MG + h + 1, :]
            g_col = colv[:, SM_MI + h:SM_MI + h + 1] - colv[:, SM_MF + h:SM_MF + h + 1]
            mx = jnp.maximum(ms[h], g_max)
            m_t = f_row + mx
            d = jnp.where(causal, f_row + g_col, -jnp.inf)
            wqk.append(jnp.exp(d - m_t) * qk[h])
            a_.append(jnp.exp(ms[h] - mx))
            m_t_.append(m_t)
            wl_.append(jnp.exp((i_row - f_row) + (f_row[:, last] - m_t[:, last])))
        pv = [_dot(vta[hs(h), :], wqk[h].astype(BF16)) for h in heads]
        ys, cs_new, ms_new = [], [], []
        for h in heads:
            a, m_t = a_[h], m_t_[h]
            num = a * cq[h][0:hd, :] + pv[h]
            den = a * cq[h][hd:hd + 1, :] + jnp.sum(wqk[h], axis=0, keepdims=True)
            h_t = num / jnp.maximum(jnp.abs(den), jnp.exp(-m_t))
            ys.append((h_t * lax.rsqrt(jnp.mean(h_t * h_t, axis=0, keepdims=True) + EPS)).T)
            vw = jnp.concatenate([vta[hs(h), :].astype(F32) * wl_[h], wl_[h], pad_rows], axis=0)
            cs_new.append(a[:, last] * cs[h] + _dot(vw.astype(BF16), ka[:, hs(h)]))
            ms_new.append(m_t[:, last])
        hn = jnp.concatenate(ys, axis=1) * g_ref[...]
        y_ref[pl.ds(r0, chunk), :] = (so_ref[pl.ds(r0, chunk), :] * hn).astype(y_ref.dtype)
        return cs_new, ms_new

    init = ([jnp.zeros((C_EXT, hd), F32)] * nh, [jnp.zeros((1, 1), F32)] * nh)
    cs, ms = lax.fori_loop(0, t // chunk, chunk_body, init, unroll=True)
    for h in heads:
        c1_ref[h] = cs[h][0:hd, :]
        n1_ref[h] = cs[h][hd:hd + 1, :]
        m1_ref[h] = ms[h]


def _mlstm_t(q, k, vt, so, col, rowt, g, c_prev, layer_out, depth, nb, t, chunk):
    n = q.shape[0]
    nh, hd = M_HEADS, M_HEAD_DIM
    seq = lambda w: pl.BlockSpec((t, w), lambda i: (i, 0))
    st_out = lambda a, b_: pl.BlockSpec((None, nh, a, b_), lambda i: (i, 0, 0, 0))
    if c_prev is None:
        c_prev = jnp.zeros((SUBLANES, LANES), F32)
        aliases = {}
    else:
        aliases = {7: 1}
    return pl.pallas_call(
        functools.partial(_mlstm_t_kernel, t=t, chunk=chunk),
        grid=(nb,),
        in_specs=[seq(M_WIDTH), seq(M_WIDTH), pl.BlockSpec((None, M_WIDTH, t), lambda i: (i, 0, 0)),
                  seq(M_WIDTH), seq(LANES), pl.BlockSpec((None, SM_ROWS, t), lambda i: (i, 0, 0)),
                  _const_spec((1, M_WIDTH)), pl.BlockSpec(memory_space=pl.ANY)],
        out_specs=[seq(M_WIDTH),
                   pl.BlockSpec((None, None, nh, hd, hd), lambda i: (i, layer_out, 0, 0, 0)),
                   st_out(1, hd), st_out(1, 1)],
        out_shape=[jax.ShapeDtypeStruct((n, M_WIDTH), BF16),
                   jax.ShapeDtypeStruct((nb, depth, nh, hd, hd), F32),
                   jax.ShapeDtypeStruct((nb, nh, 1, hd), F32),
                   jax.ShapeDtypeStruct((nb, nh, 1, 1), F32)],
        input_output_aliases=aliases,
        compiler_params=_cparams("parallel"),
        name="mlstm_t",
    )(q, k, vt, so, col, rowt, g, c_prev)


def _conv_carry_init(state_ref, carry_ref, tiles_per_seq):
    @pl.when((pl.program_id(0) % tiles_per_seq) == 0)
    def _():
        carry_ref[...] = state_ref[...]


def _conv_long(u, w_ref, c0, carry_ref):
    tm, ch = u.shape
    prev = carry_ref[:, c0:c0 + ch]
    row = lax.broadcasted_iota(jnp.int32, (tm, 1), 0)
    p1 = jnp.where(row == 0, prev[1:2, :], pltpu.roll(u, 1, axis=0))
    p2 = jnp.where(row == 0, prev[0:1, :], jnp.where(row == 1, prev[1:2, :], pltpu.roll(u, 2, axis=0)))
    carry_ref[:, c0:c0 + ch] = u[tm - 2:tm, :]
    return p2 * w_ref[0:1, c0:c0 + ch] + p1 * w_ref[1:2, c0:c0 + ch] + u * w_ref[2:3, c0:c0 + ch]


def _conv_short(u, w_ref, c0, fill_ref, s_len):
    tm, ch = u.shape
    fill = fill_ref[:, c0:c0 + ch]
    rmod = lax.broadcasted_iota(jnp.int32, (tm, 1), 0) & (s_len - 1)
    p1 = jnp.where(rmod >= 1, pltpu.roll(u, 1, axis=0), pltpu.roll(fill, tm - 1, axis=0))
    p2 = jnp.where(rmod >= 2, pltpu.roll(u, 2, axis=0), fill)
    return p2 * w_ref[0:1, c0:c0 + ch] + p1 * w_ref[1:2, c0:c0 + ch] + u * w_ref[2:3, c0:c0 + ch]


def _merge_kernel(x_ref, g_ref, wc_ref, bc_ref, wg_ref, bg_ref, ws_ref, ya_ref, ym_ref,
                  woa_ref, wom_ref, woc_ref, wo_ref, st_ref, x1_ref, cst_ref, *scratch,
                  long_seq, tiles_per_seq, s_len):
    x = x_ref[...]
    h = _rms(x, g_ref[...]).astype(BF16)
    cw = C_WIDTH
    cb = _dot(h, wc_ref[:, 0:cw]) + bc_ref[:, 0:cw]
    u = (_dot(h, wc_ref[:, cw:2 * cw]) + bc_ref[:, cw:2 * cw]) * (_dot(h, wc_ref[:, 2 * cw:3 * cw]) + bc_ref[:, 2 * cw:3 * cw])
    if long_seq:
        _conv_carry_init(st_ref, scratch[0], tiles_per_seq)
        uc = _conv_long(u, ws_ref, 0, scratch[0])
        cst_ref[...] = u[u.shape[0] - 2:, :]
    else:
        uc = _conv_short(u, ws_ref, 0, st_ref, s_len)
        cst_ref[...] = u
    yc = (cb * uc).astype(BF16)
    d = D_MODEL

    def gate(i):
        return _sigmoid(_dot(h, wg_ref[:, i * d:(i + 1) * d]) + bg_ref[:, i * d:(i + 1) * d])

    merged = gate(0) * _dot(ya_ref[...], woa_ref[...])
    merged = merged + gate(1) * _dot(ym_ref[...], wom_ref[...])
    merged = merged + gate(2) * _dot(yc, woc_ref[...])
    x1_ref[...] = x + _dot(merged.astype(BF16), wo_ref[...])


def _merge(x, g, wc, bc, wg, bg, ws, ya, ym, woa, wom, woc, wo, st, nb, t, tm):
    n = x.shape[0]
    long_seq = t >= tm
    tiles_per_seq = max(t // tm, 1)
    row = lambda c: pl.BlockSpec((tm, c), lambda i: (i, 0))
    if long_seq:
        st_spec = pl.BlockSpec((None, 2, C_WIDTH), lambda i: (i // tiles_per_seq, 0, 0))
        cst_spec = pl.BlockSpec((None, 2, C_WIDTH), lambda i: (i // tiles_per_seq, 0, 0))
        cst_shape = jax.ShapeDtypeStruct((nb, 2, C_WIDTH), F32)
        scratch = [pltpu.VMEM((2, C_WIDTH), F32)]
    else:
        st_spec = row(C_WIDTH)
        cst_spec = row(C_WIDTH)
        cst_shape = jax.ShapeDtypeStruct((n, C_WIDTH), F32)
        scratch = []
    return pl.pallas_call(
        functools.partial(_merge_kernel, long_seq=long_seq, tiles_per_seq=tiles_per_seq, s_len=t),
        grid=(n // tm,),
        in_specs=[row(D_MODEL), _const_spec((1, D_MODEL)),
                  _const_spec(wc.shape), _const_spec(bc.shape), _const_spec(wg.shape), _const_spec(bg.shape),
                  _const_spec(ws.shape), row(A_WIDTH), row(M_WIDTH),
                  _const_spec(woa.shape), _const_spec(wom.shape), _const_spec(woc.shape), _const_spec(wo.shape),
                  st_spec],
        out_specs=[row(D_MODEL), cst_spec],
        out_shape=[jax.ShapeDtypeStruct((n, D_MODEL), F32), cst_shape],
        scratch_shapes=scratch,
        compiler_params=_cparams("arbitrary"),
        name="merge",
    )(x, g, wc, bc, wg, bg, ws, ya, ym, woa, wom, woc, wo, st)


FF_CHUNK = D_FF // 2


def _ffn_kernel(x_ref, pe_ref, gf_ref, wup_ref, wfc_ref, bfc_ref, wdn_ref, gp_ref, wpg_ref, wpp_ref, gfin_ref,
                st_ref, xo_ref, fst_ref, *scratch, long_seq, tiles_per_seq, s_len, final):
    x = x_ref[...]
    tm = x.shape[0]
    h2 = _rms(x, gf_ref[...]).astype(BF16)
    acc = jnp.zeros((tm, D_MODEL), F32)
    if long_seq:
        _conv_carry_init(st_ref, scratch[0], tiles_per_seq)
    for half in range(D_FF // FF_CHUNK):
        c0 = half * FF_CHUNK
        ua = _dot(h2, wup_ref[:, c0:c0 + FF_CHUNK])
        ub = _dot(h2, wup_ref[:, D_FF + c0:D_FF + c0 + FF_CHUNK])
        if long_seq:
            uac = _conv_long(ua, wfc_ref, c0, scratch[0])
            fst_ref[:, c0:c0 + FF_CHUNK] = ua[tm - 2:, :]
        else:
            uac = _conv_short(ua, wfc_ref, c0, st_ref, s_len)
            fst_ref[:, c0:c0 + FF_CHUNK] = ua
        z = uac + bfc_ref[:, c0:c0 + FF_CHUNK]
        act = (z * _sigmoid(z) * ub).astype(BF16)
        acc = acc + _dot(act, wdn_ref[c0:c0 + FF_CHUNK, :])
    x2 = x + acc
    gate = _sigmoid(_dot(_rms(x2, gp_ref[...]).astype(BF16), wpg_ref[...]))
    x3 = x2 + gate * _dot(pe_ref[...].astype(BF16), wpp_ref[...])
    if final:
        x3 = _rms(x3, gfin_ref[...])
    xo_ref[...] = x3


def _ffn(x, pe, gf, wup, wfc, bfc, wdn, gp, wpg, wpp, gfin, st, nb, t, tm, final):
    n = x.shape[0]
    long_seq = t >= tm
    tiles_per_seq = max(t // tm, 1)
    row = lambda c: pl.BlockSpec((tm, c), lambda i: (i, 0))
    if long_seq:
        st_spec = pl.BlockSpec((None, 2, D_FF), lambda i: (i // tiles_per_seq, 0, 0))
        fst_spec = pl.BlockSpec((None, 2, D_FF), lambda i: (i // tiles_per_seq, 0, 0))
        fst_shape = jax.ShapeDtypeStruct((nb, 2, D_FF), F32)
        scratch = [pltpu.VMEM((2, D_FF), F32)]
    else:
        st_spec = row(D_FF)
        fst_spec = row(D_FF)
        fst_shape = jax.ShapeDtypeStruct((n, D_FF), F32)
        scratch = []
    return pl.pallas_call(
        functools.partial(_ffn_kernel, long_seq=long_seq, tiles_per_seq=tiles_per_seq, s_len=t, final=final),
        grid=(n // tm,),
        in_specs=[row(D_MODEL), row(pe.shape[1]), _const_spec((1, D_MODEL)),
                  _const_spec(wup.shape), _const_spec(wfc.shape), _const_spec(bfc.shape), _const_spec(wdn.shape),
                  _const_spec((1, D_MODEL)), _const_spec(wpg.shape), _const_spec(wpp.shape),
                  _const_spec((1, D_MODEL)), st_spec],
        out_specs=[row(D_MODEL), fst_spec],
        out_shape=[jax.ShapeDtypeStruct((n, D_MODEL), F32), fst_shape],
        scratch_shapes=scratch,
        compiler_params=_cparams("arbitrary"),
        name="ffn",
    )(x, pe, gf, wup, wfc, bfc, wdn, gp, wpg, wpp, gfin, st)


def _row_form(col, nb, t):
    return col[:, :SM_ROWS].reshape(nb, t, SM_ROWS).transpose(0, 2, 1)


def _short_fill(state, s_len):
    nb, _, ch = state.shape
    return jnp.pad(state, ((0, 0), (0, s_len - 2), (0, 0))).reshape(nb * s_len, ch)


def _pick(n, pref):
    return pref if n % pref == 0 else n


def _layer(x, pe, w, nb, t, chunk, past, final, carry):
    n = x.shape[0]
    layer, depth = carry["layer"], carry["depth"]
    tm_in = _pick(n, 512)
    tm_merge = _pick(n, 512)
    tm_ffn = _pick(n, 256)
    if past is None:
        kvp = dict(layer=layer, depth=depth, prev=carry["kvt"], nb=nb, t=t, wt=w["w_kvt"], bt=w["b_kvt"])
        q, k, mq, mk, so, sm, kt, vt, mvt = _inproj(x, w["g_mix"], w["w_am"], w["b_am"], tm_in, kvp)
        col, logf = _gates(sm, t, t, chunk)
        rowt = _row_form(col, nb, t)
        ya = _fox_prompt(q, k, vt, col, layer, nb, t, _pick(t, 512))
        k, v = kt, vt
        ym, c1, n1, m1 = _mlstm_t(mq, mk, mvt, so, col, rowt, w["g_mhead"], carry["c"], layer, depth, nb, t, chunk)
        conv0 = jnp.zeros((nb, 2, C_WIDTH), F32)
        ffn0 = jnp.zeros((nb, 2, D_FF), F32)
    else:
        q, k, v, mq, mk, mv, so, sm = _inproj(x, w["g_mix"], w["w_am"], w["b_am"], tm_in)
        col, logf = _gates(sm, n, t, t)
        rowt = _row_form(col, nb, t)
        ya = _fox_sample(past["page_table"], q, k, v, col, rowt, past["cache_k"], past["cache_v"],
                         past["lc"], layer, nb, t)
        conv0, ffn0 = past["conv0"], past["ffn0"]
        ym, c1, n1, m1 = _mlstm(mq, mk, mv, so, col, rowt, w["g_mhead"], past["c0"], past["n0"], past["m0"], layer,
                                carry["c"], layer, depth, nb, t, chunk, _pick(nb, SUBLANES))
    long_merge = t >= tm_merge
    long_ffn = t >= tm_ffn
    x1, conv1 = _merge(x, w["g_mix"], w["w_c"], w["b_c"], w["w_g"], w["b_g"], w["w_sconv"], ya, ym,
                       w["w_oa"], w["w_om"], w["w_oc"], w["w_o"],
                       conv0 if long_merge else _short_fill(conv0, t), nb, t, tm_merge)
    x3, ffn1 = _ffn(x1, pe, w["g_ffn"], w["w_up"], w["w_fconv"], w["b_fconv"], w["w_down"],
                    w["g_ple"], w["w_ple_gate"], w["w_ple"], w["g_final"],
                    ffn0 if long_ffn else _short_fill(ffn0, t), nb, t, tm_ffn, final)
    if not long_merge:
        conv1 = conv1.reshape(nb, t, C_WIDTH)[:, t - 2:, :]
    if not long_ffn:
        ffn1 = ffn1.reshape(nb, t, D_FF)[:, t - 2:, :]
    if past is not None:
        k = k.reshape(nb, t, A_HEADS, A_HEAD_DIM)
        v = v.reshape(nb, t, A_HEADS, A_HEAD_DIM)
    state = (k, v, logf.reshape(nb, t, A_HEADS), c1, n1.reshape(nb, M_HEADS, M_HEAD_DIM),
             m1.reshape(nb, M_HEADS), conv1, ffn1)
    return x3, state


def _layer_weights(l, w_in, b_in, g_mix, g_mhead, w_sconv, w_oa, w_om, w_oc, w_o, g_ffn, w_up,
                   w_fconv, b_fconv, w_down, g_ple, w_ple_gate, w_ple, g_final):
    a0 = 0
    af0 = 3 * A_WIDTH
    m0 = af0 + A_HEADS
    mi0 = m0 + 3 * M_WIDTH
    mo0 = mi0 + 2 * M_HEADS
    c0 = mo0 + M_WIDTH
    g0 = c0 + 3 * C_WIDTH
    wl, bl = w_in[l], b_in[l]
    pad = LANES - SM_USED

    def cols(a, lo, hi):
        return a[..., lo:hi]

    def am(a):
        return jnp.concatenate(
            [cols(a, a0, af0), cols(a, m0, mi0), cols(a, mo0, c0),
             cols(a, af0, m0), cols(a, mi0, mo0), jnp.zeros(a.shape[:-1] + (pad,), a.dtype)], axis=-1)

    row = lambda a: a.reshape(1, -1)
    return {
        "w_am": am(wl).astype(BF16), "b_am": row(am(bl)),
        "w_kvt": jnp.concatenate([cols(wl, A_WIDTH, af0), cols(wl, m0 + 2 * M_WIDTH, mi0)], axis=-1).T.astype(BF16),
        "b_kvt": jnp.concatenate([cols(bl, A_WIDTH, af0), cols(bl, m0 + 2 * M_WIDTH, mi0)], axis=-1).reshape(-1, 1),
        "w_c": cols(wl, c0, g0).astype(BF16), "b_c": row(cols(bl, c0, g0)),
        "w_g": cols(wl, g0, g0 + 3 * D_MODEL).astype(BF16), "b_g": row(cols(bl, g0, g0 + 3 * D_MODEL)),
        "g_mix": row(g_mix[l]), "g_mhead": row(g_mhead[l]), "w_sconv": w_sconv[l],
        "w_oa": w_oa[l].astype(BF16), "w_om": w_om[l].astype(BF16), "w_oc": w_oc[l].astype(BF16),
        "w_o": w_o[l].astype(BF16), "g_ffn": row(g_ffn[l]), "w_up": w_up[l].astype(BF16),
        "w_fconv": w_fconv[l], "b_fconv": row(b_fconv[l]), "w_down": w_down[l].astype(BF16),
        "g_ple": row(g_ple[l]), "w_ple_gate": w_ple_gate[l].astype(BF16), "w_ple": w_ple[l].astype(BF16),
        "g_final": row(g_final),
    }


def kernel(x_prompt, x_sample, cache_k, cache_v, cache_logf, state_mlstm_C, state_mlstm_n, state_mlstm_m, state_conv, state_ffn_conv, page_table, p_prompt, p_sample, w_in, b_in, g_mix, g_mhead, w_sconv, w_oa, w_om, w_oc, w_o, g_ffn, w_up, w_fconv, b_fconv, w_down, g_ple, w_ple_gate, w_ple, g_final):
    nbp, tp, _ = x_prompt.shape
    nbs, ts, _ = x_sample.shape
    depth = w_in.shape[0]
    n_phys, page = cache_logf.shape[0], cache_logf.shape[1]
    assert page == LANES and ts == SUBLANES and depth * A_HEADS == SM_USED

    lf_t = cache_logf.reshape(n_phys, page, depth * A_HEADS).transpose(0, 2, 1).reshape(-1, LANES)
    lc = _pool_scan(lf_t, _pick(lf_t.shape[0], 2048)).reshape(n_phys, depth * A_HEADS, LANES)

    xp = x_prompt.reshape(nbp * tp, D_MODEL)
    xs = x_sample.reshape(nbs * ts, D_MODEL)
    chunk_p = _pick(tp, 256)
    new_p, new_s = [], []
    kvt = c_p = c_s = None
    for l in range(depth):
        w = _layer_weights(l, w_in, b_in, g_mix, g_mhead, w_sconv, w_oa, w_om, w_oc, w_o, g_ffn, w_up,
                           w_fconv, b_fconv, w_down, g_ple, w_ple_gate, w_ple, g_final)
        final = l == depth - 1
        xp, st_p = _layer(xp, p_prompt[l].reshape(nbp * tp, -1), w, nbp, tp, chunk_p, None, final,
                          {"layer": l, "depth": depth, "kvt": kvt, "c": c_p})
        kvt, c_p = st_p[:2], st_p[3]
        past = {"page_table": page_table, "cache_k": cache_k, "cache_v": cache_v, "lc": lc,
                "c0": state_mlstm_C, "n0": state_mlstm_n, "m0": state_mlstm_m,
                "conv0": state_conv[:, l], "ffn0": state_ffn_conv[:, l]}
        xs, st_s = _layer(xs, p_sample[l].reshape(nbs * ts, -1), w, nbs, ts, ts, past, final,
                          {"layer": l, "depth": depth, "kvt": None, "c": c_s})
        c_s = st_s[3]
        new_p.append(st_p)
        new_s.append(st_s)

    def stack(per_layer, c_all, first):
        rows = [jnp.stack([s[i] for s in per_layer], axis=2) for i in range(first, 3)]
        states = [jnp.stack([s[i] for s in per_layer], axis=1) for i in range(4, 8)]
        return tuple(rows + [c_all] + states)

    kv_p = tuple(a.reshape(nbp, depth, A_HEADS, A_HEAD_DIM, tp).transpose(0, 4, 1, 2, 3) for a in kvt)
    return ((xp.reshape(nbp, tp, D_MODEL), xs.reshape(nbs, ts, D_MODEL)) + kv_p + stack(new_p, c_p, 2)
            + stack(new_s, c_s, 0))
```

```python
import functools

import jax
import jax.numpy as jnp
from jax import lax
from jax.experimental import pallas as pl
from jax.experimental.pallas import tpu as pltpu

F32 = jnp.float32
BF16 = jnp.bfloat16

D_MODEL = 1024
A_HEADS = 8
A_HEAD_DIM = 64
A_WIDTH = A_HEADS * A_HEAD_DIM
M_HEADS = 4
M_HEAD_DIM = 128
M_WIDTH = M_HEADS * M_HEAD_DIM
C_WIDTH = 512
D_FF = 2816
EPS = 1e-6
LANES = 128
SUBLANES = 8
VMEM_LIMIT = 56 * 1024 * 1024

SM_AF = 0
SM_MI = 8
SM_MF = 12
SM_USED = 16
SM_MG = 16
SM_ROWS = 24

NT = (((1,), (1,)), ((), ()))
LOG2E = 1.4426950408889634


def _cparams(*sem):
    return pltpu.CompilerParams(dimension_semantics=sem, vmem_limit_bytes=VMEM_LIMIT)


def _const_spec(shape):
    nd = len(shape)
    return pl.BlockSpec(shape, lambda *_: (0,) * nd, pipeline_mode=pl.Buffered(1))


def _rms(x, g):
    return x * lax.rsqrt(jnp.mean(x * x, axis=-1, keepdims=True) + EPS) * g


def _log_sigmoid(x):
    return jnp.minimum(x, 0.0) - jnp.log1p(jnp.exp(-jnp.abs(x)))


def _sigmoid(x):
    return 1.0 / (1.0 + jnp.exp(-x))


def _dot(a, b):
    return jnp.dot(a, b, preferred_element_type=F32)


def _inproj_kernel(*refs, transposed_kv):
    if transposed_kv:
        (x_ref, g_ref, w_ref, b_ref, wt_ref, bt_ref, _, _,
         q_ref, k_ref, mq_ref, mk_ref, so_ref, sm_ref, kt_ref, vt_ref, mvt_ref) = refs
    else:
        x_ref, g_ref, w_ref, b_ref, q_ref, k_ref, v_ref, mq_ref, mk_ref, mv_ref, so_ref, sm_ref = refs
    h = _rms(x_ref[...], g_ref[...]).astype(BF16)

    def proj(c0, n):
        return _dot(h, w_ref[:, c0:c0 + n]) + b_ref[:, c0:c0 + n]

    q_scale = A_HEAD_DIM ** -0.5 * (LOG2E if transposed_kv else 1.0)
    q_ref[...] = (proj(0, A_WIDTH) * q_scale).astype(BF16)
    if transposed_kv:
        k_ref[...] = proj(A_WIDTH, A_WIDTH).astype(BF16)
        kt_ref[...] = (lax.dot_general(wt_ref[0:A_WIDTH, :], h, NT, preferred_element_type=F32)
                       + bt_ref[0:A_WIDTH, :])
        vt_ref[...] = (lax.dot_general(wt_ref[A_WIDTH:2 * A_WIDTH, :], h, NT, preferred_element_type=F32)
                       + bt_ref[A_WIDTH:2 * A_WIDTH, :])
        mvt_ref[...] = (lax.dot_general(wt_ref[2 * A_WIDTH:, :], h, NT, preferred_element_type=F32)
                        + bt_ref[2 * A_WIDTH:, :]).astype(BF16)
    else:
        k_ref[...] = proj(A_WIDTH, A_WIDTH)
        v_ref[...] = proj(2 * A_WIDTH, A_WIDTH)
    o = 3 * A_WIDTH
    mq_ref[...] = proj(o, M_WIDTH).astype(BF16)
    mk_ref[...] = (proj(o + M_WIDTH, M_WIDTH) * (M_HEAD_DIM ** -0.5)).astype(BF16)
    if not transposed_kv:
        mv_ref[...] = proj(o + 2 * M_WIDTH, M_WIDTH).astype(BF16)
    so_ref[...] = _sigmoid(proj(o + 3 * M_WIDTH, M_WIDTH))
    sm_ref[...] = proj(o + 4 * M_WIDTH, LANES)


def _inproj(x, g, w, b, tm, kv=None):
    n = x.shape[0]
    wcols = w.shape[1]
    row = lambda c: pl.BlockSpec((tm, c), lambda i: (i, 0))
    common = [
        jax.ShapeDtypeStruct((n, M_WIDTH), BF16),
        jax.ShapeDtypeStruct((n, M_WIDTH), BF16),
        jax.ShapeDtypeStruct((n, M_WIDTH), BF16),
        jax.ShapeDtypeStruct((n, M_WIDTH), F32),
        jax.ShapeDtypeStruct((n, LANES), F32),
    ]
    common_specs = [row(M_WIDTH)] * 4 + [row(LANES)]
    in_specs = [row(D_MODEL), _const_spec((1, D_MODEL)), _const_spec((D_MODEL, wcols)), _const_spec((1, wcols))]
    if kv is None:
        return pl.pallas_call(
            functools.partial(_inproj_kernel, transposed_kv=False),
            grid=(n // tm,),
            in_specs=in_specs,
            out_specs=[row(A_WIDTH)] * 3 + common_specs,
            out_shape=[jax.ShapeDtypeStruct((n, A_WIDTH), BF16), jax.ShapeDtypeStruct((n, A_WIDTH), F32),
                       jax.ShapeDtypeStruct((n, A_WIDTH), F32)] + common,
            compiler_params=_cparams("parallel"),
            name="inproj",
        )(x, g, w, b)
    nb, t, depth, layer = kv["nb"], kv["t"], kv["depth"], kv["layer"]
    tps = t // tm
    t_shape = jax.ShapeDtypeStruct((nb, depth, A_WIDTH, t), F32)
    t_spec = pl.BlockSpec((None, None, A_WIDTH, tm), lambda i: (i // tps, layer, 0, i % tps))
    prev = kv["prev"]
    if prev is None:
        prev = (jnp.zeros((SUBLANES, LANES), F32),) * 2
        aliases = {}
    else:
        aliases = {6: 6, 7: 7}
    any_spec = pl.BlockSpec(memory_space=pl.ANY)
    no_mv = lambda items: items[:2] + items[3:]
    return pl.pallas_call(
        functools.partial(_inproj_kernel, transposed_kv=True),
        grid=(n // tm,),
        in_specs=in_specs + [_const_spec(kv["wt"].shape), _const_spec(kv["bt"].shape), any_spec, any_spec],
        out_specs=[row(A_WIDTH)] * 2 + no_mv(common_specs) + [t_spec, t_spec,
                   pl.BlockSpec((None, M_WIDTH, tm), lambda i: (i // tps, 0, i % tps))],
        out_shape=[jax.ShapeDtypeStruct((n, A_WIDTH), BF16), jax.ShapeDtypeStruct((n, A_WIDTH), BF16)]
                  + no_mv(common) + [t_shape, t_shape, jax.ShapeDtypeStruct((nb, M_WIDTH, t), BF16)],
        input_output_aliases=aliases,
        compiler_params=_cparams("parallel"),
        name="inproj_t",
    )(x, g, w, b, kv["wt"], kv["bt"], *prev)


def _gates_kernel(sm_ref, col_ref, logf_ref, *, seq_a, seg_m):
    x = sm_ref[...]
    rows = x.shape[0]
    lane = lax.broadcasted_iota(jnp.int32, (1, LANES), 1)
    row = lax.broadcasted_iota(jnp.int32, (rows, 1), 0)
    ls = _log_sigmoid(x)
    logf_ref[...] = ls[:, SM_AF:SM_AF + A_HEADS]
    is_a = lane < SM_MI
    is_i = (lane >= SM_MI) & (lane < SM_MF)
    is_f = (lane >= SM_MF) & (lane < SM_USED)
    y = jnp.where(is_i, x, ls)
    rmod_a = row & (seq_a - 1)
    rmod_m = row & (seg_m - 1)
    s = 1
    while s < max(seq_a, seg_m):
        take = jnp.zeros((rows, LANES), jnp.bool_)
        if s < seq_a:
            take = take | (is_a & (rmod_a >= s))
        if s < seg_m:
            take = take | (is_f & (rmod_m >= s))
        y = y + jnp.where(take, pltpu.roll(y, s, axis=0), 0.0)
        s *= 2
    nm = SM_MF - SM_MI
    is_g = (lane >= SM_MG) & (lane < SM_MG + nm)
    g = pltpu.roll(y, SM_MG - SM_MI, axis=1) - pltpu.roll(y, SM_MG - SM_MF, axis=1)
    s = 1
    while s < seg_m:
        g = jnp.where(is_g & (rmod_m >= s), jnp.maximum(g, pltpu.roll(g, s, axis=0)), g)
        s *= 2
    col_ref[...] = jnp.where(is_g, g, y)


def _gates(sm, rows, seq_a, seg_m):
    n = sm.shape[0]
    return pl.pallas_call(
        functools.partial(_gates_kernel, seq_a=seq_a, seg_m=seg_m),
        grid=(n // rows,),
        in_specs=[pl.BlockSpec((rows, LANES), lambda i: (i, 0))],
        out_specs=[pl.BlockSpec((rows, LANES), lambda i: (i, 0)),
                   pl.BlockSpec((rows, A_HEADS), lambda i: (i, 0))],
        out_shape=[jax.ShapeDtypeStruct((n, LANES), F32),
                   jax.ShapeDtypeStruct((n, A_HEADS), F32)],
        compiler_params=_cparams("parallel"),
        name="gates",
    )(sm)


def _fox_kernel(q_ref, k_ref, vt_ref, col_ref, o_ref, vb_ref, ka_ref, sa_ref, sb_ref, pa_ref, pb_ref, *, tq, t):
    hp = pl.program_id(1)
    lane = lax.broadcasted_iota(jnp.int32, (1, LANES), 1)
    hd = A_HEAD_DIM
    aug0 = [hd * (1 - hh) for hh in range(2)]

    vb_ref[...] = vt_ref[...].astype(BF16)
    colv = col_ref[...]
    kf = k_ref[...]
    for hh in range(2):
        ck = jnp.sum(jnp.where(lane == SM_AF + hp * 2 + hh, colv, 0.0), axis=1, keepdims=True) * (-LOG2E)
        c1 = ck.astype(BF16)
        r1 = ck - c1.astype(F32)
        c2 = r1.astype(BF16)
        c3 = (r1 - c2.astype(F32)).astype(BF16)
        a0 = aug0[hh]
        ka_ref[hh] = jnp.where(lane == a0, c1, jnp.where(lane == a0 + 1, c2, jnp.where(lane == a0 + 2, c3, kf)))

    slots = ((sa_ref, pa_ref), (sb_ref, pb_ref))
    kpos = lax.broadcasted_iota(jnp.int32, (tq, tq), 0)
    qpos = lax.broadcasted_iota(jnp.int32, (tq, tq), 1)

    def values(j, p_in):
        return [_dot(vb_ref[hh * hd:(hh + 1) * hd, j * tq:(j + 1) * tq], p_in[hh]) for hh in range(2)]

    for qi in range(t // tq):
        q2 = q_ref[qi * tq:(qi + 1) * tq, :]
        zero = jnp.zeros_like(q2)
        one = jnp.ones_like(q2)
        qm = [jnp.where((lane >= aug0[hh]) & (lane < aug0[hh] + 3), one,
                        jnp.where((lane >= hd) == bool(hh), q2, zero)) for hh in range(2)]

        def scores(j, s_out):
            for hh in range(2):
                s_out[hh] = lax.dot_general(ka_ref[hh, j * tq:(j + 1) * tq, :], qm[hh], NT,
                                            preferred_element_type=F32)

        def softmax(s_in, p_out, ms, ls, diag):
            m_out, l_out, alphas = [], [], []
            for hh in range(2):
                s = s_in[hh]
                if diag:
                    s = jnp.where(kpos <= qpos, s, -jnp.inf)
                m_new = jnp.max(s, axis=0, keepdims=True)
                if ms is not None:
                    m_new = jnp.maximum(ms[hh], m_new)
                p = jnp.exp2(s - m_new)
                l_new = jnp.sum(p, axis=0, keepdims=True)
                if ms is not None:
                    alpha = jnp.exp2(ms[hh] - m_new)
                    l_new = alpha * ls[hh] + l_new
                    alphas.append(alpha)
                l_out.append(l_new)
                m_out.append(m_new)
                p_out[hh] = p.astype(BF16)
            return m_out, l_out, alphas

        scores(0, slots[qi % 2][0])
        ms = ls = a_prev = accs = None
        for j in range(qi):
            (s_cur, p_cur), (s_nxt, p_nxt) = slots[(qi - j) % 2], slots[(qi - j - 1) % 2]
            pv = values(j - 1, p_nxt) if j > 0 else None
            scores(j + 1, s_nxt)
            ms, ls, alphas = softmax(s_cur, p_cur, ms, ls, False)
            if j == 1:
                accs = pv
            elif j > 1:
                accs = [a_prev[hh] * accs[hh] + pv[hh] for hh in range(2)]
            a_prev = alphas
        pv = values(qi - 1, pb_ref) if qi > 0 else None
        ms, ls, alphas = softmax(sa_ref, pa_ref, ms, ls, True)
        pd = values(qi, pa_ref)
        if qi == 0:
            yt = [pd[hh] / ls[hh] for hh in range(2)]
        elif qi == 1:
            yt = [(alphas[hh] * pv[hh] + pd[hh]) / ls[hh] for hh in range(2)]
        else:
            yt = [(alphas[hh] * (a_prev[hh] * accs[hh] + pv[hh]) + pd[hh]) / ls[hh] for hh in range(2)]
        o_ref[qi * tq:(qi + 1) * tq, :] = jnp.concatenate(yt, axis=0).T.astype(o_ref.dtype)


def _fox_prompt(q, k, vt, col, layer, nb, t, tq):
    n = q.shape[0]
    nq = t // tq
    pairs = A_WIDTH // LANES
    seq = pl.BlockSpec((t, LANES), lambda b, hp: (b, hp))
    return pl.pallas_call(
        functools.partial(_fox_kernel, tq=tq, t=t),
        grid=(nb, pairs),
        in_specs=[seq, seq,
                  pl.BlockSpec((None, None, LANES, t), lambda b, hp: (b, layer, hp, 0)),
                  pl.BlockSpec((t, LANES), lambda b, hp: (b, 0))],
        out_specs=seq,
        out_shape=jax.ShapeDtypeStruct((n, A_WIDTH), BF16),
        scratch_shapes=[pltpu.VMEM((LANES, t), BF16), pltpu.VMEM((2, t, LANES), BF16),
                        pltpu.VMEM((2, tq, tq), F32), pltpu.VMEM((2, tq, tq), F32),
                        pltpu.VMEM((2, tq, tq), BF16), pltpu.VMEM((2, tq, tq), BF16)],
        compiler_params=_cparams("parallel", "parallel"),
        name="fox_prompt",
    )(q, k, vt, col)


def _pool_scan_kernel(x_ref, o_ref):
    y = x_ref[...]
    lane = lax.broadcasted_iota(jnp.int32, (1, LANES), 1)
    s = 1
    while s < LANES:
        y = y + jnp.where(lane >= s, pltpu.roll(y, s, axis=1), 0.0)
        s *= 2
    o_ref[...] = y


def _pool_scan(x, rows):
    n = x.shape[0]
    return pl.pallas_call(
        _pool_scan_kernel,
        grid=(n // rows,),
        in_specs=[pl.BlockSpec((rows, LANES), lambda i: (i, 0))],
        out_specs=pl.BlockSpec((rows, LANES), lambda i: (i, 0)),
        out_shape=jax.ShapeDtypeStruct((n, LANES), F32),
        compiler_params=_cparams("parallel"),
        name="pool_scan",
    )(x)


def _fox_sample_kernel(pt_ref, q_ref, kn_ref, vn_ref, col_ref, rown_ref, *rest,
                       n_pages, layer, s_len):
    k_refs = rest[:n_pages]
    v_refs = rest[n_pages:2 * n_pages]
    lc_ref = rest[2 * n_pages]
    o_ref = rest[2 * n_pages + 1]
    s_ref = rest[2 * n_pages + 2]
    b = pl.program_id(0)
    lc_refs = [lc_ref.at[pt_ref[b, p]] for p in range(n_pages)]
    nh = A_HEADS
    rows = nh * s_len
    lane_w = lax.broadcasted_iota(jnp.int32, (1, A_WIDTH), 1)
    rid = lax.broadcasted_iota(jnp.int32, (rows, 1), 0)
    own = jnp.right_shift(lane_w, 6) == jnp.right_shift(rid, 3)
    q = q_ref[...].astype(F32)
    qbd = jnp.where(own, jnp.concatenate([q] * nh, axis=0), 0.0)
    colv = col_ref[...]
    rown = rown_ref[...]
    lo = layer * nh

    tot = [lc_refs[p][lo:lo + nh, LANES - 1:LANES] for p in range(n_pages)]
    base = [None] * n_pages
    suf = jnp.zeros((nh, 1), F32)
    for p in range(n_pages - 1, -1, -1):
        suf = suf + tot[p]
        base[p] = suf

    m_run = [None] * nh
    for p in range(n_pages):
        sp = _dot(qbd, k_refs[p][...])
        for h in range(nh):
            sl = slice(h * s_len, (h + 1) * s_len)
            cn = colv[:, SM_AF + h:SM_AF + h + 1]
            bias = (base[p][h:h + 1, :] - lc_refs[p][lo + h:lo + h + 1, :]) + cn
            sh = sp[sl, :] + bias
            s_ref[sl, p * LANES:(p + 1) * LANES] = sh
            mx = jnp.max(sh, axis=1, keepdims=True)
            m_run[h] = mx if m_run[h] is None else jnp.maximum(m_run[h], mx)
    sn = lax.dot_general(qbd, kn_ref[...], NT, preferred_element_type=F32)
    r = lax.broadcasted_iota(jnp.int32, (s_len, s_len), 0)
    c = lax.broadcasted_iota(jnp.int32, (s_len, s_len), 1)
    pn, m_all = [], []
    for h in range(nh):
        sl = slice(h * s_len, (h + 1) * s_len)
        cn = colv[:, SM_AF + h:SM_AF + h + 1]
        sh = sn[sl, :] + (cn - rown[SM_AF + h:SM_AF + h + 1, :])
        sh = jnp.where(c <= r, sh, -jnp.inf)
        pn.append(sh)
        m_all.append(jnp.maximum(m_run[h], jnp.max(sh, axis=1, keepdims=True)))
    m = jnp.concatenate(m_all, axis=0)
    p_new = jnp.exp(jnp.concatenate(pn, axis=0) - m)
    l = jnp.sum(p_new, axis=1, keepdims=True)
    acc = _dot(p_new, vn_ref[...])
    for p in range(n_pages):
        pp = jnp.exp(s_ref[:, p * LANES:(p + 1) * LANES] - m)
        l = l + jnp.sum(pp, axis=1, keepdims=True)
        acc = acc + lax.dot_general(pp, v_refs[p][...], NT, preferred_element_type=F32)
    acc = jnp.where(own, acc / l, 0.0)
    y = acc[0:s_len, :]
    for h in range(1, nh):
        y = y + acc[h * s_len:(h + 1) * s_len, :]
    o_ref[...] = y.astype(o_ref.dtype)


def _fox_sample(page_table, q, k_new, v_new, col, rown, cache_k, cache_v, lc, layer, nb, s_len):
    n_pages = page_table.shape[1]
    n_phys, page = cache_k.shape[0], cache_k.shape[1]
    depth = cache_k.shape[2]
    ck = cache_k.transpose(0, 2, 3, 4, 1).reshape(n_phys, depth, A_WIDTH, page)
    cv = cache_v.transpose(0, 2, 3, 4, 1).reshape(n_phys, depth, A_WIDTH, page)
    tok = lambda c: pl.BlockSpec((s_len, c), lambda b, pt: (b, 0))
    page_specs = [pl.BlockSpec((None, None, A_WIDTH, page),
                               lambda b, pt, j=j: (pt[b, j], layer, 0, 0)) for j in range(n_pages)]
    lc_spec = pl.BlockSpec(lc.shape, lambda b, pt: (0, 0, 0), pipeline_mode=pl.Buffered(1))
    grid_spec = pltpu.PrefetchScalarGridSpec(
        num_scalar_prefetch=1,
        grid=(nb,),
        in_specs=[tok(A_WIDTH), tok(A_WIDTH), tok(A_WIDTH), tok(LANES),
                  pl.BlockSpec((None, SM_ROWS, s_len), lambda b, pt: (b, 0, 0))]
                 + page_specs + page_specs + [lc_spec],
        out_specs=tok(A_WIDTH),
        scratch_shapes=[pltpu.VMEM((A_HEADS * s_len, n_pages * LANES), F32)],
    )
    return pl.pallas_call(
        functools.partial(_fox_sample_kernel, n_pages=n_pages, layer=layer, s_len=s_len),
        grid_spec=grid_spec,
        out_shape=jax.ShapeDtypeStruct((nb * s_len, A_WIDTH), BF16),
        compiler_params=_cparams("arbitrary"),
        name="fox_sample",
    )(page_table, q, k_new, v_new, col, rown, *([ck] * n_pages), *([cv] * n_pages), lc)


def _mlstm_kernel(q_ref, k_ref, v_ref, so_ref, col_ref, row_ref, g_ref, c0_ref, n0_ref, m0_ref, _,
                  y_ref, c1_ref, n1_ref, m1_ref, *, t, bb, group):
    nh, hd = M_HEADS, M_HEAD_DIM
    r = lax.broadcasted_iota(jnp.int32, (t, t), 0)
    c = lax.broadcasted_iota(jnp.int32, (t, t), 1)
    causal = c <= r
    g = g_ref[...]
    hs = lambda h: slice(h * hd, (h + 1) * hd)
    last = slice(t - 1, t)
    units = [(u, h) for u in range(group) for h in range(nh)]

    def group_body(gi, _):
        bis = [gi * group + u for u in range(group)]
        r0s = [pl.multiple_of(bi * t, t) for bi in bis]
        qa = [q_ref[pl.ds(r0, t), :].astype(F32) for r0 in r0s]
        ka = [k_ref[pl.ds(r0, t), :].astype(F32) for r0 in r0s]
        va = [v_ref[pl.ds(r0, t), :].astype(F32) for r0 in r0s]
        colv = [col_ref[pl.ds(r0, t), :] for r0 in r0s]
        rows = [row_ref[bi] for bi in bis]
        cs = {(u, h): c0_ref[bis[u], h] for u, h in units}
        ns = {(u, h): n0_ref[bis[u], h] for u, h in units}
        ms = {(u, h): m0_ref[bis[u], h] for u, h in units}
        qk = {(u, h): lax.dot_general(qa[u][:, hs(h)], ka[u][:, hs(h)], NT, preferred_element_type=F32)
              for u, h in units}
        qc = {(u, h): lax.dot_general(qa[u][:, hs(h)], cs[u, h], NT, preferred_element_type=F32) for u, h in units}
        wqk, a_, m_t_, wl_ = {}, {}, {}, {}
        for u, h in units:
            f_col = colv[u][:, SM_MF + h:SM_MF + h + 1]
            i_col = colv[u][:, SM_MI + h:SM_MI + h + 1]
            g_max = colv[u][:, SM_MG + h:SM_MG + h + 1]
            f_row = rows[u][SM_MF + h:SM_MF + h + 1, :]
            i_row = rows[u][SM_MI + h:SM_MI + h + 1, :]
            mx = jnp.maximum(ms[u, h], g_max)
            m_t = f_col + mx
            d = jnp.where(causal, f_col + (i_row - f_row), -jnp.inf)
            wqk[u, h] = jnp.exp(d - m_t) * qk[u, h]
            a_[u, h] = jnp.exp(ms[u, h] - mx)
            m_t_[u, h] = m_t
            wl_[u, h] = jnp.exp((i_col - f_col) + (f_col[last, :] - m_t[last, :]))
        pv = {(u, h): _dot(wqk[u, h], va[u][:, hs(h)]) for u, h in units}
        ys = {}
        for u, h in units:
            a, m_t, w_last = a_[u, h], m_t_[u, h], wl_[u, h]
            num = a * qc[u, h] + pv[u, h]
            den = (a * jnp.sum(qa[u][:, hs(h)] * ns[u, h], axis=1, keepdims=True)
                   + jnp.sum(wqk[u, h], axis=1, keepdims=True))
            h_t = num / jnp.maximum(jnp.abs(den), jnp.exp(-m_t))
            ys[u, h] = h_t * lax.rsqrt(jnp.mean(h_t * h_t, axis=1, keepdims=True) + EPS)
            a_last = a[last, :]
            vw = va[u][:, hs(h)] * w_last
            c1_ref[bis[u], h] = a_last * cs[u, h] + _dot(vw.T, ka[u][:, hs(h)])
            n1_ref[bis[u], h] = a_last * ns[u, h] + jnp.sum(ka[u][:, hs(h)] * w_last, axis=0, keepdims=True)
            m1_ref[bis[u], h] = m_t[last, :]
        for u in range(group):
            hn = jnp.concatenate([ys[u, h] for h in range(nh)], axis=1) * g
            y_ref[pl.ds(r0s[u], t), :] = (so_ref[pl.ds(r0s[u], t), :] * hn).astype(y_ref.dtype)
        return 0

    lax.fori_loop(0, bb // group, group_body, 0)


def _mlstm(q, k, v, so, col, rowt, g, c0, n0, m0, layer_in, c_prev, layer_out, depth, nb, t, bb):
    n = q.shape[0]
    nh, hd = M_HEADS, M_HEAD_DIM
    seq = lambda w: pl.BlockSpec((bb * t, w), lambda i: (i, 0))
    st_in = lambda a, b_: pl.BlockSpec((bb, None, nh, a, b_), lambda i: (i, layer_in, 0, 0, 0))
    st_out = lambda a, b_: pl.BlockSpec((bb, nh, a, b_), lambda i: (i, 0, 0, 0))
    if c_prev is None:
        c_prev = jnp.zeros((SUBLANES, LANES), F32)
        aliases = {}
    else:
        aliases = {10: 1}
    return pl.pallas_call(
        functools.partial(_mlstm_kernel, t=t, bb=bb, group=2 if bb % 2 == 0 else 1),
        grid=(nb // bb,),
        in_specs=[seq(M_WIDTH), seq(M_WIDTH), seq(M_WIDTH), seq(M_WIDTH), seq(LANES),
                  pl.BlockSpec((bb, SM_ROWS, t), lambda i: (i, 0, 0)),
                  _const_spec((1, M_WIDTH)),
                  st_in(hd, hd), st_in(1, hd), st_in(1, 1),
                  pl.BlockSpec(memory_space=pl.ANY)],
        out_specs=[seq(M_WIDTH),
                   pl.BlockSpec((bb, None, nh, hd, hd), lambda i: (i, layer_out, 0, 0, 0)),
                   st_out(1, hd), st_out(1, 1)],
        out_shape=[jax.ShapeDtypeStruct((n, M_WIDTH), BF16),
                   jax.ShapeDtypeStruct((nb, depth, nh, hd, hd), F32),
                   jax.ShapeDtypeStruct((nb, nh, 1, hd), F32),
                   jax.ShapeDtypeStruct((nb, nh, 1, 1), F32)],
        input_output_aliases=aliases,
        compiler_params=_cparams("parallel"),
        name="mlstm",
    )(q, k, v, so, col, rowt, g, c0, n0.reshape(n0.shape[:3] + (1, hd)), m0.reshape(m0.shape[:3] + (1, 1)), c_prev)


C_EXT = M_HEAD_DIM + 16


def _mlstm_t_kernel(q_ref, k_ref, vt_ref, so_ref, col_ref, row_ref, g_ref, _,
                    y_ref, c1_ref, n1_ref, m1_ref, *, t, chunk):
    nh, hd = M_HEADS, M_HEAD_DIM
    s_idx = lax.broadcasted_iota(jnp.int32, (chunk, chunk), 0)
    l_idx = lax.broadcasted_iota(jnp.int32, (chunk, chunk), 1)
    causal = s_idx <= l_idx
    heads = range(nh)
    hs = lambda h: slice(h * hd, (h + 1) * hd)
    last = slice(chunk - 1, chunk)
    pad_rows = jnp.zeros((C_EXT - hd - 1, chunk), F32)

    def chunk_body(ci, carry):
        cs, ms = carry
        r0 = pl.multiple_of(ci * chunk, chunk)
        qa = q_ref[pl.ds(r0, chunk), :]
        ka = k_ref[pl.ds(r0, chunk), :]
        vta = vt_ref[:, pl.ds(r0, chunk)]
        colv = col_ref[pl.ds(r0, chunk), :]
        rows = row_ref[:, pl.ds(r0, chunk)]
        qk = [lax.dot_general(ka[:, hs(h)], qa[:, hs(h)], NT, preferred_element_type=F32) for h in heads]
        cq = [lax.dot_general(cs[h].astype(BF16), qa[:, hs(h)], NT, preferred_element_type=F32) for h in heads]
        wqk, a_, m_t_, wl_ = [], [], [], []
        for h in heads:
            f_row = rows[SM_MF + h:SM_MF + h + 1, :]
            i_row = rows[SM_MI + h:SM_MI + h + 1, :]
            g_max = rows[SM_MG + h:SM_MG + h + 1, :]
            g_col = colv[:, SM_MI + h:SM_MI + h + 1] - colv[:, SM_MF + h:SM_MF + h + 1]
            mx = jnp.maximum(ms[h], g_max)
            m_t = f_row + mx
            d = jnp.where(causal, f_row + g_col, -jnp.inf)
            wqk.append(jnp.exp(d - m_t) * qk[h])
            a_.append(jnp.exp(ms[h] - mx))
            m_t_.append(m_t)
            wl_.append(jnp.exp((i_row - f_row) + (f_row[:, last] - m_t[:, last])))
        pv = [_dot(vta[hs(h), :], wqk[h].astype(BF16)) for h in heads]
        ys, cs_new, ms_new = [], [], []
        for h in heads:
            a, m_t = a_[h], m_t_[h]
            num = a * cq[h][0:hd, :] + pv[h]
            den = a * cq[h][hd:hd + 1, :] + jnp.sum(wqk[h], axis=0, keepdims=True)
            h_t = num / jnp.maximum(jnp.abs(den), jnp.exp(-m_t))
            ys.append((h_t * lax.rsqrt(jnp.mean(h_t * h_t, axis=0, keepdims=True) + EPS)).T)
            vw = jnp.concatenate([vta[hs(h), :].astype(F32) * wl_[h], wl_[h], pad_rows], axis=0)
            cs_new.append(a[:, last] * cs[h] + _dot(vw.astype(BF16), ka[:, hs(h)]))
            ms_new.append(m_t[:, last])
        hn = jnp.concatenate(ys, axis=1) * g_ref[...]
        y_ref[pl.ds(r0, chunk), :] = (so_ref[pl.ds(r0, chunk), :] * hn).astype(y_ref.dtype)
        return cs_new, ms_new

    init = ([jnp.zeros((C_EXT, hd), F32)] * nh, [jnp.zeros((1, 1), F32)] * nh)
    cs, ms = lax.fori_loop(0, t // chunk, chunk_body, init, unroll=True)
    for h in heads:
        c1_ref[h] = cs[h][0:hd, :]
        n1_ref[h] = cs[h][hd:hd + 1, :]
        m1_ref[h] = ms[h]


def _mlstm_t(q, k, vt, so, col, rowt, g, c_prev, layer_out, depth, nb, t, chunk):
    n = q.shape[0]
    nh, hd = M_HEADS, M_HEAD_DIM
    seq = lambda w: pl.BlockSpec((t, w), lambda i: (i, 0))
    st_out = lambda a, b_: pl.BlockSpec((None, nh, a, b_), lambda i: (i, 0, 0, 0))
    if c_prev is None:
        c_prev = jnp.zeros((SUBLANES, LANES), F32)
        aliases = {}
    else:
        aliases = {7: 1}
    return pl.pallas_call(
        functools.partial(_mlstm_t_kernel, t=t, chunk=chunk),
        grid=(nb,),
        in_specs=[seq(M_WIDTH), seq(M_WIDTH), pl.BlockSpec((None, M_WIDTH, t), lambda i: (i, 0, 0)),
                  seq(M_WIDTH), seq(LANES), pl.BlockSpec((None, SM_ROWS, t), lambda i: (i, 0, 0)),
                  _const_spec((1, M_WIDTH)), pl.BlockSpec(memory_space=pl.ANY)],
        out_specs=[seq(M_WIDTH),
                   pl.BlockSpec((None, None, nh, hd, hd), lambda i: (i, layer_out, 0, 0, 0)),
                   st_out(1, hd), st_out(1, 1)],
        out_shape=[jax.ShapeDtypeStruct((n, M_WIDTH), BF16),
                   jax.ShapeDtypeStruct((nb, depth, nh, hd, hd), F32),
                   jax.ShapeDtypeStruct((nb, nh, 1, hd), F32),
                   jax.ShapeDtypeStruct((nb, nh, 1, 1), F32)],
        input_output_aliases=aliases,
        compiler_params=_cparams("parallel"),
        name="mlstm_t",
    )(q, k, vt, so, col, rowt, g, c_prev)


def _conv_carry_init(state_ref, carry_ref, tiles_per_seq):
    @pl.when((pl.program_id(0) % tiles_per_seq) == 0)
    def _():
        carry_ref[...] = state_ref[...]


def _conv_long(u, w_ref, c0, carry_ref):
    tm, ch = u.shape
    prev = carry_ref[:, c0:c0 + ch]
    row = lax.broadcasted_iota(jnp.int32, (tm, 1), 0)
    p1 = jnp.where(row == 0, prev[1:2, :], pltpu.roll(u, 1, axis=0))
    p2 = jnp.where(row == 0, prev[0:1, :], jnp.where(row == 1, prev[1:2, :], pltpu.roll(u, 2, axis=0)))
    carry_ref[:, c0:c0 + ch] = u[tm - 2:tm, :]
    return p2 * w_ref[0:1, c0:c0 + ch] + p1 * w_ref[1:2, c0:c0 + ch] + u * w_ref[2:3, c0:c0 + ch]


def _conv_short(u, w_ref, c0, fill_ref, s_len):
    tm, ch = u.shape
    fill = fill_ref[:, c0:c0 + ch]
    rmod = lax.broadcasted_iota(jnp.int32, (tm, 1), 0) & (s_len - 1)
    p1 = jnp.where(rmod >= 1, pltpu.roll(u, 1, axis=0), pltpu.roll(fill, tm - 1, axis=0))
    p2 = jnp.where(rmod >= 2, pltpu.roll(u, 2, axis=0), fill)
    return p2 * w_ref[0:1, c0:c0 + ch] + p1 * w_ref[1:2, c0:c0 + ch] + u * w_ref[2:3, c0:c0 + ch]


def _merge_kernel(x_ref, g_ref, wc_ref, bc_ref, wg_ref, bg_ref, ws_ref, ya_ref, ym_ref,
                  woa_ref, wom_ref, woc_ref, wo_ref, st_ref, x1_ref, cst_ref, *scratch,
                  long_seq, tiles_per_seq, s_len):
    x = x_ref[...]
    h = _rms(x, g_ref[...]).astype(BF16)
    cw = C_WIDTH
    cb = _dot(h, wc_ref[:, 0:cw]) + bc_ref[:, 0:cw]
    u = (_dot(h, wc_ref[:, cw:2 * cw]) + bc_ref[:, cw:2 * cw]) * (_dot(h, wc_ref[:, 2 * cw:3 * cw]) + bc_ref[:, 2 * cw:3 * cw])
    if long_seq:
        _conv_carry_init(st_ref, scratch[0], tiles_per_seq)
        uc = _conv_long(u, ws_ref, 0, scratch[0])
        cst_ref[...] = u[u.shape[0] - 2:, :]
    else:
        uc = _conv_short(u, ws_ref, 0, st_ref, s_len)
        cst_ref[...] = u
    yc = (cb * uc).astype(BF16)
    d = D_MODEL

    def gate(i):
        return _sigmoid(_dot(h, wg_ref[:, i * d:(i + 1) * d]) + bg_ref[:, i * d:(i + 1) * d])

    merged = gate(0) * _dot(ya_ref[...], woa_ref[...])
    merged = merged + gate(1) * _dot(ym_ref[...], wom_ref[...])
    merged = merged + gate(2) * _dot(yc, woc_ref[...])
    x1_ref[...] = x + _dot(merged.astype(BF16), wo_ref[...])


def _merge(x, g, wc, bc, wg, bg, ws, ya, ym, woa, wom, woc, wo, st, nb, t, tm):
    n = x.shape[0]
    long_seq = t >= tm
    tiles_per_seq = max(t // tm, 1)
    row = lambda c: pl.BlockSpec((tm, c), lambda i: (i, 0))
    if long_seq:
        st_spec = pl.BlockSpec((None, 2, C_WIDTH), lambda i: (i // tiles_per_seq, 0, 0))
        cst_spec = pl.BlockSpec((None, 2, C_WIDTH), lambda i: (i // tiles_per_seq, 0, 0))
        cst_shape = jax.ShapeDtypeStruct((nb, 2, C_WIDTH), F32)
        scratch = [pltpu.VMEM((2, C_WIDTH), F32)]
    else:
        st_spec = row(C_WIDTH)
        cst_spec = row(C_WIDTH)
        cst_shape = jax.ShapeDtypeStruct((n, C_WIDTH), F32)
        scratch = []
    return pl.pallas_call(
        functools.partial(_merge_kernel, long_seq=long_seq, tiles_per_seq=tiles_per_seq, s_len=t),
        grid=(n // tm,),
        in_specs=[row(D_MODEL), _const_spec((1, D_MODEL)),
                  _const_spec(wc.shape), _const_spec(bc.shape), _const_spec(wg.shape), _const_spec(bg.shape),
                  _const_spec(ws.shape), row(A_WIDTH), row(M_WIDTH),
                  _const_spec(woa.shape), _const_spec(wom.shape), _const_spec(woc.shape), _const_spec(wo.shape),
                  st_spec],
        out_specs=[row(D_MODEL), cst_spec],
        out_shape=[jax.ShapeDtypeStruct((n, D_MODEL), F32), cst_shape],
        scratch_shapes=scratch,
        compiler_params=_cparams("arbitrary"),
        name="merge",
    )(x, g, wc, bc, wg, bg, ws, ya, ym, woa, wom, woc, wo, st)


FF_CHUNK = D_FF // 2


def _ffn_kernel(x_ref, pe_ref, gf_ref, wup_ref, wfc_ref, bfc_ref, wdn_ref, gp_ref, wpg_ref, wpp_ref, gfin_ref,
                st_ref, xo_ref, fst_ref, *scratch, long_seq, tiles_per_seq, s_len, final):
    x = x_ref[...]
    tm = x.shape[0]
    h2 = _rms(x, gf_ref[...]).astype(BF16)
    acc = jnp.zeros((tm, D_MODEL), F32)
    if long_seq:
        _conv_carry_init(st_ref, scratch[0], tiles_per_seq)
    for half in range(D_FF // FF_CHUNK):
        c0 = half * FF_CHUNK
        ua = _dot(h2, wup_ref[:, c0:c0 + FF_CHUNK])
        ub = _dot(h2, wup_ref[:, D_FF + c0:D_FF + c0 + FF_CHUNK])
        if long_seq:
            uac = _conv_long(ua, wfc_ref, c0, scratch[0])
            fst_ref[:, c0:c0 + FF_CHUNK] = ua[tm - 2:, :]
        else:
            uac = _conv_short(ua, wfc_ref, c0, st_ref, s_len)
            fst_ref[:, c0:c0 + FF_CHUNK] = ua
        z = uac + bfc_ref[:, c0:c0 + FF_CHUNK]
        act = (z * _sigmoid(z) * ub).astype(BF16)
        acc = acc + _dot(act, wdn_ref[c0:c0 + FF_CHUNK, :])
    x2 = x + acc
    gate = _sigmoid(_dot(_rms(x2, gp_ref[...]).astype(BF16), wpg_ref[...]))
    x3 = x2 + gate * _dot(pe_ref[...].astype(BF16), wpp_ref[...])
    if final:
        x3 = _rms(x3, gfin_ref[...])
    xo_ref[...] = x3


def _ffn(x, pe, gf, wup, wfc, bfc, wdn, gp, wpg, wpp, gfin, st, nb, t, tm, final):
    n = x.shape[0]
    long_seq = t >= tm
    tiles_per_seq = max(t // tm, 1)
    row = lambda c: pl.BlockSpec((tm, c), lambda i: (i, 0))
    if long_seq:
        st_spec = pl.BlockSpec((None, 2, D_FF), lambda i: (i // tiles_per_seq, 0, 0))
        fst_spec = pl.BlockSpec((None, 2, D_FF), lambda i: (i // tiles_per_seq, 0, 0))
        fst_shape = jax.ShapeDtypeStruct((nb, 2, D_FF), F32)
        scratch = [pltpu.VMEM((2, D_FF), F32)]
    else:
        st_spec = row(D_FF)
        fst_spec = row(D_FF)
        fst_shape = jax.ShapeDtypeStruct((n, D_FF), F32)
        scratch = []
    return pl.pallas_call(
        functools.partial(_ffn_kernel, long_seq=long_seq, tiles_per_seq=tiles_per_seq, s_len=t, final=final),
        grid=(n // tm,),
        in_specs=[row(D_MODEL), row(pe.shape[1]), _const_spec((1, D_MODEL)),
                  _const_spec(wup.shape), _const_spec(wfc.shape), _const_spec(bfc.shape), _const_spec(wdn.shape),
                  _const_spec((1, D_MODEL)), _const_spec(wpg.shape), _const_spec(wpp.shape),
                  _const_spec((1, D_MODEL)), st_spec],
        out_specs=[row(D_MODEL), fst_spec],
        out_shape=[jax.ShapeDtypeStruct((n, D_MODEL), F32), fst_shape],
        scratch_shapes=scratch,
        compiler_params=_cparams("arbitrary"),
        name="ffn",
    )(x, pe, gf, wup, wfc, bfc, wdn, gp, wpg, wpp, gfin, st)


def _row_form(col, nb, t):
    return col[:, :SM_ROWS].reshape(nb, t, SM_ROWS).transpose(0, 2, 1)


def _short_fill(state, s_len):
    nb, _, ch = state.shape
    return jnp.pad(state, ((0, 0), (0, s_len - 2), (0, 0))).reshape(nb * s_len, ch)


def _pick(n, pref):
    return pref if n % pref == 0 else n


def _layer(x, pe, w, nb, t, chunk, past, final, carry):
    n = x.shape[0]
    layer, depth = carry["layer"], carry["depth"]
    tm_in = _pick(n, 512)
    tm_merge = _pick(n, 512)
    tm_ffn = _pick(n, 256)
    if past is None:
        kvp = dict(layer=layer, depth=depth, prev=carry["kvt"], nb=nb, t=t, wt=w["w_kvt"], bt=w["b_kvt"])
        q, k, mq, mk, so, sm, kt, vt, mvt = _inproj(x, w["g_mix"], w["w_am"], w["b_am"], tm_in, kvp)
        col, logf = _gates(sm, t, t, chunk)
        rowt = _row_form(col, nb, t)
        ya = _fox_prompt(q, k, vt, col, layer, nb, t, _pick(t, 512))
        k, v = kt, vt
        ym, c1, n1, m1 = _mlstm_t(mq, mk, mvt, so, col, rowt, w["g_mhead"], carry["c"], layer, depth, nb, t, chunk)
        conv0 = jnp.zeros((nb, 2, C_WIDTH), F32)
        ffn0 = jnp.zeros((nb, 2, D_FF), F32)
    else:
        q, k, v, mq, mk, mv, so, sm = _inproj(x, w["g_mix"], w["w_am"], w["b_am"], tm_in)
        col, logf = _gates(sm, n, t, t)
        rowt = _row_form(col, nb, t)
        ya = _fox_sample(past["page_table"], q, k, v, col, rowt, past["cache_k"], past["cache_v"],
                         past["lc"], layer, nb, t)
        conv0, ffn0 = past["conv0"], past["ffn0"]
        ym, c1, n1, m1 = _mlstm(mq, mk, mv, so, col, rowt, w["g_mhead"], past["c0"], past["n0"], past["m0"], layer,
                                carry["c"], layer, depth, nb, t, _pick(nb, SUBLANES))
    long_merge = t >= tm_merge
    long_ffn = t >= tm_ffn
    x1, conv1 = _merge(x, w["g_mix"], w["w_c"], w["b_c"], w["w_g"], w["b_g"], w["w_sconv"], ya, ym,
                       w["w_oa"], w["w_om"], w["w_oc"], w["w_o"],
                       conv0 if long_merge else _short_fill(conv0, t), nb, t, tm_merge)
    x3, ffn1 = _ffn(x1, pe, w["g_ffn"], w["w_up"], w["w_fconv"], w["b_fconv"], w["w_down"],
                    w["g_ple"], w["w_ple_gate"], w["w_ple"], w["g_final"],
                    ffn0 if long_ffn else _short_fill(ffn0, t), nb, t, tm_ffn, final)
    if not long_merge:
        conv1 = conv1.reshape(nb, t, C_WIDTH)[:, t - 2:, :]
    if not long_ffn:
        ffn1 = ffn1.reshape(nb, t, D_FF)[:, t - 2:, :]
    if past is not None:
        k = k.reshape(nb, t, A_HEADS, A_HEAD_DIM)
        v = v.reshape(nb, t, A_HEADS, A_HEAD_DIM)
    state = (k, v, logf.reshape(nb, t, A_HEADS), c1, n1.reshape(nb, M_HEADS, M_HEAD_DIM),
             m1.reshape(nb, M_HEADS), conv1, ffn1)
    return x3, state


def _layer_weights(l, w_in, b_in, g_mix, g_mhead, w_sconv, w_oa, w_om, w_oc, w_o, g_ffn, w_up,
                   w_fconv, b_fconv, w_down, g_ple, w_ple_gate, w_ple, g_final):
    a0 = 0
    af0 = 3 * A_WIDTH
    m0 = af0 + A_HEADS
    mi0 = m0 + 3 * M_WIDTH
    mo0 = mi0 + 2 * M_HEADS
    c0 = mo0 + M_WIDTH
    g0 = c0 + 3 * C_WIDTH
    wl, bl = w_in[l], b_in[l]
    pad = LANES - SM_USED

    def cols(a, lo, hi):
        return a[..., lo:hi]

    def am(a):
        return jnp.concatenate(
            [cols(a, a0, af0), cols(a, m0, mi0), cols(a, mo0, c0),
             cols(a, af0, m0), cols(a, mi0, mo0), jnp.zeros(a.shape[:-1] + (pad,), a.dtype)], axis=-1)

    row = lambda a: a.reshape(1, -1)
    return {
        "w_am": am(wl).astype(BF16), "b_am": row(am(bl)),
        "w_kvt": jnp.concatenate([cols(wl, A_WIDTH, af0), cols(wl, m0 + 2 * M_WIDTH, mi0)], axis=-1).T.astype(BF16),
        "b_kvt": jnp.concatenate([cols(bl, A_WIDTH, af0), cols(bl, m0 + 2 * M_WIDTH, mi0)], axis=-1).reshape(-1, 1),
        "w_c": cols(wl, c0, g0).astype(BF16), "b_c": row(cols(bl, c0, g0)),
        "w_g": cols(wl, g0, g0 + 3 * D_MODEL).astype(BF16), "b_g": row(cols(bl, g0, g0 + 3 * D_MODEL)),
        "g_mix": row(g_mix[l]), "g_mhead": row(g_mhead[l]), "w_sconv": w_sconv[l],
        "w_oa": w_oa[l].astype(BF16), "w_om": w_om[l].astype(BF16), "w_oc": w_oc[l].astype(BF16),
        "w_o": w_o[l].astype(BF16), "g_ffn": row(g_ffn[l]), "w_up": w_up[l].astype(BF16),
        "w_fconv": w_fconv[l], "b_fconv": row(b_fconv[l]), "w_down": w_down[l].astype(BF16),
        "g_ple": row(g_ple[l]), "w_ple_gate": w_ple_gate[l].astype(BF16), "w_ple": w_ple[l].astype(BF16),
        "g_final": row(g_final),
    }


def kernel(x_prompt, x_sample, cache_k, cache_v, cache_logf, state_mlstm_C, state_mlstm_n, state_mlstm_m, state_conv, state_ffn_conv, page_table, p_prompt, p_sample, w_in, b_in, g_mix, g_mhead, w_sconv, w_oa, w_om, w_oc, w_o, g_ffn, w_up, w_fconv, b_fconv, w_down, g_ple, w_ple_gate, w_ple, g_final):
    nbp, tp, _ = x_prompt.shape
    nbs, ts, _ = x_sample.shape
    depth = w_in.shape[0]
    n_phys, page = cache_logf.shape[0], cache_logf.shape[1]
    assert page == LANES and ts == SUBLANES and depth * A_HEADS == SM_USED

    lf_t = cache_logf.reshape(n_phys, page, depth * A_HEADS).transpose(0, 2, 1).reshape(-1, LANES)
    lc = _pool_scan(lf_t, _pick(lf_t.shape[0], 2048)).reshape(n_phys, depth * A_HEADS, LANES)

    xp = x_prompt.reshape(nbp * tp, D_MODEL)
    xs = x_sample.reshape(nbs * ts, D_MODEL)
    chunk_p = _pick(tp, 256)
    new_p, new_s = [], []
    kvt = c_p = c_s = None
    for l in range(depth):
        w = _layer_weights(l, w_in, b_in, g_mix, g_mhead, w_sconv, w_oa, w_om, w_oc, w_o, g_ffn, w_up,
                           w_fconv, b_fconv, w_down, g_ple, w_ple_gate, w_ple, g_final)
        final = l == depth - 1
        xp, st_p = _layer(xp, p_prompt[l].reshape(nbp * tp, -1), w, nbp, tp, chunk_p, None, final,
                          {"layer": l, "depth": depth, "kvt": kvt, "c": c_p})
        kvt, c_p = st_p[:2], st_p[3]
        past = {"page_table": page_table, "cache_k": cache_k, "cache_v": cache_v, "lc": lc,
                "c0": state_mlstm_C, "n0": state_mlstm_n, "m0": state_mlstm_m,
                "conv0": state_conv[:, l], "ffn0": state_ffn_conv[:, l]}
        xs, st_s = _layer(xs, p_sample[l].reshape(nbs * ts, -1), w, nbs, ts, ts, past, final,
                          {"layer": l, "depth": depth, "kvt": None, "c": c_s})
        c_s = st_s[3]
        new_p.append(st_p)
        new_s.append(st_s)

    def stack(per_layer, c_all, first):
        rows = [jnp.stack([s[i] for s in per_layer], axis=2) for i in range(first, 3)]
        states = [jnp.stack([s[i] for s in per_layer], axis=1) for i in range(4, 8)]
        return tuple(rows + [c_all] + states)

    kv_p = tuple(a.reshape(nbp, depth, A_HEADS, A_HEAD_DIM, tp).transpose(0, 4, 1, 2, 3) for a in kvt)
    return ((xp.reshape(nbp, tp, D_MODEL), xs.reshape(nbs, ts, D_MODEL)) + kv_p + stack(new_p, c_p, 2)
            + stack(new_s, c_s, 0))
```

```python
import functools

import jax
import jax.numpy as jnp
from jax import lax
from jax.experimental import pallas as pl
from jax.experimental.pallas import tpu as pltpu

F32 = jnp.float32
BF16 = jnp.bfloat16

D_MODEL = 1024
A_HEADS = 8
A_HEAD_DIM = 64
A_WIDTH = A_HEADS * A_HEAD_DIM
M_HEADS = 4
M_HEAD_DIM = 128
M_WIDTH = M_HEADS * M_HEAD_DIM
C_WIDTH = 512
D_FF = 2816
EPS = 1e-6
LANES = 128
SUBLANES = 8
VMEM_LIMIT = 56 * 1024 * 1024

SM_AF = 0
SM_MI = 8
SM_MF = 12
SM_USED = 16
SM_MG = 16
SM_ROWS = 24

NT = (((1,), (1,)), ((), ()))
LOG2E = 1.4426950408889634


def _cparams(*sem):
    return pltpu.CompilerParams(dimension_semantics=sem, vmem_limit_bytes=VMEM_LIMIT)


def _const_spec(shape):
    nd = len(shape)
    return pl.BlockSpec(shape, lambda *_: (0,) * nd, pipeline_mode=pl.Buffered(1))


def _rms(x, g):
    return x * lax.rsqrt(jnp.mean(x * x, axis=-1, keepdims=True) + EPS) * g


def _log_sigmoid(x):
    return jnp.minimum(x, 0.0) - jnp.log1p(jnp.exp(-jnp.abs(x)))


def _sigmoid(x):
    return 1.0 / (1.0 + jnp.exp(-x))


def _dot(a, b):
    return jnp.dot(a, b, preferred_element_type=F32)


def _inproj_kernel(*refs, transposed_kv):
    if transposed_kv:
        (x_ref, g_ref, w_ref, b_ref, wt_ref, bt_ref, _, _,
         q_ref, k_ref, mq_ref, mk_ref, so_ref, sm_ref, kt_ref, vt_ref, mvt_ref) = refs
    else:
        x_ref, g_ref, w_ref, b_ref, q_ref, k_ref, v_ref, mq_ref, mk_ref, mv_ref, so_ref, sm_ref = refs
    h = _rms(x_ref[...], g_ref[...]).astype(BF16)

    def proj(c0, n):
        return _dot(h, w_ref[:, c0:c0 + n]) + b_ref[:, c0:c0 + n]

    q_scale = A_HEAD_DIM ** -0.5 * (LOG2E if transposed_kv else 1.0)
    q_ref[...] = (proj(0, A_WIDTH) * q_scale).astype(BF16)
    if transposed_kv:
        k_ref[...] = proj(A_WIDTH, A_WIDTH).astype(BF16)
        kt_ref[...] = (lax.dot_general(wt_ref[0:A_WIDTH, :], h, NT, preferred_element_type=F32)
                       + bt_ref[0:A_WIDTH, :])
        vt_ref[...] = (lax.dot_general(wt_ref[A_WIDTH:2 * A_WIDTH, :], h, NT, preferred_element_type=F32)
                       + bt_ref[A_WIDTH:2 * A_WIDTH, :])
        mvt_ref[...] = (lax.dot_general(wt_ref[2 * A_WIDTH:, :], h, NT, preferred_element_type=F32)
                        + bt_ref[2 * A_WIDTH:, :]).astype(BF16)
    else:
        k_ref[...] = proj(A_WIDTH, A_WIDTH)
        v_ref[...] = proj(2 * A_WIDTH, A_WIDTH)
    o = 3 * A_WIDTH
    mq_ref[...] = proj(o, M_WIDTH).astype(BF16)
    mk_ref[...] = (proj(o + M_WIDTH, M_WIDTH) * (M_HEAD_DIM ** -0.5)).astype(BF16)
    if not transposed_kv:
        mv_ref[...] = proj(o + 2 * M_WIDTH, M_WIDTH).astype(BF16)
    so_ref[...] = _sigmoid(proj(o + 3 * M_WIDTH, M_WIDTH))
    sm_ref[...] = proj(o + 4 * M_WIDTH, LANES)


def _inproj(x, g, w, b, tm, kv=None):
    n = x.shape[0]
    wcols = w.shape[1]
    row = lambda c: pl.BlockSpec((tm, c), lambda i: (i, 0))
    common = [
        jax.ShapeDtypeStruct((n, M_WIDTH), BF16),
        jax.ShapeDtypeStruct((n, M_WIDTH), BF16),
        jax.ShapeDtypeStruct((n, M_WIDTH), BF16),
        jax.ShapeDtypeStruct((n, M_WIDTH), F32),
        jax.ShapeDtypeStruct((n, LANES), F32),
    ]
    common_specs = [row(M_WIDTH)] * 4 + [row(LANES)]
    in_specs = [row(D_MODEL), _const_spec((1, D_MODEL)), _const_spec((D_MODEL, wcols)), _const_spec((1, wcols))]
    if kv is None:
        return pl.pallas_call(
            functools.partial(_inproj_kernel, transposed_kv=False),
            grid=(n // tm,),
            in_specs=in_specs,
            out_specs=[row(A_WIDTH)] * 3 + common_specs,
            out_shape=[jax.ShapeDtypeStruct((n, A_WIDTH), BF16), jax.ShapeDtypeStruct((n, A_WIDTH), F32),
                       jax.ShapeDtypeStruct((n, A_WIDTH), F32)] + common,
            compiler_params=_cparams("parallel"),
            name="inproj",
        )(x, g, w, b)
    nb, t, depth, layer = kv["nb"], kv["t"], kv["depth"], kv["layer"]
    tps = t // tm
    t_shape = jax.ShapeDtypeStruct((nb, depth, A_WIDTH, t), F32)
    t_spec = pl.BlockSpec((None, None, A_WIDTH, tm), lambda i: (i // tps, layer, 0, i % tps))
    prev = kv["prev"]
    if prev is None:
        prev = (jnp.zeros((SUBLANES, LANES), F32),) * 2
        aliases = {}
    else:
        aliases = {6: 6, 7: 7}
    any_spec = pl.BlockSpec(memory_space=pl.ANY)
    no_mv = lambda items: items[:2] + items[3:]
    return pl.pallas_call(
        functools.partial(_inproj_kernel, transposed_kv=True),
        grid=(n // tm,),
        in_specs=in_specs + [_const_spec(kv["wt"].shape), _const_spec(kv["bt"].shape), any_spec, any_spec],
        out_specs=[row(A_WIDTH)] * 2 + no_mv(common_specs) + [t_spec, t_spec,
                   pl.BlockSpec((None, M_WIDTH, tm), lambda i: (i // tps, 0, i % tps))],
        out_shape=[jax.ShapeDtypeStruct((n, A_WIDTH), BF16), jax.ShapeDtypeStruct((n, A_WIDTH), BF16)]
                  + no_mv(common) + [t_shape, t_shape, jax.ShapeDtypeStruct((nb, M_WIDTH, t), BF16)],
        input_output_aliases=aliases,
        compiler_params=_cparams("parallel"),
        name="inproj_t",
    )(x, g, w, b, kv["wt"], kv["bt"], *prev)


def _gates_kernel(sm_ref, col_ref, logf_ref, *, seq_a, seg_m):
    x = sm_ref[...]
    rows = x.shape[0]
    lane = lax.broadcasted_iota(jnp.int32, (1, LANES), 1)
    row = lax.broadcasted_iota(jnp.int32, (rows, 1), 0)
    ls = _log_sigmoid(x)
    logf_ref[...] = ls[:, SM_AF:SM_AF + A_HEADS]
    is_a = lane < SM_MI
    is_i = (lane >= SM_MI) & (lane < SM_MF)
    is_f = (lane >= SM_MF) & (lane < SM_USED)
    y = jnp.where(is_i, x, ls)
    rmod_a = row & (seq_a - 1)
    rmod_m = row & (seg_m - 1)
    s = 1
    while s < max(seq_a, seg_m):
        take = jnp.zeros((rows, LANES), jnp.bool_)
        if s < seq_a:
            take = take | (is_a & (rmod_a >= s))
        if s < seg_m:
            take = take | (is_f & (rmod_m >= s))
        y = y + jnp.where(take, pltpu.roll(y, s, axis=0), 0.0)
        s *= 2
    nm = SM_MF - SM_MI
    is_g = (lane >= SM_MG) & (lane < SM_MG + nm)
    g = pltpu.roll(y, SM_MG - SM_MI, axis=1) - pltpu.roll(y, SM_MG - SM_MF, axis=1)
    s = 1
    while s < seg_m:
        g = jnp.where(is_g & (rmod_m >= s), jnp.maximum(g, pltpu.roll(g, s, axis=0)), g)
        s *= 2
    col_ref[...] = jnp.where(is_g, g, y)


def _gates(sm, rows, seq_a, seg_m):
    n = sm.shape[0]
    return pl.pallas_call(
        functools.partial(_gates_kernel, seq_a=seq_a, seg_m=seg_m),
        grid=(n // rows,),
        in_specs=[pl.BlockSpec((rows, LANES), lambda i: (i, 0))],
        out_specs=[pl.BlockSpec((rows, LANES), lambda i: (i, 0)),
                   pl.BlockSpec((rows, A_HEADS), lambda i: (i, 0))],
        out_shape=[jax.ShapeDtypeStruct((n, LANES), F32),
                   jax.ShapeDtypeStruct((n, A_HEADS), F32)],
        compiler_params=_cparams("parallel"),
        name="gates",
    )(sm)


V_EXT = A_HEAD_DIM + 16


def _fox_kernel(q_ref, k_ref, vt_ref, col_ref, o_ref, vb_ref, ka_ref, sa_ref, sb_ref, pa_ref, pb_ref, *, tq, t):
    hp = pl.program_id(1)
    lane = lax.broadcasted_iota(jnp.int32, (1, LANES), 1)
    hd = A_HEAD_DIM
    aug0 = [hd * (1 - hh) for hh in range(2)]

    ones_rows = jnp.ones((V_EXT - hd, t), BF16)
    for hh in range(2):
        vb_ref[hh, 0:hd, :] = vt_ref[hh * hd:(hh + 1) * hd, :].astype(BF16)
        vb_ref[hh, hd:V_EXT, :] = ones_rows
    colv = col_ref[...]
    kf = k_ref[...]
    for hh in range(2):
        ck = jnp.sum(jnp.where(lane == SM_AF + hp * 2 + hh, colv, 0.0), axis=1, keepdims=True) * (-LOG2E)
        c1 = ck.astype(BF16)
        r1 = ck - c1.astype(F32)
        c2 = r1.astype(BF16)
        c3 = (r1 - c2.astype(F32)).astype(BF16)
        a0 = aug0[hh]
        ka_ref[hh] = jnp.where(lane == a0, c1, jnp.where(lane == a0 + 1, c2, jnp.where(lane == a0 + 2, c3, kf)))

    slots = ((sa_ref, pa_ref), (sb_ref, pb_ref))
    kpos = lax.broadcasted_iota(jnp.int32, (tq, tq), 0)
    qpos = lax.broadcasted_iota(jnp.int32, (tq, tq), 1)

    def values(j, p_in):
        return [_dot(vb_ref[hh, :, j * tq:(j + 1) * tq], p_in[hh]) for hh in range(2)]

    for qi in range(t // tq):
        q2 = q_ref[qi * tq:(qi + 1) * tq, :]
        zero = jnp.zeros_like(q2)
        one = jnp.ones_like(q2)
        qm = [jnp.where((lane >= aug0[hh]) & (lane < aug0[hh] + 3), one,
                        jnp.where((lane >= hd) == bool(hh), q2, zero)) for hh in range(2)]

        def scores(j, s_out):
            for hh in range(2):
                s_out[hh] = lax.dot_general(ka_ref[hh, j * tq:(j + 1) * tq, :], qm[hh], NT,
                                            preferred_element_type=F32)

        def softmax(s_in, p_out, ms, diag):
            m_out, alphas = [], []
            for hh in range(2):
                s = s_in[hh]
                if diag:
                    s = jnp.where(kpos <= qpos, s, -jnp.inf)
                m_new = jnp.max(s, axis=0, keepdims=True)
                if ms is not None:
                    m_new = jnp.maximum(ms[hh], m_new)
                    alphas.append(jnp.exp2(ms[hh] - m_new))
                m_out.append(m_new)
                p_out[hh] = jnp.exp2(s - m_new).astype(BF16)
            return m_out, alphas

        scores(0, slots[qi % 2][0])
        ms = a_prev = accs = None
        for j in range(qi):
            (s_cur, p_cur), (s_nxt, p_nxt) = slots[(qi - j) % 2], slots[(qi - j - 1) % 2]
            pv = values(j - 1, p_nxt) if j > 0 else None
            scores(j + 1, s_nxt)
            ms, alphas = softmax(s_cur, p_cur, ms, False)
            if j == 1:
                accs = pv
            elif j > 1:
                accs = [a_prev[hh] * accs[hh] + pv[hh] for hh in range(2)]
            a_prev = alphas
        pv = values(qi - 1, pb_ref) if qi > 0 else None
        ms, alphas = softmax(sa_ref, pa_ref, ms, True)
        acc = values(qi, pa_ref)
        if qi == 1:
            acc = [alphas[hh] * pv[hh] + acc[hh] for hh in range(2)]
        elif qi > 1:
            acc = [alphas[hh] * (a_prev[hh] * accs[hh] + pv[hh]) + acc[hh] for hh in range(2)]
        yt = [acc[hh][0:hd, :] / acc[hh][hd:hd + 1, :] for hh in range(2)]
        o_ref[qi * tq:(qi + 1) * tq, :] = jnp.concatenate(yt, axis=0).T.astype(o_ref.dtype)


def _fox_prompt(q, k, vt, col, layer, nb, t, tq):
    n = q.shape[0]
    nq = t // tq
    pairs = A_WIDTH // LANES
    seq = pl.BlockSpec((t, LANES), lambda b, hp: (b, hp))
    return pl.pallas_call(
        functools.partial(_fox_kernel, tq=tq, t=t),
        grid=(nb, pairs),
        in_specs=[seq, seq,
                  pl.BlockSpec((None, None, LANES, t), lambda b, hp: (b, layer, hp, 0)),
                  pl.BlockSpec((t, LANES), lambda b, hp: (b, 0))],
        out_specs=seq,
        out_shape=jax.ShapeDtypeStruct((n, A_WIDTH), BF16),
        scratch_shapes=[pltpu.VMEM((2, V_EXT, t), BF16), pltpu.VMEM((2, t, LANES), BF16),
                        pltpu.VMEM((2, tq, tq), F32), pltpu.VMEM((2, tq, tq), F32),
                        pltpu.VMEM((2, tq, tq), BF16), pltpu.VMEM((2, tq, tq), BF16)],
        compiler_params=_cparams("parallel", "parallel"),
        name="fox_prompt",
    )(q, k, vt, col)


def _lane_cumsum(y):
    lane = lax.broadcasted_iota(jnp.int32, (1, LANES), 1)
    s = 1
    while s < LANES:
        y = y + jnp.where(lane >= s, pltpu.roll(y, s, axis=1), 0.0)
        s *= 2
    return y


def _fox_sample_kernel(pt_ref, q_ref, kn_ref, vn_ref, col_ref, rown_ref, *rest,
                       n_pages, layer, s_len):
    k_refs = rest[:n_pages]
    v_refs = rest[n_pages:2 * n_pages]
    lf_ref = rest[2 * n_pages]
    o_ref = rest[2 * n_pages + 1]
    s_ref = rest[2 * n_pages + 2]
    b = pl.program_id(0)
    nh = A_HEADS
    rows = nh * s_len
    lane_w = lax.broadcasted_iota(jnp.int32, (1, A_WIDTH), 1)
    rid = lax.broadcasted_iota(jnp.int32, (rows, 1), 0)
    own = jnp.right_shift(lane_w, 6) == jnp.right_shift(rid, 3)
    q = q_ref[...].astype(F32)
    qbd = jnp.where(own, jnp.concatenate([q] * nh, axis=0), 0.0)
    colv = col_ref[...]
    rown = rown_ref[...]
    lo = layer * nh

    lc_all = _lane_cumsum(jnp.concatenate([lf_ref[pt_ref[b, p], lo:lo + nh, :] for p in range(n_pages)], axis=0))
    lcs = [lc_all[p * nh:(p + 1) * nh, :] for p in range(n_pages)]
    tot = [lcs[p][:, LANES - 1:LANES] for p in range(n_pages)]
    base = [None] * n_pages
    suf = jnp.zeros((nh, 1), F32)
    for p in range(n_pages - 1, -1, -1):
        suf = suf + tot[p]
        base[p] = suf

    m_run = [None] * nh
    for p in range(n_pages):
        sp = _dot(qbd, k_refs[p][...])
        for h in range(nh):
            sl = slice(h * s_len, (h + 1) * s_len)
            cn = colv[:, SM_AF + h:SM_AF + h + 1]
            bias = (base[p][h:h + 1, :] - lcs[p][h:h + 1, :]) + cn
            sh = sp[sl, :] + bias
            s_ref[sl, p * LANES:(p + 1) * LANES] = sh
            mx = jnp.max(sh, axis=1, keepdims=True)
            m_run[h] = mx if m_run[h] is None else jnp.maximum(m_run[h], mx)
    sn = lax.dot_general(qbd, kn_ref[...], NT, preferred_element_type=F32)
    r = lax.broadcasted_iota(jnp.int32, (s_len, s_len), 0)
    c = lax.broadcasted_iota(jnp.int32, (s_len, s_len), 1)
    pn, m_all = [], []
    for h in range(nh):
        sl = slice(h * s_len, (h + 1) * s_len)
        cn = colv[:, SM_AF + h:SM_AF + h + 1]
        sh = sn[sl, :] + (cn - rown[SM_AF + h:SM_AF + h + 1, :])
        sh = jnp.where(c <= r, sh, -jnp.inf)
        pn.append(sh)
        m_all.append(jnp.maximum(m_run[h], jnp.max(sh, axis=1, keepdims=True)))
    m = jnp.concatenate(m_all, axis=0)
    p_new = jnp.exp(jnp.concatenate(pn, axis=0) - m)
    l = jnp.sum(p_new, axis=1, keepdims=True)
    acc = _dot(p_new, vn_ref[...])
    for p in range(n_pages):
        pp = jnp.exp(s_ref[:, p * LANES:(p + 1) * LANES] - m)
        l = l + jnp.sum(pp, axis=1, keepdims=True)
        acc = acc + lax.dot_general(pp, v_refs[p][...], NT, preferred_element_type=F32)
    acc = jnp.where(own, acc / l, 0.0)
    y = acc[0:s_len, :]
    for h in range(1, nh):
        y = y + acc[h * s_len:(h + 1) * s_len, :]
    o_ref[...] = y.astype(o_ref.dtype)


def _fox_sample(page_table, q, k_new, v_new, col, rown, cache_k, cache_v, lc, layer, nb, s_len):
    n_pages = page_table.shape[1]
    n_phys, page = cache_k.shape[0], cache_k.shape[1]
    depth = cache_k.shape[2]
    ck = cache_k.transpose(0, 2, 3, 4, 1).reshape(n_phys, depth, A_WIDTH, page)
    cv = cache_v.transpose(0, 2, 3, 4, 1).reshape(n_phys, depth, A_WIDTH, page)
    tok = lambda c: pl.BlockSpec((s_len, c), lambda b, pt: (b, 0))
    page_specs = [pl.BlockSpec((None, None, A_WIDTH, page),
                               lambda b, pt, j=j: (pt[b, j], layer, 0, 0)) for j in range(n_pages)]
    lc_spec = pl.BlockSpec(lc.shape, lambda b, pt: (0, 0, 0), pipeline_mode=pl.Buffered(1))
    grid_spec = pltpu.PrefetchScalarGridSpec(
        num_scalar_prefetch=1,
        grid=(nb,),
        in_specs=[tok(A_WIDTH), tok(A_WIDTH), tok(A_WIDTH), tok(LANES),
                  pl.BlockSpec((None, SM_ROWS, s_len), lambda b, pt: (b, 0, 0))]
                 + page_specs + page_specs + [lc_spec],
        out_specs=tok(A_WIDTH),
        scratch_shapes=[pltpu.VMEM((A_HEADS * s_len, n_pages * LANES), F32)],
    )
    return pl.pallas_call(
        functools.partial(_fox_sample_kernel, n_pages=n_pages, layer=layer, s_len=s_len),
        grid_spec=grid_spec,
        out_shape=jax.ShapeDtypeStruct((nb * s_len, A_WIDTH), BF16),
        compiler_params=_cparams("arbitrary"),
        name="fox_sample",
    )(page_table, q, k_new, v_new, col, rown, *([ck] * n_pages), *([cv] * n_pages), lc)


def _mlstm_kernel(q_ref, k_ref, v_ref, so_ref, col_ref, row_ref, g_ref, c0_ref, n0_ref, m0_ref, _,
                  y_ref, c1_ref, n1_ref, m1_ref, *, t, bb, group):
    nh, hd = M_HEADS, M_HEAD_DIM
    r = lax.broadcasted_iota(jnp.int32, (t, t), 0)
    c = lax.broadcasted_iota(jnp.int32, (t, t), 1)
    causal = c <= r
    g = g_ref[...]
    hs = lambda h: slice(h * hd, (h + 1) * hd)
    last = slice(t - 1, t)
    units = [(u, h) for u in range(group) for h in range(nh)]

    def group_body(gi, _):
        bis = [gi * group + u for u in range(group)]
        r0s = [pl.multiple_of(bi * t, t) for bi in bis]
        qa = [q_ref[pl.ds(r0, t), :].astype(F32) for r0 in r0s]
        ka = [k_ref[pl.ds(r0, t), :].astype(F32) for r0 in r0s]
        va = [v_ref[pl.ds(r0, t), :].astype(F32) for r0 in r0s]
        colv = [col_ref[pl.ds(r0, t), :] for r0 in r0s]
        rows = [row_ref[bi] for bi in bis]
        cs = {(u, h): c0_ref[bis[u], h] for u, h in units}
        ns = {(u, h): n0_ref[bis[u], h] for u, h in units}
        ms = {(u, h): m0_ref[bis[u], h] for u, h in units}
        qk = {(u, h): lax.dot_general(qa[u][:, hs(h)], ka[u][:, hs(h)], NT, preferred_element_type=F32)
              for u, h in units}
        qc = {(u, h): lax.dot_general(qa[u][:, hs(h)], cs[u, h], NT, preferred_element_type=F32) for u, h in units}
        wqk, a_, m_t_, wl_ = {}, {}, {}, {}
        for u, h in units:
            f_col = colv[u][:, SM_MF + h:SM_MF + h + 1]
            i_col = colv[u][:, SM_MI + h:SM_MI + h + 1]
            g_max = colv[u][:, SM_MG + h:SM_MG + h + 1]
            f_row = rows[u][SM_MF + h:SM_MF + h + 1, :]
            i_row = rows[u][SM_MI + h:SM_MI + h + 1, :]
            mx = jnp.maximum(ms[u, h], g_max)
            m_t = f_col + mx
            d = jnp.where(causal, f_col + (i_row - f_row), -jnp.inf)
            wqk[u, h] = jnp.exp(d - m_t) * qk[u, h]
            a_[u, h] = jnp.exp(ms[u, h] - mx)
            m_t_[u, h] = m_t
            wl_[u, h] = jnp.exp((i_col - f_col) + (f_col[last, :] - m_t[last, :]))
        pv = {(u, h): _dot(wqk[u, h], va[u][:, hs(h)]) for u, h in units}
        ys = {}
        for u, h in units:
            a, m_t, w_last = a_[u, h], m_t_[u, h], wl_[u, h]
            num = a * qc[u, h] + pv[u, h]
            den = (a * jnp.sum(qa[u][:, hs(h)] * ns[u, h], axis=1, keepdims=True)
                   + jnp.sum(wqk[u, h], axis=1, keepdims=True))
            h_t = num / jnp.maximum(jnp.abs(den), jnp.exp(-m_t))
            ys[u, h] = h_t * lax.rsqrt(jnp.mean(h_t * h_t, axis=1, keepdims=True) + EPS)
            a_last = a[last, :]
            vw = va[u][:, hs(h)] * w_last
            c1_ref[bis[u], h] = a_last * cs[u, h] + _dot(vw.T, ka[u][:, hs(h)])
            n1_ref[bis[u], h] = a_last * ns[u, h] + jnp.sum(ka[u][:, hs(h)] * w_last, axis=0, keepdims=True)
            m1_ref[bis[u], h] = m_t[last, :]
        for u in range(group):
            hn = jnp.concatenate([ys[u, h] for h in range(nh)], axis=1) * g
            y_ref[pl.ds(r0s[u], t), :] = (so_ref[pl.ds(r0s[u], t), :] * hn).astype(y_ref.dtype)
        return 0

    lax.fori_loop(0, bb // group, group_body, 0)


def _mlstm(q, k, v, so, col, rowt, g, c0, n0, m0, layer_in, c_prev, layer_out, depth, nb, t, bb):
    n = q.shape[0]
    nh, hd = M_HEADS, M_HEAD_DIM
    seq = lambda w: pl.BlockSpec((bb * t, w), lambda i: (i, 0))
    st_in = lambda a, b_: pl.BlockSpec((bb, None, nh, a, b_), lambda i: (i, layer_in, 0, 0, 0))
    st_out = lambda a, b_: pl.BlockSpec((bb, nh, a, b_), lambda i: (i, 0, 0, 0))
    if c_prev is None:
        c_prev = jnp.zeros((SUBLANES, LANES), F32)
        aliases = {}
    else:
        aliases = {10: 1}
    return pl.pallas_call(
        functools.partial(_mlstm_kernel, t=t, bb=bb, group=2 if bb % 2 == 0 else 1),
        grid=(nb // bb,),
        in_specs=[seq(M_WIDTH), seq(M_WIDTH), seq(M_WIDTH), seq(M_WIDTH), seq(LANES),
                  pl.BlockSpec((bb, SM_ROWS, t), lambda i: (i, 0, 0)),
                  _const_spec((1, M_WIDTH)),
                  st_in(hd, hd), st_in(1, hd), st_in(1, 1),
                  pl.BlockSpec(memory_space=pl.ANY)],
        out_specs=[seq(M_WIDTH),
                   pl.BlockSpec((bb, None, nh, hd, hd), lambda i: (i, layer_out, 0, 0, 0)),
                   st_out(1, hd), st_out(1, 1)],
        out_shape=[jax.ShapeDtypeStruct((n, M_WIDTH), BF16),
                   jax.ShapeDtypeStruct((nb, depth, nh, hd, hd), F32),
                   jax.ShapeDtypeStruct((nb, nh, 1, hd), F32),
                   jax.ShapeDtypeStruct((nb, nh, 1, 1), F32)],
        input_output_aliases=aliases,
        compiler_params=_cparams("parallel"),
        name="mlstm",
    )(q, k, v, so, col, rowt, g, c0, n0.reshape(n0.shape[:3] + (1, hd)), m0.reshape(m0.shape[:3] + (1, 1)), c_prev)


C_EXT = M_HEAD_DIM + 16


def _mlstm_t_kernel(q_ref, k_ref, vt_ref, so_ref, col_ref, row_ref, g_ref, _,
                    y_ref, c1_ref, n1_ref, m1_ref, *, t, chunk):
    nh, hd = M_HEADS, M_HEAD_DIM
    s_idx = lax.broadcasted_iota(jnp.int32, (chunk, chunk), 0)
    l_idx = lax.broadcasted_iota(jnp.int32, (chunk, chunk), 1)
    causal = s_idx <= l_idx
    heads = range(nh)
    hs = lambda h: slice(h * hd, (h + 1) * hd)
    last = slice(chunk - 1, chunk)
    pad_rows = jnp.zeros((C_EXT - hd - 1, chunk), F32)

    def chunk_body(ci, carry):
        cs, ms = carry
        r0 = pl.multiple_of(ci * chunk, chunk)
        qa = q_ref[pl.ds(r0, chunk), :]
        ka = k_ref[pl.ds(r0, chunk), :]
        vta = vt_ref[:, pl.ds(r0, chunk)]
        colv = col_ref[pl.ds(r0, chunk), :]
        rows = row_ref[:, pl.ds(r0, chunk)]
        qk = [lax.dot_general(ka[:, hs(h)], qa[:, hs(h)], NT, preferred_element_type=F32) for h in heads]
        cq = [lax.dot_general(cs[h].astype(BF16), qa[:, hs(h)], NT, preferred_element_type=F32) for h in heads]
        wqk, a_, m_t_, wl_ = [], [], [], []
        for h in heads:
            f_row = rows[SM_MF + h:SM_MF + h + 1, :]
            i_row = rows[SM_MI + h:SM_MI + h + 1, :]
            g_max = rows[SM_MG + h:SM_MG + h + 1, :]
            g_col = colv[:, SM_MI + h:SM_MI + h + 1] - colv[:, SM_MF + h:SM_MF + h + 1]
            mx = jnp.maximum(ms[h], g_max)
            m_t = f_row + mx
            d = jnp.where(causal, f_row + g_col, -jnp.inf)
            wqk.append(jnp.exp(d - m_t) * qk[h])
            a_.append(jnp.exp(ms[h] - mx))
            m_t_.append(m_t)
            wl_.append(jnp.exp((i_row - f_row) + (f_row[:, last] - m_t[:, last])))
        pv = [_dot(vta[hs(h), :], wqk[h].astype(BF16)) for h in heads]
        ys, cs_new, ms_new = [], [], []
        for h in heads:
            a, m_t = a_[h], m_t_[h]
            num = a * cq[h][0:hd, :] + pv[h]
            den = a * cq[h][hd:hd + 1, :] + jnp.sum(wqk[h], axis=0, keepdims=True)
            h_t = num / jnp.maximum(jnp.abs(den), jnp.exp(-m_t))
            ys.append((h_t * lax.rsqrt(jnp.mean(h_t * h_t, axis=0, keepdims=True) + EPS)).T)
            vw = jnp.concatenate([vta[hs(h), :].astype(F32) * wl_[h], wl_[h], pad_rows], axis=0)
            cs_new.append(a[:, last] * cs[h] + _dot(vw.astype(BF16), ka[:, hs(h)]))
            ms_new.append(m_t[:, last])
        hn = jnp.concatenate(ys, axis=1) * g_ref[...]
        y_ref[pl.ds(r0, chunk), :] = (so_ref[pl.ds(r0, chunk), :] * hn).astype(y_ref.dtype)
        return cs_new, ms_new

    init = ([jnp.zeros((C_EXT, hd), F32)] * nh, [jnp.zeros((1, 1), F32)] * nh)
    cs, ms = lax.fori_loop(0, t // chunk, chunk_body, init, unroll=True)
    for h in heads:
        c1_ref[h] = cs[h][0:hd, :]
        n1_ref[h] = cs[h][hd:hd + 1, :]
        m1_ref[h] = ms[h]


def _mlstm_t(q, k, vt, so, col, rowt, g, c_prev, layer_out, depth, nb, t, chunk):
    n = q.shape[0]
    nh, hd = M_HEADS, M_HEAD_DIM
    seq = lambda w: pl.BlockSpec((t, w), lambda i: (i, 0))
    st_out = lambda a, b_: pl.BlockSpec((None, nh, a, b_), lambda i: (i, 0, 0, 0))
    if c_prev is None:
        c_prev = jnp.zeros((SUBLANES, LANES), F32)
        aliases = {}
    else:
        aliases = {7: 1}
    return pl.pallas_call(
        functools.partial(_mlstm_t_kernel, t=t, chunk=chunk),
        grid=(nb,),
        in_specs=[seq(M_WIDTH), seq(M_WIDTH), pl.BlockSpec((None, M_WIDTH, t), lambda i: (i, 0, 0)),
                  seq(M_WIDTH), seq(LANES), pl.BlockSpec((None, SM_ROWS, t), lambda i: (i, 0, 0)),
                  _const_spec((1, M_WIDTH)), pl.BlockSpec(memory_space=pl.ANY)],
        out_specs=[seq(M_WIDTH),
                   pl.BlockSpec((None, None, nh, hd, hd), lambda i: (i, layer_out, 0, 0, 0)),
                   st_out(1, hd), st_out(1, 1)],
        out_shape=[jax.ShapeDtypeStruct((n, M_WIDTH), BF16),
                   jax.ShapeDtypeStruct((nb, depth, nh, hd, hd), F32),
                   jax.ShapeDtypeStruct((nb, nh, 1, hd), F32),
                   jax.ShapeDtypeStruct((nb, nh, 1, 1), F32)],
        input_output_aliases=aliases,
        compiler_params=_cparams("parallel"),
        name="mlstm_t",
    )(q, k, vt, so, col, rowt, g, c_prev)


def _conv_carry_init(state_ref, carry_ref, tiles_per_seq):
    @pl.when((pl.program_id(0) % tiles_per_seq) == 0)
    def _():
        carry_ref[...] = state_ref[...]


def _conv_long(u, w_ref, c0, carry_ref):
    tm, ch = u.shape
    prev = carry_ref[:, c0:c0 + ch]
    row = lax.broadcasted_iota(jnp.int32, (tm, 1), 0)
    p1 = jnp.where(row == 0, prev[1:2, :], pltpu.roll(u, 1, axis=0))
    p2 = jnp.where(row == 0, prev[0:1, :], jnp.where(row == 1, prev[1:2, :], pltpu.roll(u, 2, axis=0)))
    carry_ref[:, c0:c0 + ch] = u[tm - 2:tm, :]
    return p2 * w_ref[0:1, c0:c0 + ch] + p1 * w_ref[1:2, c0:c0 + ch] + u * w_ref[2:3, c0:c0 + ch]


def _conv_short(u, w_ref, c0, fill_ref, s_len):
    tm, ch = u.shape
    fill = fill_ref[:, c0:c0 + ch]
    rmod = lax.broadcasted_iota(jnp.int32, (tm, 1), 0) & (s_len - 1)
    p1 = jnp.where(rmod >= 1, pltpu.roll(u, 1, axis=0), pltpu.roll(fill, tm - 1, axis=0))
    p2 = jnp.where(rmod >= 2, pltpu.roll(u, 2, axis=0), fill)
    return p2 * w_ref[0:1, c0:c0 + ch] + p1 * w_ref[1:2, c0:c0 + ch] + u * w_ref[2:3, c0:c0 + ch]


def _merge_kernel(x_ref, g_ref, wc_ref, bc_ref, wg_ref, bg_ref, ws_ref, ya_ref, ym_ref,
                  woa_ref, wom_ref, woc_ref, wo_ref, st_ref, x1_ref, cst_ref, *scratch,
                  long_seq, tiles_per_seq, s_len):
    x = x_ref[...]
    h = _rms(x, g_ref[...]).astype(BF16)
    cw = C_WIDTH
    cb = _dot(h, wc_ref[:, 0:cw]) + bc_ref[:, 0:cw]
    u = (_dot(h, wc_ref[:, cw:2 * cw]) + bc_ref[:, cw:2 * cw]) * (_dot(h, wc_ref[:, 2 * cw:3 * cw]) + bc_ref[:, 2 * cw:3 * cw])
    if long_seq:
        _conv_carry_init(st_ref, scratch[0], tiles_per_seq)
        uc = _conv_long(u, ws_ref, 0, scratch[0])
        cst_ref[...] = u[u.shape[0] - 2:, :]
    else:
        uc = _conv_short(u, ws_ref, 0, st_ref, s_len)
        cst_ref[...] = u
    yc = (cb * uc).astype(BF16)
    d = D_MODEL

    def gate(i):
        return _sigmoid(_dot(h, wg_ref[:, i * d:(i + 1) * d]) + bg_ref[:, i * d:(i + 1) * d])

    merged = gate(0) * _dot(ya_ref[...], woa_ref[...])
    merged = merged + gate(1) * _dot(ym_ref[...], wom_ref[...])
    merged = merged + gate(2) * _dot(yc, woc_ref[...])
    x1_ref[...] = x + _dot(merged.astype(BF16), wo_ref[...])


def _merge(x, g, wc, bc, wg, bg, ws, ya, ym, woa, wom, woc, wo, st, nb, t, tm):
    n = x.shape[0]
    long_seq = t >= tm
    tiles_per_seq = max(t // tm, 1)
    row = lambda c: pl.BlockSpec((tm, c), lambda i: (i, 0))
    if long_seq:
        st_spec = pl.BlockSpec((None, 2, C_WIDTH), lambda i: (i // tiles_per_seq, 0, 0))
        cst_spec = pl.BlockSpec((None, 2, C_WIDTH), lambda i: (i // tiles_per_seq, 0, 0))
        cst_shape = jax.ShapeDtypeStruct((nb, 2, C_WIDTH), F32)
        scratch = [pltpu.VMEM((2, C_WIDTH), F32)]
    else:
        st_spec = row(C_WIDTH)
        cst_spec = row(C_WIDTH)
        cst_shape = jax.ShapeDtypeStruct((n, C_WIDTH), F32)
        scratch = []
    return pl.pallas_call(
        functools.partial(_merge_kernel, long_seq=long_seq, tiles_per_seq=tiles_per_seq, s_len=t),
        grid=(n // tm,),
        in_specs=[row(D_MODEL), _const_spec((1, D_MODEL)),
                  _const_spec(wc.shape), _const_spec(bc.shape), _const_spec(wg.shape), _const_spec(bg.shape),
                  _const_spec(ws.shape), row(A_WIDTH), row(M_WIDTH),
                  _const_spec(woa.shape), _const_spec(wom.shape), _const_spec(woc.shape), _const_spec(wo.shape),
                  st_spec],
        out_specs=[row(D_MODEL), cst_spec],
        out_shape=[jax.ShapeDtypeStruct((n, D_MODEL), F32), cst_shape],
        scratch_shapes=scratch,
        compiler_params=_cparams("arbitrary"),
        name="merge",
    )(x, g, wc, bc, wg, bg, ws, ya, ym, woa, wom, woc, wo, st)


FF_CHUNK = D_FF // 2


def _ffn_kernel(x_ref, pe_ref, gf_ref, wup_ref, wfc_ref, bfc_ref, wdn_ref, gp_ref, wpg_ref, wpp_ref, gfin_ref,
                st_ref, xo_ref, fst_ref, *scratch, long_seq, tiles_per_seq, s_len, final):
    x = x_ref[...]
    tm = x.shape[0]
    h2 = _rms(x, gf_ref[...]).astype(BF16)
    acc = jnp.zeros((tm, D_MODEL), F32)
    if long_seq:
        _conv_carry_init(st_ref, scratch[0], tiles_per_seq)
    for half in range(D_FF // FF_CHUNK):
        c0 = half * FF_CHUNK
        ua = _dot(h2, wup_ref[:, c0:c0 + FF_CHUNK])
        ub = _dot(h2, wup_ref[:, D_FF + c0:D_FF + c0 + FF_CHUNK])
        if long_seq:
            uac = _conv_long(ua, wfc_ref, c0, scratch[0])
            fst_ref[:, c0:c0 + FF_CHUNK] = ua[tm - 2:, :]
        else:
            uac = _conv_short(ua, wfc_ref, c0, st_ref, s_len)
            fst_ref[:, c0:c0 + FF_CHUNK] = ua
        z = uac + bfc_ref[:, c0:c0 + FF_CHUNK]
        act = (z * _sigmoid(z) * ub).astype(BF16)
        acc = acc + _dot(act, wdn_ref[c0:c0 + FF_CHUNK, :])
    x2 = x + acc
    gate = _sigmoid(_dot(_rms(x2, gp_ref[...]).astype(BF16), wpg_ref[...]))
    x3 = x2 + gate * _dot(pe_ref[...].astype(BF16), wpp_ref[...])
    if final:
        x3 = _rms(x3, gfin_ref[...])
    xo_ref[...] = x3


def _ffn(x, pe, layer, gf, wup, wfc, bfc, wdn, gp, wpg, wpp, gfin, st, nb, t, tm, final):
    n = x.shape[0]
    long_seq = t >= tm
    tiles_per_seq = max(t // tm, 1)
    row = lambda c: pl.BlockSpec((tm, c), lambda i: (i, 0))
    if long_seq:
        st_spec = pl.BlockSpec((None, 2, D_FF), lambda i: (i // tiles_per_seq, 0, 0))
        fst_spec = pl.BlockSpec((None, 2, D_FF), lambda i: (i // tiles_per_seq, 0, 0))
        fst_shape = jax.ShapeDtypeStruct((nb, 2, D_FF), F32)
        scratch = [pltpu.VMEM((2, D_FF), F32)]
    else:
        st_spec = row(D_FF)
        fst_spec = row(D_FF)
        fst_shape = jax.ShapeDtypeStruct((n, D_FF), F32)
        scratch = []
    return pl.pallas_call(
        functools.partial(_ffn_kernel, long_seq=long_seq, tiles_per_seq=tiles_per_seq, s_len=t, final=final),
        grid=(n // tm,),
        in_specs=[row(D_MODEL), pl.BlockSpec((None, tm, pe.shape[2]), lambda i: (layer, i, 0)),
                  _const_spec((1, D_MODEL)),
                  _const_spec(wup.shape), _const_spec(wfc.shape), _const_spec(bfc.shape), _const_spec(wdn.shape),
                  _const_spec((1, D_MODEL)), _const_spec(wpg.shape), _const_spec(wpp.shape),
                  _const_spec((1, D_MODEL)), st_spec],
        out_specs=[row(D_MODEL), fst_spec],
        out_shape=[jax.ShapeDtypeStruct((n, D_MODEL), F32), fst_shape],
        scratch_shapes=scratch,
        compiler_params=_cparams("arbitrary"),
        name="ffn",
    )(x, pe, gf, wup, wfc, bfc, wdn, gp, wpg, wpp, gfin, st)


def _row_form(col, nb, t):
    return col[:, :SM_ROWS].reshape(nb, t, SM_ROWS).transpose(0, 2, 1)


def _short_fill(state, s_len):
    nb, _, ch = state.shape
    return jnp.pad(state, ((0, 0), (0, s_len - 2), (0, 0))).reshape(nb * s_len, ch)


def _pick(n, pref):
    return pref if n % pref == 0 else n


def _layer(x, pe, w, nb, t, chunk, past, final, carry):
    n = x.shape[0]
    layer, depth = carry["layer"], carry["depth"]
    tm_in = _pick(n, 512)
    tm_merge = _pick(n, 512)
    tm_ffn = _pick(n, 256)
    if past is None:
        kvp = dict(layer=layer, depth=depth, prev=carry["kvt"], nb=nb, t=t, wt=w["w_kvt"], bt=w["b_kvt"])
        q, k, mq, mk, so, sm, kt, vt, mvt = _inproj(x, w["g_mix"], w["w_am"], w["b_am"], tm_in, kvp)
        col, logf = _gates(sm, t, t, chunk)
        rowt = _row_form(col, nb, t)
        ya = _fox_prompt(q, k, vt, col, layer, nb, t, _pick(t, 512))
        k, v = kt, vt
        ym, c1, n1, m1 = _mlstm_t(mq, mk, mvt, so, col, rowt, w["g_mhead"], carry["c"], layer, depth, nb, t, chunk)
        conv0 = jnp.zeros((nb, 2, C_WIDTH), F32)
        ffn0 = jnp.zeros((nb, 2, D_FF), F32)
    else:
        q, k, v, mq, mk, mv, so, sm = _inproj(x, w["g_mix"], w["w_am"], w["b_am"], tm_in)
        col, logf = _gates(sm, n, t, t)
        rowt = _row_form(col, nb, t)
        ya = _fox_sample(past["page_table"], q, k, v, col, rowt, past["cache_k"], past["cache_v"],
                         past["lc"], layer, nb, t)
        conv0, ffn0 = past["conv0"], past["ffn0"]
        ym, c1, n1, m1 = _mlstm(mq, mk, mv, so, col, rowt, w["g_mhead"], past["c0"], past["n0"], past["m0"], layer,
                                carry["c"], layer, depth, nb, t, _pick(nb, SUBLANES))
    long_merge = t >= tm_merge
    long_ffn = t >= tm_ffn
    x1, conv1 = _merge(x, w["g_mix"], w["w_c"], w["b_c"], w["w_g"], w["b_g"], w["w_sconv"], ya, ym,
                       w["w_oa"], w["w_om"], w["w_oc"], w["w_o"],
                       conv0 if long_merge else _short_fill(conv0, t), nb, t, tm_merge)
    x3, ffn1 = _ffn(x1, pe, layer, w["g_ffn"], w["w_up"], w["w_fconv"], w["b_fconv"], w["w_down"],
                    w["g_ple"], w["w_ple_gate"], w["w_ple"], w["g_final"],
                    ffn0 if long_ffn else _short_fill(ffn0, t), nb, t, tm_ffn, final)
    if not long_merge:
        conv1 = conv1.reshape(nb, t, C_WIDTH)[:, t - 2:, :]
    if not long_ffn:
        ffn1 = ffn1.reshape(nb, t, D_FF)[:, t - 2:, :]
    if past is not None:
        k = k.reshape(nb, t, A_HEADS, A_HEAD_DIM)
        v = v.reshape(nb, t, A_HEADS, A_HEAD_DIM)
    state = (k, v, logf.reshape(nb, t, A_HEADS), c1, n1.reshape(nb, M_HEADS, M_HEAD_DIM),
             m1.reshape(nb, M_HEADS), conv1, ffn1)
    return x3, state


def _layer_weights(l, w_in, b_in, g_mix, g_mhead, w_sconv, w_oa, w_om, w_oc, w_o, g_ffn, w_up,
                   w_fconv, b_fconv, w_down, g_ple, w_ple_gate, w_ple, g_final):
    a0 = 0
    af0 = 3 * A_WIDTH
    m0 = af0 + A_HEADS
    mi0 = m0 + 3 * M_WIDTH
    mo0 = mi0 + 2 * M_HEADS
    c0 = mo0 + M_WIDTH
    g0 = c0 + 3 * C_WIDTH
    wl, bl = w_in[l], b_in[l]
    pad = LANES - SM_USED

    def cols(a, lo, hi):
        return a[..., lo:hi]

    def am(a):
        return jnp.concatenate(
            [cols(a, a0, af0), cols(a, m0, mi0), cols(a, mo0, c0),
             cols(a, af0, m0), cols(a, mi0, mo0), jnp.zeros(a.shape[:-1] + (pad,), a.dtype)], axis=-1)

    row = lambda a: a.reshape(1, -1)
    return {
        "w_am": am(wl).astype(BF16), "b_am": row(am(bl)),
        "w_kvt": jnp.concatenate([cols(wl, A_WIDTH, af0), cols(wl, m0 + 2 * M_WIDTH, mi0)], axis=-1).T.astype(BF16),
        "b_kvt": jnp.concatenate([cols(bl, A_WIDTH, af0), cols(bl, m0 + 2 * M_WIDTH, mi0)], axis=-1).reshape(-1, 1),
        "w_c": cols(wl, c0, g0).astype(BF16), "b_c": row(cols(bl, c0, g0)),
        "w_g": cols(wl, g0, g0 + 3 * D_MODEL).astype(BF16), "b_g": row(cols(bl, g0, g0 + 3 * D_MODEL)),
        "g_mix": row(g_mix[l]), "g_mhead": row(g_mhead[l]), "w_sconv": w_sconv[l],
        "w_oa": w_oa[l].astype(BF16), "w_om": w_om[l].astype(BF16), "w_oc": w_oc[l].astype(BF16),
        "w_o": w_o[l].astype(BF16), "g_ffn": row(g_ffn[l]), "w_up": w_up[l].astype(BF16),
        "w_fconv": w_fconv[l], "b_fconv": row(b_fconv[l]), "w_down": w_down[l].astype(BF16),
        "g_ple": row(g_ple[l]), "w_ple_gate": w_ple_gate[l].astype(BF16), "w_ple": w_ple[l].astype(BF16),
        "g_final": row(g_final),
    }


def kernel(x_prompt, x_sample, cache_k, cache_v, cache_logf, state_mlstm_C, state_mlstm_n, state_mlstm_m, state_conv, state_ffn_conv, page_table, p_prompt, p_sample, w_in, b_in, g_mix, g_mhead, w_sconv, w_oa, w_om, w_oc, w_o, g_ffn, w_up, w_fconv, b_fconv, w_down, g_ple, w_ple_gate, w_ple, g_final):
    nbp, tp, _ = x_prompt.shape
    nbs, ts, _ = x_sample.shape
    depth = w_in.shape[0]
    n_phys, page = cache_logf.shape[0], cache_logf.shape[1]
    assert page == LANES and ts == SUBLANES and depth * A_HEADS == SM_USED

    lc = cache_logf.reshape(n_phys, page, depth * A_HEADS).transpose(0, 2, 1)

    xp = x_prompt.reshape(nbp * tp, D_MODEL)
    xs = x_sample.reshape(nbs * ts, D_MODEL)
    chunk_p = _pick(tp, 256)
    new_p, new_s = [], []
    kvt = c_p = c_s = None
    for l in range(depth):
        w = _layer_weights(l, w_in, b_in, g_mix, g_mhead, w_sconv, w_oa, w_om, w_oc, w_o, g_ffn, w_up,
                           w_fconv, b_fconv, w_down, g_ple, w_ple_gate, w_ple, g_final)
        final = l == depth - 1
        xp, st_p = _layer(xp, p_prompt.reshape(depth, nbp * tp, -1), w, nbp, tp, chunk_p, None, final,
                          {"layer": l, "depth": depth, "kvt": kvt, "c": c_p})
        kvt, c_p = st_p[:2], st_p[3]
        past = {"page_table": page_table, "cache_k": cache_k, "cache_v": cache_v, "lc": lc,
                "c0": state_mlstm_C, "n0": state_mlstm_n, "m0": state_mlstm_m,
                "conv0": state_conv[:, l], "ffn0": state_ffn_conv[:, l]}
        xs, st_s = _layer(xs, p_sample.reshape(depth, nbs * ts, -1), w, nbs, ts, ts, past, final,
                          {"layer": l, "depth": depth, "kvt": None, "c": c_s})
        c_s = st_s[3]
        new_p.append(st_p)
        new_s.append(st_s)

    def stack(per_layer, c_all, first):
        rows = [jnp.stack([s[i] for s in per_layer], axis=2) for i in range(first, 3)]
        states = [jnp.stack([s[i] for s in per_layer], axis=1) for i in range(4, 8)]
        return tuple(rows + [c_all] + states)

    kv_p = tuple(a.reshape(nbp, depth, A_HEADS, A_HEAD_DIM, tp).transpose(0, 4, 1, 2, 3) for a in kvt)
    return ((xp.reshape(nbp, tp, D_MODEL), xs.reshape(nbs, ts, D_MODEL)) + kv_p + stack(new_p, c_p, 2)
            + stack(new_s, c_s, 0))
```

```python
import functools

import jax
import jax.numpy as jnp
from jax import lax
from jax.experimental import pallas as pl
from jax.experimental.pallas import tpu as pltpu

F32 = jnp.float32
BF16 = jnp.bfloat16

D_MODEL = 1024
A_HEADS = 8
A_HEAD_DIM = 64
A_WIDTH = A_HEADS * A_HEAD_DIM
M_HEADS = 4
M_HEAD_DIM = 128
M_WIDTH = M_HEADS * M_HEAD_DIM
C_WIDTH = 512
D_FF = 2816
EPS = 1e-6
LANES = 128
SUBLANES = 8
VMEM_LIMIT = 56 * 1024 * 1024

SM_AF = 0
SM_MI = 8
SM_MF = 12
SM_USED = 16
SM_MG = 16
SM_ROWS = 24

NT = (((1,), (1,)), ((), ()))
LOG2E = 1.4426950408889634


def _cparams(*sem):
    return pltpu.CompilerParams(dimension_semantics=sem, vmem_limit_bytes=VMEM_LIMIT)


def _const_spec(shape):
    nd = len(shape)
    return pl.BlockSpec(shape, lambda *_: (0,) * nd, pipeline_mode=pl.Buffered(1))


def _rms(x, g):
    return x * lax.rsqrt(jnp.mean(x * x, axis=-1, keepdims=True) + EPS) * g


def _log_sigmoid(x):
    return jnp.minimum(x, 0.0) - jnp.log1p(jnp.exp(-jnp.abs(x)))


def _sigmoid(x):
    return 1.0 / (1.0 + jnp.exp(-x))


def _dot(a, b):
    return jnp.dot(a, b, preferred_element_type=F32)


def _inproj_kernel(*refs, transposed_kv):
    if transposed_kv:
        (x_ref, g_ref, w_ref, b_ref, wt_ref, bt_ref, _, _,
         q_ref, k_ref, mq_ref, mk_ref, so_ref, sm_ref, kt_ref, vt_ref, mvt_ref) = refs
    else:
        x_ref, g_ref, w_ref, b_ref, q_ref, k_ref, v_ref, mq_ref, mk_ref, mv_ref, so_ref, sm_ref = refs
    h = _rms(x_ref[...], g_ref[...]).astype(BF16)

    def proj(c0, n):
        return _dot(h, w_ref[:, c0:c0 + n]) + b_ref[:, c0:c0 + n]

    q_scale = A_HEAD_DIM ** -0.5 * (LOG2E if transposed_kv else 1.0)
    q_ref[...] = (proj(0, A_WIDTH) * q_scale).astype(BF16)
    if transposed_kv:
        k_ref[...] = proj(A_WIDTH, A_WIDTH).astype(BF16)
        kt_ref[...] = (lax.dot_general(wt_ref[0:A_WIDTH, :], h, NT, preferred_element_type=F32)
                       + bt_ref[0:A_WIDTH, :])
        vt_ref[...] = (lax.dot_general(wt_ref[A_WIDTH:2 * A_WIDTH, :], h, NT, preferred_element_type=F32)
                       + bt_ref[A_WIDTH:2 * A_WIDTH, :])
        mvt_ref[...] = (lax.dot_general(wt_ref[2 * A_WIDTH:, :], h, NT, preferred_element_type=F32)
                        + bt_ref[2 * A_WIDTH:, :]).astype(BF16)
    else:
        k_ref[...] = proj(A_WIDTH, A_WIDTH)
        v_ref[...] = proj(2 * A_WIDTH, A_WIDTH)
    o = 3 * A_WIDTH
    mq_ref[...] = proj(o, M_WIDTH).astype(BF16)
    mk_ref[...] = (proj(o + M_WIDTH, M_WIDTH) * (M_HEAD_DIM ** -0.5)).astype(BF16)
    if not transposed_kv:
        mv_ref[...] = proj(o + 2 * M_WIDTH, M_WIDTH).astype(BF16)
    so_ref[...] = _sigmoid(proj(o + 3 * M_WIDTH, M_WIDTH))
    sm_ref[...] = proj(o + 4 * M_WIDTH, LANES)


def _inproj(x, g, w, b, tm, kv=None):
    n = x.shape[0]
    wcols = w.shape[1]
    row = lambda c: pl.BlockSpec((tm, c), lambda i: (i, 0))
    common = [
        jax.ShapeDtypeStruct((n, M_WIDTH), BF16),
        jax.ShapeDtypeStruct((n, M_WIDTH), BF16),
        jax.ShapeDtypeStruct((n, M_WIDTH), BF16),
        jax.ShapeDtypeStruct((n, M_WIDTH), F32),
        jax.ShapeDtypeStruct((n, LANES), F32),
    ]
    common_specs = [row(M_WIDTH)] * 4 + [row(LANES)]
    in_specs = [row(D_MODEL), _const_spec((1, D_MODEL)), _const_spec((D_MODEL, wcols)), _const_spec((1, wcols))]
    if kv is None:
        return pl.pallas_call(
            functools.partial(_inproj_kernel, transposed_kv=False),
            grid=(n // tm,),
            in_specs=in_specs,
            out_specs=[row(A_WIDTH)] * 3 + common_specs,
            out_shape=[jax.ShapeDtypeStruct((n, A_WIDTH), BF16), jax.ShapeDtypeStruct((n, A_WIDTH), F32),
                       jax.ShapeDtypeStruct((n, A_WIDTH), F32)] + common,
            compiler_params=_cparams("parallel"),
            name="inproj",
        )(x, g, w, b)
    nb, t, depth, layer = kv["nb"], kv["t"], kv["depth"], kv["layer"]
    tps = t // tm
    t_shape = jax.ShapeDtypeStruct((nb, depth, A_WIDTH, t), F32)
    t_spec = pl.BlockSpec((None, None, A_WIDTH, tm), lambda i: (i // tps, layer, 0, i % tps))
    prev = kv["prev"]
    if prev is None:
        prev = (jnp.zeros((SUBLANES, LANES), F32),) * 2
        aliases = {}
    else:
        aliases = {6: 6, 7: 7}
    any_spec = pl.BlockSpec(memory_space=pl.ANY)
    no_mv = lambda items: items[:2] + items[3:]
    return pl.pallas_call(
        functools.partial(_inproj_kernel, transposed_kv=True),
        grid=(n // tm,),
        in_specs=in_specs + [_const_spec(kv["wt"].shape), _const_spec(kv["bt"].shape), any_spec, any_spec],
        out_specs=[row(A_WIDTH)] * 2 + no_mv(common_specs) + [t_spec, t_spec,
                   pl.BlockSpec((None, M_WIDTH, tm), lambda i: (i // tps, 0, i % tps))],
        out_shape=[jax.ShapeDtypeStruct((n, A_WIDTH), BF16), jax.ShapeDtypeStruct((n, A_WIDTH), BF16)]
                  + no_mv(common) + [t_shape, t_shape, jax.ShapeDtypeStruct((nb, M_WIDTH, t), BF16)],
        input_output_aliases=aliases,
        compiler_params=_cparams("parallel"),
        name="inproj_t",
    )(x, g, w, b, kv["wt"], kv["bt"], *prev)


def _gates_kernel(sm_ref, col_ref, logf_ref, *, seq_a, seg_m):
    x = sm_ref[...]
    rows = x.shape[0]
    lane = lax.broadcasted_iota(jnp.int32, (1, LANES), 1)
    row = lax.broadcasted_iota(jnp.int32, (rows, 1), 0)
    ls = _log_sigmoid(x)
    logf_ref[...] = ls[:, SM_AF:SM_AF + A_HEADS]
    is_a = lane < SM_MI
    is_i = (lane >= SM_MI) & (lane < SM_MF)
    is_f = (lane >= SM_MF) & (lane < SM_USED)
    y = jnp.where(is_i, x, ls)
    rmod_a = row & (seq_a - 1)
    rmod_m = row & (seg_m - 1)
    s = 1
    while s < max(seq_a, seg_m):
        take = jnp.zeros((rows, LANES), jnp.bool_)
        if s < seq_a:
            take = take | (is_a & (rmod_a >= s))
        if s < seg_m:
            take = take | (is_f & (rmod_m >= s))
        y = y + jnp.where(take, pltpu.roll(y, s, axis=0), 0.0)
        s *= 2
    nm = SM_MF - SM_MI
    is_g = (lane >= SM_MG) & (lane < SM_MG + nm)
    g = pltpu.roll(y, SM_MG - SM_MI, axis=1) - pltpu.roll(y, SM_MG - SM_MF, axis=1)
    s = 1
    while s < seg_m:
        g = jnp.where(is_g & (rmod_m >= s), jnp.maximum(g, pltpu.roll(g, s, axis=0)), g)
        s *= 2
    col_ref[...] = jnp.where(is_g, g, y)


def _gates(sm, rows, seq_a, seg_m):
    n = sm.shape[0]
    return pl.pallas_call(
        functools.partial(_gates_kernel, seq_a=seq_a, seg_m=seg_m),
        grid=(n // rows,),
        in_specs=[pl.BlockSpec((rows, LANES), lambda i: (i, 0))],
        out_specs=[pl.BlockSpec((rows, LANES), lambda i: (i, 0)),
                   pl.BlockSpec((rows, A_HEADS), lambda i: (i, 0))],
        out_shape=[jax.ShapeDtypeStruct((n, LANES), F32),
                   jax.ShapeDtypeStruct((n, A_HEADS), F32)],
        compiler_params=_cparams("parallel"),
        name="gates",
    )(sm)


V_EXT = A_HEAD_DIM + 16


def _fox_kernel(q_ref, k_ref, vt_ref, col_ref, o_ref, vb_ref, ka_ref, sa_ref, sb_ref, pa_ref, pb_ref, *, tq, t):
    hp = pl.program_id(1)
    lane = lax.broadcasted_iota(jnp.int32, (1, LANES), 1)
    hd = A_HEAD_DIM
    aug0 = [hd * (1 - hh) for hh in range(2)]

    ones_rows = jnp.ones((V_EXT - hd, t), BF16)
    for hh in range(2):
        vb_ref[hh, 0:hd, :] = vt_ref[hh * hd:(hh + 1) * hd, :].astype(BF16)
        vb_ref[hh, hd:V_EXT, :] = ones_rows
    colv = col_ref[...]
    kf = k_ref[...]
    for hh in range(2):
        ck = jnp.sum(jnp.where(lane == SM_AF + hp * 2 + hh, colv, 0.0), axis=1, keepdims=True) * (-LOG2E)
        c1 = ck.astype(BF16)
        r1 = ck - c1.astype(F32)
        c2 = r1.astype(BF16)
        c3 = (r1 - c2.astype(F32)).astype(BF16)
        a0 = aug0[hh]
        ka_ref[hh] = jnp.where(lane == a0, c1, jnp.where(lane == a0 + 1, c2, jnp.where(lane == a0 + 2, c3, kf)))

    slots = ((sa_ref, pa_ref), (sb_ref, pb_ref))
    kpos = lax.broadcasted_iota(jnp.int32, (tq, tq), 0)
    qpos = lax.broadcasted_iota(jnp.int32, (tq, tq), 1)

    def values(j, p_in):
        return [_dot(vb_ref[hh, :, j * tq:(j + 1) * tq], p_in[hh]) for hh in range(2)]

    for qi in range(t // tq):
        q2 = q_ref[qi * tq:(qi + 1) * tq, :]
        zero = jnp.zeros_like(q2)
        one = jnp.ones_like(q2)
        qm = [jnp.where((lane >= aug0[hh]) & (lane < aug0[hh] + 3), one,
                        jnp.where((lane >= hd) == bool(hh), q2, zero)) for hh in range(2)]

        def scores(j, s_out):
            for hh in range(2):
                s_out[hh] = lax.dot_general(ka_ref[hh, j * tq:(j + 1) * tq, :], qm[hh], NT,
                                            preferred_element_type=F32)

        def softmax(s_in, p_out, ms, diag):
            m_out, alphas = [], []
            for hh in range(2):
                s = s_in[hh]
                if diag:
                    s = jnp.where(kpos <= qpos, s, -jnp.inf)
                m_new = jnp.max(s, axis=0, keepdims=True)
                if ms is not None:
                    m_new = jnp.maximum(ms[hh], m_new)
                    alphas.append(jnp.exp2(ms[hh] - m_new))
                m_out.append(m_new)
                p_out[hh] = jnp.exp2(s - m_new).astype(BF16)
            return m_out, alphas

        scores(0, slots[qi % 2][0])
        ms = a_prev = accs = None
        for j in range(qi):
            (s_cur, p_cur), (s_nxt, p_nxt) = slots[(qi - j) % 2], slots[(qi - j - 1) % 2]
            pv = values(j - 1, p_nxt) if j > 0 else None
            scores(j + 1, s_nxt)
            ms, alphas = softmax(s_cur, p_cur, ms, False)
            if j == 1:
                accs = pv
            elif j > 1:
                accs = [a_prev[hh] * accs[hh] + pv[hh] for hh in range(2)]
            a_prev = alphas
        pv = values(qi - 1, pb_ref) if qi > 0 else None
        ms, alphas = softmax(sa_ref, pa_ref, ms, True)
        acc = values(qi, pa_ref)
        if qi == 1:
            acc = [alphas[hh] * pv[hh] + acc[hh] for hh in range(2)]
        elif qi > 1:
            acc = [alphas[hh] * (a_prev[hh] * accs[hh] + pv[hh]) + acc[hh] for hh in range(2)]
        yt = [acc[hh][0:hd, :] / acc[hh][hd:hd + 1, :] for hh in range(2)]
        o_ref[qi * tq:(qi + 1) * tq, :] = jnp.concatenate(yt, axis=0).T.astype(o_ref.dtype)


def _fox_prompt(q, k, vt, col, layer, nb, t, tq):
    n = q.shape[0]
    nq = t // tq
    pairs = A_WIDTH // LANES
    seq = pl.BlockSpec((t, LANES), lambda b, hp: (b, hp))
    return pl.pallas_call(
        functools.partial(_fox_kernel, tq=tq, t=t),
        grid=(nb, pairs),
        in_specs=[seq, seq,
                  pl.BlockSpec((None, None, LANES, t), lambda b, hp: (b, layer, hp, 0)),
                  pl.BlockSpec((t, LANES), lambda b, hp: (b, 0))],
        out_specs=seq,
        out_shape=jax.ShapeDtypeStruct((n, A_WIDTH), BF16),
        scratch_shapes=[pltpu.VMEM((2, V_EXT, t), BF16), pltpu.VMEM((2, t, LANES), BF16),
                        pltpu.VMEM((2, tq, tq), F32), pltpu.VMEM((2, tq, tq), F32),
                        pltpu.VMEM((2, tq, tq), BF16), pltpu.VMEM((2, tq, tq), BF16)],
        compiler_params=_cparams("parallel", "parallel"),
        name="fox_prompt",
    )(q, k, vt, col)


def _lane_cumsum(y):
    lane = lax.broadcasted_iota(jnp.int32, (1, LANES), 1)
    s = 1
    while s < LANES:
        y = y + jnp.where(lane >= s, pltpu.roll(y, s, axis=1), 0.0)
        s *= 2
    return y


def _fox_sample_kernel(pt_ref, q_ref, kn_ref, vn_ref, col_ref, rown_ref, *rest,
                       n_pages, layer, s_len):
    k_refs = rest[:n_pages]
    v_refs = rest[n_pages:2 * n_pages]
    lf_ref = rest[2 * n_pages]
    o_ref = rest[2 * n_pages + 1]
    s_ref = rest[2 * n_pages + 2]
    b = pl.program_id(0)
    nh = A_HEADS
    rows = nh * s_len
    lane_w = lax.broadcasted_iota(jnp.int32, (1, A_WIDTH), 1)
    rid = lax.broadcasted_iota(jnp.int32, (rows, 1), 0)
    own = jnp.right_shift(lane_w, 6) == jnp.right_shift(rid, 3)
    q = q_ref[...].astype(F32)
    qbd = jnp.where(own, jnp.concatenate([q] * nh, axis=0), 0.0)
    colv = col_ref[...]
    rown = rown_ref[...]
    lo = layer * nh

    lc_all = _lane_cumsum(jnp.concatenate([lf_ref[pt_ref[b, p], lo:lo + nh, :] for p in range(n_pages)], axis=0))
    lcs = [lc_all[p * nh:(p + 1) * nh, :] for p in range(n_pages)]
    tot = [lcs[p][:, LANES - 1:LANES] for p in range(n_pages)]
    base = [None] * n_pages
    suf = jnp.zeros((nh, 1), F32)
    for p in range(n_pages - 1, -1, -1):
        suf = suf + tot[p]
        base[p] = suf

    m_run = [None] * nh
    for p in range(n_pages):
        sp = _dot(qbd, k_refs[p][...])
        for h in range(nh):
            sl = slice(h * s_len, (h + 1) * s_len)
            cn = colv[:, SM_AF + h:SM_AF + h + 1]
            bias = (base[p][h:h + 1, :] - lcs[p][h:h + 1, :]) + cn
            sh = sp[sl, :] + bias
            s_ref[sl, p * LANES:(p + 1) * LANES] = sh
            mx = jnp.max(sh, axis=1, keepdims=True)
            m_run[h] = mx if m_run[h] is None else jnp.maximum(m_run[h], mx)
    sn = lax.dot_general(qbd, kn_ref[...], NT, preferred_element_type=F32)
    r = lax.broadcasted_iota(jnp.int32, (s_len, s_len), 0)
    c = lax.broadcasted_iota(jnp.int32, (s_len, s_len), 1)
    pn, m_all = [], []
    for h in range(nh):
        sl = slice(h * s_len, (h + 1) * s_len)
        cn = colv[:, SM_AF + h:SM_AF + h + 1]
        sh = sn[sl, :] + (cn - rown[SM_AF + h:SM_AF + h + 1, :])
        sh = jnp.where(c <= r, sh, -jnp.inf)
        pn.append(sh)
        m_all.append(jnp.maximum(m_run[h], jnp.max(sh, axis=1, keepdims=True)))
    m = jnp.concatenate(m_all, axis=0)
    p_new = jnp.exp(jnp.concatenate(pn, axis=0) - m)
    l = jnp.sum(p_new, axis=1, keepdims=True)
    acc = _dot(p_new, vn_ref[...])
    for p in range(n_pages):
        pp = jnp.exp(s_ref[:, p * LANES:(p + 1) * LANES] - m)
        l = l + jnp.sum(pp, axis=1, keepdims=True)
        acc = acc + lax.dot_general(pp, v_refs[p][...], NT, preferred_element_type=F32)
    acc = jnp.where(own, acc / l, 0.0)
    y = acc[0:s_len, :]
    for h in range(1, nh):
        y = y + acc[h * s_len:(h + 1) * s_len, :]
    o_ref[...] = y.astype(o_ref.dtype)


def _fox_sample(page_table, q, k_new, v_new, col, rown, cache_k, cache_v, lc, layer, nb, s_len):
    n_pages = page_table.shape[1]
    n_phys, page = cache_k.shape[0], cache_k.shape[1]
    depth = cache_k.shape[2]
    ck = cache_k.transpose(0, 2, 3, 4, 1).reshape(n_phys, depth, A_WIDTH, page)
    cv = cache_v.transpose(0, 2, 3, 4, 1).reshape(n_phys, depth, A_WIDTH, page)
    tok = lambda c: pl.BlockSpec((s_len, c), lambda b, pt: (b, 0))
    page_specs = [pl.BlockSpec((None, None, A_WIDTH, page),
                               lambda b, pt, j=j: (pt[b, j], layer, 0, 0)) for j in range(n_pages)]
    lc_spec = pl.BlockSpec(lc.shape, lambda b, pt: (0, 0, 0), pipeline_mode=pl.Buffered(1))
    grid_spec = pltpu.PrefetchScalarGridSpec(
        num_scalar_prefetch=1,
        grid=(nb,),
        in_specs=[tok(A_WIDTH), tok(A_WIDTH), tok(A_WIDTH), tok(LANES),
                  pl.BlockSpec((None, SM_ROWS, s_len), lambda b, pt: (b, 0, 0))]
                 + page_specs + page_specs + [lc_spec],
        out_specs=tok(A_WIDTH),
        scratch_shapes=[pltpu.VMEM((A_HEADS * s_len, n_pages * LANES), F32)],
    )
    return pl.pallas_call(
        functools.partial(_fox_sample_kernel, n_pages=n_pages, layer=layer, s_len=s_len),
        grid_spec=grid_spec,
        out_shape=jax.ShapeDtypeStruct((nb * s_len, A_WIDTH), BF16),
        compiler_params=_cparams("arbitrary"),
        name="fox_sample",
    )(page_table, q, k_new, v_new, col, rown, *([ck] * n_pages), *([cv] * n_pages), lc)


def _mlstm_kernel(q_ref, k_ref, v_ref, so_ref, col_ref, row_ref, g_ref, c0_ref, n0_ref, m0_ref, _,
                  y_ref, c1_ref, n1_ref, m1_ref, *, t, bb, group):
    nh, hd = M_HEADS, M_HEAD_DIM
    r = lax.broadcasted_iota(jnp.int32, (t, t), 0)
    c = lax.broadcasted_iota(jnp.int32, (t, t), 1)
    causal = c <= r
    g = g_ref[...]
    hs = lambda h: slice(h * hd, (h + 1) * hd)
    last = slice(t - 1, t)
    units = [(u, h) for u in range(group) for h in range(nh)]

    def group_body(gi, _):
        bis = [gi * group + u for u in range(group)]
        r0s = [pl.multiple_of(bi * t, t) for bi in bis]
        qa = [q_ref[pl.ds(r0, t), :].astype(F32) for r0 in r0s]
        ka = [k_ref[pl.ds(r0, t), :].astype(F32) for r0 in r0s]
        va = [v_ref[pl.ds(r0, t), :].astype(F32) for r0 in r0s]
        colv = [col_ref[pl.ds(r0, t), :] for r0 in r0s]
        rows = [row_ref[bi] for bi in bis]
        cs = {(u, h): c0_ref[bis[u], h] for u, h in units}
        ns = {(u, h): n0_ref[bis[u], h] for u, h in units}
        ms = {(u, h): m0_ref[bis[u], h] for u, h in units}
        qk = {(u, h): lax.dot_general(qa[u][:, hs(h)], ka[u][:, hs(h)], NT, preferred_element_type=F32)
              for u, h in units}
        qc = {(u, h): lax.dot_general(qa[u][:, hs(h)], cs[u, h], NT, preferred_element_type=F32) for u, h in units}
        wqk, a_, m_t_, wl_ = {}, {}, {}, {}
        for u, h in units:
            f_col = colv[u][:, SM_MF + h:SM_MF + h + 1]
            i_col = colv[u][:, SM_MI + h:SM_MI + h + 1]
            g_max = colv[u][:, SM_MG + h:SM_MG + h + 1]
            f_row = rows[u][SM_MF + h:SM_MF + h + 1, :]
            i_row = rows[u][SM_MI + h:SM_MI + h + 1, :]
            mx = jnp.maximum(ms[u, h], g_max)
            m_t = f_col + mx
            d = jnp.where(causal, f_col + (i_row - f_row), -jnp.inf)
            wqk[u, h] = jnp.exp(d - m_t) * qk[u, h]
            a_[u, h] = jnp.exp(ms[u, h] - mx)
            m_t_[u, h] = m_t
            wl_[u, h] = jnp.exp((i_col - f_col) + (f_col[last, :] - m_t[last, :]))
        pv = {(u, h): _dot(wqk[u, h], va[u][:, hs(h)]) for u, h in units}
        ys = {}
        for u, h in units:
            a, m_t, w_last = a_[u, h], m_t_[u, h], wl_[u, h]
            num = a * qc[u, h] + pv[u, h]
            den = (a * jnp.sum(qa[u][:, hs(h)] * ns[u, h], axis=1, keepdims=True)
                   + jnp.sum(wqk[u, h], axis=1, keepdims=True))
            h_t = num / jnp.maximum(jnp.abs(den), jnp.exp(-m_t))
            ys[u, h] = h_t * lax.rsqrt(jnp.mean(h_t * h_t, axis=1, keepdims=True) + EPS)
            a_last = a[last, :]
            vw = va[u][:, hs(h)] * w_last
            c1_ref[bis[u], h] = a_last * cs[u, h] + _dot(vw.T, ka[u][:, hs(h)])
            n1_ref[bis[u], h] = a_last * ns[u, h] + jnp.sum(ka[u][:, hs(h)] * w_last, axis=0, keepdims=True)
            m1_ref[bis[u], h] = m_t[last, :]
        for u in range(group):
            hn = jnp.concatenate([ys[u, h] for h in range(nh)], axis=1) * g
            y_ref[pl.ds(r0s[u], t), :] = (so_ref[pl.ds(r0s[u], t), :] * hn).astype(y_ref.dtype)
        return 0

    lax.fori_loop(0, bb // group, group_body, 0)


def _mlstm(q, k, v, so, col, rowt, g, c0, n0, m0, layer_in, c_prev, layer_out, depth, nb, t, bb):
    n = q.shape[0]
    nh, hd = M_HEADS, M_HEAD_DIM
    seq = lambda w: pl.BlockSpec((bb * t, w), lambda i: (i, 0))
    st_in = lambda a, b_: pl.BlockSpec((bb, None, nh, a, b_), lambda i: (i, layer_in, 0, 0, 0))
    st_out = lambda a, b_: pl.BlockSpec((bb, nh, a, b_), lambda i: (i, 0, 0, 0))
    if c_prev is None:
        c_prev = jnp.zeros((SUBLANES, LANES), F32)
        aliases = {}
    else:
        aliases = {10: 1}
    return pl.pallas_call(
        functools.partial(_mlstm_kernel, t=t, bb=bb, group=2 if bb % 2 == 0 else 1),
        grid=(nb // bb,),
        in_specs=[seq(M_WIDTH), seq(M_WIDTH), seq(M_WIDTH), seq(M_WIDTH), seq(LANES),
                  pl.BlockSpec((bb, SM_ROWS, t), lambda i: (i, 0, 0)),
                  _const_spec((1, M_WIDTH)),
                  st_in(hd, hd), st_in(1, hd), st_in(1, 1),
                  pl.BlockSpec(memory_space=pl.ANY)],
        out_specs=[seq(M_WIDTH),
                   pl.BlockSpec((bb, None, nh, hd, hd), lambda i: (i, layer_out, 0, 0, 0)),
                   st_out(1, hd), st_out(1, 1)],
        out_shape=[jax.ShapeDtypeStruct((n, M_WIDTH), BF16),
                   jax.ShapeDtypeStruct((nb, depth, nh, hd, hd), F32),
                   jax.ShapeDtypeStruct((nb, nh, 1, hd), F32),
                   jax.ShapeDtypeStruct((nb, nh, 1, 1), F32)],
        input_output_aliases=aliases,
        compiler_params=_cparams("parallel"),
        name="mlstm",
    )(q, k, v, so, col, rowt, g, c0, n0.reshape(n0.shape[:3] + (1, hd)), m0.reshape(m0.shape[:3] + (1, 1)), c_prev)


C_EXT = M_HEAD_DIM + 16


def _mlstm_t_kernel(q_ref, k_ref, vt_ref, so_ref, col_ref, row_ref, g_ref, _,
                    y_ref, c1_ref, n1_ref, m1_ref, *, t, chunk):
    nh, hd = M_HEADS, M_HEAD_DIM
    s_idx = lax.broadcasted_iota(jnp.int32, (chunk, chunk), 0)
    l_idx = lax.broadcasted_iota(jnp.int32, (chunk, chunk), 1)
    causal = s_idx <= l_idx
    heads = range(nh)
    hs = lambda h: slice(h * hd, (h + 1) * hd)
    last = slice(chunk - 1, chunk)
    pad_rows = jnp.zeros((C_EXT - hd - 1, chunk), F32)

    def chunk_body(ci, carry):
        cs, ms = carry
        r0 = pl.multiple_of(ci * chunk, chunk)
        qa = q_ref[pl.ds(r0, chunk), :]
        ka = k_ref[pl.ds(r0, chunk), :]
        vta = vt_ref[:, pl.ds(r0, chunk)]
        colv = col_ref[pl.ds(r0, chunk), :]
        rows = row_ref[:, pl.ds(r0, chunk)]
        qk = [lax.dot_general(ka[:, hs(h)], qa[:, hs(h)], NT, preferred_element_type=F32) for h in heads]
        cq = [lax.dot_general(cs[h].astype(BF16), qa[:, hs(h)], NT, preferred_element_type=F32) for h in heads]
        wqk, a_, m_t_, wl_ = [], [], [], []
        for h in heads:
            f_row = rows[SM_MF + h:SM_MF + h + 1, :]
            i_row = rows[SM_MI + h:SM_MI + h + 1, :]
            g_max = rows[SM_MG + h:SM_MG + h + 1, :]
            g_col = colv[:, SM_MI + h:SM_MI + h + 1] - colv[:, SM_MF + h:SM_MF + h + 1]
            mx = jnp.maximum(ms[h], g_max)
            m_t = f_row + mx
            d = jnp.where(causal, f_row + g_col, -jnp.inf)
            wqk.append(jnp.exp(d - m_t) * qk[h])
            a_.append(jnp.exp(ms[h] - mx))
            m_t_.append(m_t)
            wl_.append(jnp.exp((i_row - f_row) + (f_row[:, last] - m_t[:, last])))
        pv = [_dot(vta[hs(h), :], wqk[h].astype(BF16)) for h in heads]
        ys, cs_new, ms_new = [], [], []
        for h in heads:
            a, m_t = a_[h], m_t_[h]
            num = a * cq[h][0:hd, :] + pv[h]
            den = a * cq[h][hd:hd + 1, :] + jnp.sum(wqk[h], axis=0, keepdims=True)
            h_t = num / jnp.maximum(jnp.abs(den), jnp.exp(-m_t))
            ys.append((h_t * lax.rsqrt(jnp.mean(h_t * h_t, axis=0, keepdims=True) + EPS)).T)
            vw = jnp.concatenate([vta[hs(h), :].astype(F32) * wl_[h], wl_[h], pad_rows], axis=0)
            cs_new.append(a[:, last] * cs[h] + _dot(vw.astype(BF16), ka[:, hs(h)]))
            ms_new.append(m_t[:, last])
        hn = jnp.concatenate(ys, axis=1) * g_ref[...]
        y_ref[pl.ds(r0, chunk), :] = (so_ref[pl.ds(r0, chunk), :] * hn).astype(y_ref.dtype)
        return cs_new, ms_new

    init = ([jnp.zeros((C_EXT, hd), F32)] * nh, [jnp.zeros((1, 1), F32)] * nh)
    cs, ms = lax.fori_loop(0, t // chunk, chunk_body, init, unroll=True)
    for h in heads:
        c1_ref[h] = cs[h][0:hd, :]
        n1_ref[h] = cs[h][hd:hd + 1, :]
        m1_ref[h] = ms[h]


def _mlstm_t(q, k, vt, so, col, rowt, g, c_prev, layer_out, depth, nb, t, chunk):
    n = q.shape[0]
    nh, hd = M_HEADS, M_HEAD_DIM
    seq = lambda w: pl.BlockSpec((t, w), lambda i: (i, 0))
    st_out = lambda a, b_: pl.BlockSpec((None, nh, a, b_), lambda i: (i, 0, 0, 0))
    if c_prev is None:
        c_prev = jnp.zeros((SUBLANES, LANES), F32)
        aliases = {}
    else:
        aliases = {7: 1}
    return pl.pallas_call(
        functools.partial(_mlstm_t_kernel, t=t, chunk=chunk),
        grid=(nb,),
        in_specs=[seq(M_WIDTH), seq(M_WIDTH), pl.BlockSpec((None, M_WIDTH, t), lambda i: (i, 0, 0)),
                  seq(M_WIDTH), seq(LANES), pl.BlockSpec((None, SM_ROWS, t), lambda i: (i, 0, 0)),
                  _const_spec((1, M_WIDTH)), pl.BlockSpec(memory_space=pl.ANY)],
        out_specs=[seq(M_WIDTH),
                   pl.BlockSpec((None, None, nh, hd, hd), lambda i: (i, layer_out, 0, 0, 0)),
                   st_out(1, hd), st_out(1, 1)],
        out_shape=[jax.ShapeDtypeStruct((n, M_WIDTH), BF16),
                   jax.ShapeDtypeStruct((nb, depth, nh, hd, hd), F32),
                   jax.ShapeDtypeStruct((nb, nh, 1, hd), F32),
                   jax.ShapeDtypeStruct((nb, nh, 1, 1), F32)],
        input_output_aliases=aliases,
        compiler_params=_cparams("parallel"),
        name="mlstm_t",
    )(q, k, vt, so, col, rowt, g, c_prev)


def _conv_carry_init(state_ref, carry_ref, tiles_per_seq):
    @pl.when((pl.program_id(0) % tiles_per_seq) == 0)
    def _():
        carry_ref[...] = state_ref[...]


def _conv_long(u, w_ref, c0, carry_ref):
    tm, ch = u.shape
    prev = carry_ref[:, c0:c0 + ch]
    row = lax.broadcasted_iota(jnp.int32, (tm, 1), 0)
    p1 = jnp.where(row == 0, prev[1:2, :], pltpu.roll(u, 1, axis=0))
    p2 = jnp.where(row == 0, prev[0:1, :], jnp.where(row == 1, prev[1:2, :], pltpu.roll(u, 2, axis=0)))
    carry_ref[:, c0:c0 + ch] = u[tm - 2:tm, :]
    return p2 * w_ref[0:1, c0:c0 + ch] + p1 * w_ref[1:2, c0:c0 + ch] + u * w_ref[2:3, c0:c0 + ch]


def _conv_short(u, w_ref, c0, fill_ref, s_len):
    tm, ch = u.shape
    fill = fill_ref[:, c0:c0 + ch]
    rmod = lax.broadcasted_iota(jnp.int32, (tm, 1), 0) & (s_len - 1)
    p1 = jnp.where(rmod >= 1, pltpu.roll(u, 1, axis=0), pltpu.roll(fill, tm - 1, axis=0))
    p2 = jnp.where(rmod >= 2, pltpu.roll(u, 2, axis=0), fill)
    return p2 * w_ref[0:1, c0:c0 + ch] + p1 * w_ref[1:2, c0:c0 + ch] + u * w_ref[2:3, c0:c0 + ch]


def _merge_kernel(x_ref, g_ref, wc_ref, bc_ref, wg_ref, bg_ref, ws_ref, ya_ref, ym_ref,
                  woa_ref, wom_ref, woc_ref, wo_ref, st_ref, x1_ref, cst_ref, *scratch,
                  long_seq, tiles_per_seq, s_len):
    x = x_ref[...]
    h = _rms(x, g_ref[...]).astype(BF16)
    cw = C_WIDTH
    cb = _dot(h, wc_ref[:, 0:cw]) + bc_ref[:, 0:cw]
    u = (_dot(h, wc_ref[:, cw:2 * cw]) + bc_ref[:, cw:2 * cw]) * (_dot(h, wc_ref[:, 2 * cw:3 * cw]) + bc_ref[:, 2 * cw:3 * cw])
    if long_seq:
        _conv_carry_init(st_ref, scratch[0], tiles_per_seq)
        uc = _conv_long(u, ws_ref, 0, scratch[0])
        cst_ref[...] = u[u.shape[0] - 2:, :]
    else:
        uc = _conv_short(u, ws_ref, 0, st_ref, s_len)
        cst_ref[...] = u
    yc = (cb * uc).astype(BF16)
    d = D_MODEL

    def gate(i):
        return _sigmoid(_dot(h, wg_ref[:, i * d:(i + 1) * d]) + bg_ref[:, i * d:(i + 1) * d])

    merged = gate(0) * _dot(ya_ref[...], woa_ref[...])
    merged = merged + gate(1) * _dot(ym_ref[...], wom_ref[...])
    merged = merged + gate(2) * _dot(yc, woc_ref[...])
    x1_ref[...] = x + _dot(merged.astype(BF16), wo_ref[...])


def _merge(x, g, wc, bc, wg, bg, ws, ya, ym, woa, wom, woc, wo, st, nb, t, tm):
    n = x.shape[0]
    long_seq = t >= tm
    tiles_per_seq = max(t // tm, 1)
    row = lambda c: pl.BlockSpec((tm, c), lambda i: (i, 0))
    if long_seq:
        st_spec = pl.BlockSpec((None, 2, C_WIDTH), lambda i: (i // tiles_per_seq, 0, 0))
        cst_spec = pl.BlockSpec((None, 2, C_WIDTH), lambda i: (i // tiles_per_seq, 0, 0))
        cst_shape = jax.ShapeDtypeStruct((nb, 2, C_WIDTH), F32)
        scratch = [pltpu.VMEM((2, C_WIDTH), F32)]
    else:
        st_spec = row(C_WIDTH)
        cst_spec = row(C_WIDTH)
        cst_shape = jax.ShapeDtypeStruct((n, C_WIDTH), F32)
        scratch = []
    return pl.pallas_call(
        functools.partial(_merge_kernel, long_seq=long_seq, tiles_per_seq=tiles_per_seq, s_len=t),
        grid=(n // tm,),
        in_specs=[row(D_MODEL), _const_spec((1, D_MODEL)),
                  _const_spec(wc.shape), _const_spec(bc.shape), _const_spec(wg.shape), _const_spec(bg.shape),
                  _const_spec(ws.shape), row(A_WIDTH), row(M_WIDTH),
                  _const_spec(woa.shape), _const_spec(wom.shape), _const_spec(woc.shape), _const_spec(wo.shape),
                  st_spec],
        out_specs=[row(D_MODEL), cst_spec],
        out_shape=[jax.ShapeDtypeStruct((n, D_MODEL), F32), cst_shape],
        scratch_shapes=scratch,
        compiler_params=_cparams("arbitrary"),
        name="merge",
    )(x, g, wc, bc, wg, bg, ws, ya, ym, woa, wom, woc, wo, st)


FF_CHUNK = D_FF // 2
FFN_SUB_ROWS = 256


def _ffn_kernel(x_ref, pe_ref, gf_ref, wup_ref, wfc_ref, bfc_ref, wdn_ref, gp_ref, wpg_ref, wpp_ref, gfin_ref,
                st_ref, xo_ref, fst_ref, *scratch, long_seq, tiles_per_seq, s_len, final, sub):
    if long_seq:
        _conv_carry_init(st_ref, scratch[0], tiles_per_seq)
    for r0 in range(0, x_ref.shape[0], sub):
        rs = slice(r0, r0 + sub)
        x = x_ref[rs, :]
        h2 = _rms(x, gf_ref[...]).astype(BF16)
        acc = jnp.zeros((sub, D_MODEL), F32)
        for half in range(D_FF // FF_CHUNK):
            c0 = half * FF_CHUNK
            ua = _dot(h2, wup_ref[:, c0:c0 + FF_CHUNK])
            ub = _dot(h2, wup_ref[:, D_FF + c0:D_FF + c0 + FF_CHUNK])
            if long_seq:
                uac = _conv_long(ua, wfc_ref, c0, scratch[0])
                fst_ref[:, c0:c0 + FF_CHUNK] = ua[sub - 2:, :]
            else:
                uac = _conv_short(ua, wfc_ref, c0, st_ref.at[rs, :], s_len)
                fst_ref[rs, c0:c0 + FF_CHUNK] = ua
            z = uac + bfc_ref[:, c0:c0 + FF_CHUNK]
            act = (z * _sigmoid(z) * ub).astype(BF16)
            acc = acc + _dot(act, wdn_ref[c0:c0 + FF_CHUNK, :])
        x2 = x + acc
        gate = _sigmoid(_dot(_rms(x2, gp_ref[...]).astype(BF16), wpg_ref[...]))
        x3 = x2 + gate * _dot(pe_ref[rs, :].astype(BF16), wpp_ref[...])
        if final:
            x3 = _rms(x3, gfin_ref[...])
        xo_ref[rs, :] = x3


def _ffn(x, pe, layer, gf, wup, wfc, bfc, wdn, gp, wpg, wpp, gfin, st, nb, t, tm, final):
    n = x.shape[0]
    long_seq = t >= tm
    tiles_per_seq = max(t // tm, 1)
    row = lambda c: pl.BlockSpec((tm, c), lambda i: (i, 0))
    if long_seq:
        st_spec = pl.BlockSpec((None, 2, D_FF), lambda i: (i // tiles_per_seq, 0, 0))
        fst_spec = pl.BlockSpec((None, 2, D_FF), lambda i: (i // tiles_per_seq, 0, 0))
        fst_shape = jax.ShapeDtypeStruct((nb, 2, D_FF), F32)
        scratch = [pltpu.VMEM((2, D_FF), F32)]
    else:
        st_spec = row(D_FF)
        fst_spec = row(D_FF)
        fst_shape = jax.ShapeDtypeStruct((n, D_FF), F32)
        scratch = []
    return pl.pallas_call(
        functools.partial(_ffn_kernel, long_seq=long_seq, tiles_per_seq=tiles_per_seq, s_len=t, final=final,
                          sub=min(tm, FFN_SUB_ROWS)),
        grid=(n // tm,),
        in_specs=[row(D_MODEL), pl.BlockSpec((None, tm, pe.shape[2]), lambda i: (layer, i, 0)),
                  _const_spec((1, D_MODEL)),
                  _const_spec(wup.shape), _const_spec(wfc.shape), _const_spec(bfc.shape), _const_spec(wdn.shape),
                  _const_spec((1, D_MODEL)), _const_spec(wpg.shape), _const_spec(wpp.shape),
                  _const_spec((1, D_MODEL)), st_spec],
        out_specs=[row(D_MODEL), fst_spec],
        out_shape=[jax.ShapeDtypeStruct((n, D_MODEL), F32), fst_shape],
        scratch_shapes=scratch,
        compiler_params=_cparams("arbitrary"),
        name="ffn",
    )(x, pe, gf, wup, wfc, bfc, wdn, gp, wpg, wpp, gfin, st)


def _row_form(col, nb, t):
    return col[:, :SM_ROWS].reshape(nb, t, SM_ROWS).transpose(0, 2, 1)


def _short_fill(state, s_len):
    nb, _, ch = state.shape
    return jnp.pad(state, ((0, 0), (0, s_len - 2), (0, 0))).reshape(nb * s_len, ch)


def _pick(n, pref):
    return pref if n % pref == 0 else n


def _layer(x, pe, w, nb, t, chunk, past, final, carry):
    n = x.shape[0]
    layer, depth = carry["layer"], carry["depth"]
    tm_in = _pick(n, 512)
    tm_merge = _pick(n, 512)
    tm_ffn = _pick(n, 2 * FFN_SUB_ROWS)
    if past is None:
        kvp = dict(layer=layer, depth=depth, prev=carry["kvt"], nb=nb, t=t, wt=w["w_kvt"], bt=w["b_kvt"])
        q, k, mq, mk, so, sm, kt, vt, mvt = _inproj(x, w["g_mix"], w["w_am"], w["b_am"], tm_in, kvp)
        col, logf = _gates(sm, t, t, chunk)
        rowt = _row_form(col, nb, t)
        ya = _fox_prompt(q, k, vt, col, layer, nb, t, _pick(t, 512))
        k, v = kt, vt
        ym, c1, n1, m1 = _mlstm_t(mq, mk, mvt, so, col, rowt, w["g_mhead"], carry["c"], layer, depth, nb, t, chunk)
        conv0 = jnp.zeros((nb, 2, C_WIDTH), F32)
        ffn0 = jnp.zeros((nb, 2, D_FF), F32)
    else:
        q, k, v, mq, mk, mv, so, sm = _inproj(x, w["g_mix"], w["w_am"], w["b_am"], tm_in)
        col, logf = _gates(sm, n, t, t)
        rowt = _row_form(col, nb, t)
        ya = _fox_sample(past["page_table"], q, k, v, col, rowt, past["cache_k"], past["cache_v"],
                         past["lc"], layer, nb, t)
        conv0, ffn0 = past["conv0"], past["ffn0"]
        ym, c1, n1, m1 = _mlstm(mq, mk, mv, so, col, rowt, w["g_mhead"], past["c0"], past["n0"], past["m0"], layer,
                                carry["c"], layer, depth, nb, t, _pick(nb, SUBLANES))
    long_merge = t >= tm_merge
    long_ffn = t >= tm_ffn
    x1, conv1 = _merge(x, w["g_mix"], w["w_c"], w["b_c"], w["w_g"], w["b_g"], w["w_sconv"], ya, ym,
                       w["w_oa"], w["w_om"], w["w_oc"], w["w_o"],
                       conv0 if long_merge else _short_fill(conv0, t), nb, t, tm_merge)
    x3, ffn1 = _ffn(x1, pe, layer, w["g_ffn"], w["w_up"], w["w_fconv"], w["b_fconv"], w["w_down"],
                    w["g_ple"], w["w_ple_gate"], w["w_ple"], w["g_final"],
                    ffn0 if long_ffn else _short_fill(ffn0, t), nb, t, tm_ffn, final)
    if not long_merge:
        conv1 = conv1.reshape(nb, t, C_WIDTH)[:, t - 2:, :]
    if not long_ffn:
        ffn1 = ffn1.reshape(nb, t, D_FF)[:, t - 2:, :]
    if past is not None:
        k = k.reshape(nb, t, A_HEADS, A_HEAD_DIM)
        v = v.reshape(nb, t, A_HEADS, A_HEAD_DIM)
    state = (k, v, logf.reshape(nb, t, A_HEADS), c1, n1.reshape(nb, M_HEADS, M_HEAD_DIM),
             m1.reshape(nb, M_HEADS), conv1, ffn1)
    return x3, state


def _layer_weights(l, w_in, b_in, g_mix, g_mhead, w_sconv, w_oa, w_om, w_oc, w_o, g_ffn, w_up,
                   w_fconv, b_fconv, w_down, g_ple, w_ple_gate, w_ple, g_final):
    a0 = 0
    af0 = 3 * A_WIDTH
    m0 = af0 + A_HEADS
    mi0 = m0 + 3 * M_WIDTH
    mo0 = mi0 + 2 * M_HEADS
    c0 = mo0 + M_WIDTH
    g0 = c0 + 3 * C_WIDTH
    wl, bl = w_in[l], b_in[l]
    pad = LANES - SM_USED

    def cols(a, lo, hi):
        return a[..., lo:hi]

    def am(a):
        return jnp.concatenate(
            [cols(a, a0, af0), cols(a, m0, mi0), cols(a, mo0, c0),
             cols(a, af0, m0), cols(a, mi0, mo0), jnp.zeros(a.shape[:-1] + (pad,), a.dtype)], axis=-1)

    row = lambda a: a.reshape(1, -1)
    return {
        "w_am": am(wl).astype(BF16), "b_am": row(am(bl)),
        "w_kvt": jnp.concatenate([cols(wl, A_WIDTH, af0), cols(wl, m0 + 2 * M_WIDTH, mi0)], axis=-1).T.astype(BF16),
        "b_kvt": jnp.concatenate([cols(bl, A_WIDTH, af0), cols(bl, m0 + 2 * M_WIDTH, mi0)], axis=-1).reshape(-1, 1),
        "w_c": cols(wl, c0, g0).astype(BF16), "b_c": row(cols(bl, c0, g0)),
        "w_g": cols(wl, g0, g0 + 3 * D_MODEL).astype(BF16), "b_g": row(cols(bl, g0, g0 + 3 * D_MODEL)),
        "g_mix": row(g_mix[l]), "g_mhead": row(g_mhead[l]), "w_sconv": w_sconv[l],
        "w_oa": w_oa[l].astype(BF16), "w_om": w_om[l].astype(BF16), "w_oc": w_oc[l].astype(BF16),
        "w_o": w_o[l].astype(BF16), "g_ffn": row(g_ffn[l]), "w_up": w_up[l].astype(BF16),
        "w_fconv": w_fconv[l], "b_fconv": row(b_fconv[l]), "w_down": w_down[l].astype(BF16),
        "g_ple": row(g_ple[l]), "w_ple_gate": w_ple_gate[l].astype(BF16), "w_ple": w_ple[l].astype(BF16),
        "g_final": row(g_final),
    }


def kernel(x_prompt, x_sample, cache_k, cache_v, cache_logf, state_mlstm_C, state_mlstm_n, state_mlstm_m, state_conv, state_ffn_conv, page_table, p_prompt, p_sample, w_in, b_in, g_mix, g_mhead, w_sconv, w_oa, w_om, w_oc, w_o, g_ffn, w_up, w_fconv, b_fconv, w_down, g_ple, w_ple_gate, w_ple, g_final):
    nbp, tp, _ = x_prompt.shape
    nbs, ts, _ = x_sample.shape
    depth = w_in.shape[0]
    n_phys, page = cache_logf.shape[0], cache_logf.shape[1]
    assert page == LANES and ts == SUBLANES and depth * A_HEADS == SM_USED

    lc = cache_logf.reshape(n_phys, page, depth * A_HEADS).transpose(0, 2, 1)

    xp = x_prompt.reshape(nbp * tp, D_MODEL)
    xs = x_sample.reshape(nbs * ts, D_MODEL)
    chunk_p = _pick(tp, 256)
    new_p, new_s = [], []
    kvt = c_p = c_s = None
    for l in range(depth):
        w = _layer_weights(l, w_in, b_in, g_mix, g_mhead, w_sconv, w_oa, w_om, w_oc, w_o, g_ffn, w_up,
                           w_fconv, b_fconv, w_down, g_ple, w_ple_gate, w_ple, g_final)
        final = l == depth - 1
        xp, st_p = _layer(xp, p_prompt.reshape(depth, nbp * tp, -1), w, nbp, tp, chunk_p, None, final,
                          {"layer": l, "depth": depth, "kvt": kvt, "c": c_p})
        kvt, c_p = st_p[:2], st_p[3]
        past = {"page_table": page_table, "cache_k": cache_k, "cache_v": cache_v, "lc": lc,
                "c0": state_mlstm_C, "n0": state_mlstm_n, "m0": state_mlstm_m,
                "conv0": state_conv[:, l], "ffn0": state_ffn_conv[:, l]}
        xs, st_s = _layer(xs, p_sample.reshape(depth, nbs * ts, -1), w, nbs, ts, ts, past, final,
                          {"layer": l, "depth": depth, "kvt": None, "c": c_s})
        c_s = st_s[3]
        new_p.append(st_p)
        new_s.append(st_s)

    def stack(per_layer, c_all, first):
        rows = [jnp.stack([s[i] for s in per_layer], axis=2) for i in range(first, 3)]
        states = [jnp.stack([s[i] for s in per_layer], axis=1) for i in range(4, 8)]
        return tuple(rows + [c_all] + states)

    kv_p = tuple(a.reshape(nbp, depth, A_HEADS, A_HEAD_DIM, tp).transpose(0, 4, 1, 2, 3) for a in kvt)
    return ((xp.reshape(nbp, tp, D_MODEL), xs.reshape(nbs, ts, D_MODEL)) + kv_p + stack(new_p, c_p, 2)
            + stack(new_s, c_s, 0))
```

```python
import functools

import jax
import jax.numpy as jnp
from jax import lax
from jax.experimental import pallas as pl
from jax.experimental.pallas import tpu as pltpu

F32 = jnp.float32
BF16 = jnp.bfloat16

D_MODEL = 1024
A_HEADS = 8
A_HEAD_DIM = 64
A_WIDTH = A_HEADS * A_HEAD_DIM
M_HEADS = 4
M_HEAD_DIM = 128
M_WIDTH = M_HEADS * M_HEAD_DIM
C_WIDTH = 512
D_FF = 2816
EPS = 1e-6
LANES = 128
SUBLANES = 8
VMEM_LIMIT = 56 * 1024 * 1024

SM_AF = 0
SM_MI = 8
SM_MF = 12
SM_USED = 16
SM_MG = 16
SM_ROWS = 24

NT = (((1,), (1,)), ((), ()))
LOG2E = 1.4426950408889634


def _cparams(*sem):
    return pltpu.CompilerParams(dimension_semantics=sem, vmem_limit_bytes=VMEM_LIMIT)


def _const_spec(shape):
    nd = len(shape)
    return pl.BlockSpec(shape, lambda *_: (0,) * nd, pipeline_mode=pl.Buffered(1))


def _rms(x, g):
    return x * lax.rsqrt(jnp.mean(x * x, axis=-1, keepdims=True) + EPS) * g


def _log_sigmoid(x):
    return jnp.minimum(x, 0.0) - jnp.log1p(jnp.exp(-jnp.abs(x)))


def _sigmoid(x):
    return 1.0 / (1.0 + jnp.exp(-x))


def _dot(a, b):
    return jnp.dot(a, b, preferred_element_type=F32)


def _inproj_kernel(*refs, transposed_kv):
    if transposed_kv:
        (x_ref, g_ref, w_ref, b_ref, wt_ref, bt_ref, _, _,
         q_ref, mq_ref, mk_ref, so_ref, sm_ref, kt_ref, vt_ref, mvt_ref) = refs
    else:
        x_ref, g_ref, w_ref, b_ref, q_ref, k_ref, v_ref, mq_ref, mk_ref, mv_ref, so_ref, sm_ref = refs
    h = _rms(x_ref[...], g_ref[...]).astype(BF16)

    def proj(c0, n):
        return _dot(h, w_ref[:, c0:c0 + n]) + b_ref[:, c0:c0 + n]

    q_scale = A_HEAD_DIM ** -0.5 * (LOG2E if transposed_kv else 1.0)
    q_ref[...] = (proj(0, A_WIDTH) * q_scale).astype(BF16)
    if transposed_kv:
        kt_ref[...] = (lax.dot_general(wt_ref[0:A_WIDTH, :], h, NT, preferred_element_type=F32)
                       + bt_ref[0:A_WIDTH, :])
        vt_ref[...] = (lax.dot_general(wt_ref[A_WIDTH:2 * A_WIDTH, :], h, NT, preferred_element_type=F32)
                       + bt_ref[A_WIDTH:2 * A_WIDTH, :])
        mvt_ref[...] = (lax.dot_general(wt_ref[2 * A_WIDTH:, :], h, NT, preferred_element_type=F32)
                        + bt_ref[2 * A_WIDTH:, :]).astype(BF16)
    else:
        k_ref[...] = proj(A_WIDTH, A_WIDTH)
        v_ref[...] = proj(2 * A_WIDTH, A_WIDTH)
    o = 3 * A_WIDTH
    mq_ref[...] = proj(o, M_WIDTH).astype(BF16)
    mk_ref[...] = (proj(o + M_WIDTH, M_WIDTH) * (M_HEAD_DIM ** -0.5)).astype(BF16)
    if not transposed_kv:
        mv_ref[...] = proj(o + 2 * M_WIDTH, M_WIDTH).astype(BF16)
    so_ref[...] = _sigmoid(proj(o + 3 * M_WIDTH, M_WIDTH))
    sm_ref[...] = proj(o + 4 * M_WIDTH, LANES)


def _inproj(x, g, w, b, tm, kv=None):
    n = x.shape[0]
    wcols = w.shape[1]
    row = lambda c: pl.BlockSpec((tm, c), lambda i: (i, 0))
    common = [
        jax.ShapeDtypeStruct((n, M_WIDTH), BF16),
        jax.ShapeDtypeStruct((n, M_WIDTH), BF16),
        jax.ShapeDtypeStruct((n, M_WIDTH), BF16),
        jax.ShapeDtypeStruct((n, M_WIDTH), F32),
        jax.ShapeDtypeStruct((n, LANES), F32),
    ]
    common_specs = [row(M_WIDTH)] * 4 + [row(LANES)]
    in_specs = [row(D_MODEL), _const_spec((1, D_MODEL)), _const_spec((D_MODEL, wcols)), _const_spec((1, wcols))]
    if kv is None:
        return pl.pallas_call(
            functools.partial(_inproj_kernel, transposed_kv=False),
            grid=(n // tm,),
            in_specs=in_specs,
            out_specs=[row(A_WIDTH)] * 3 + common_specs,
            out_shape=[jax.ShapeDtypeStruct((n, A_WIDTH), BF16), jax.ShapeDtypeStruct((n, A_WIDTH), F32),
                       jax.ShapeDtypeStruct((n, A_WIDTH), F32)] + common,
            compiler_params=_cparams("parallel"),
            name="inproj",
        )(x, g, w, b)
    nb, t, depth, layer = kv["nb"], kv["t"], kv["depth"], kv["layer"]
    tps = t // tm
    t_shape = jax.ShapeDtypeStruct((nb, depth, A_WIDTH, t), F32)
    t_spec = pl.BlockSpec((None, None, A_WIDTH, tm), lambda i: (i // tps, layer, 0, i % tps))
    prev = kv["prev"]
    if prev is None:
        prev = (jnp.zeros((SUBLANES, LANES), F32),) * 2
        aliases = {}
    else:
        aliases = {6: 5, 7: 6}
    any_spec = pl.BlockSpec(memory_space=pl.ANY)
    no_mv = lambda items: items[:2] + items[3:]
    return pl.pallas_call(
        functools.partial(_inproj_kernel, transposed_kv=True),
        grid=(n // tm,),
        in_specs=in_specs + [_const_spec(kv["wt"].shape), _const_spec(kv["bt"].shape), any_spec, any_spec],
        out_specs=[row(A_WIDTH)] + no_mv(common_specs) + [t_spec, t_spec,
                   pl.BlockSpec((None, M_WIDTH, tm), lambda i: (i // tps, 0, i % tps))],
        out_shape=[jax.ShapeDtypeStruct((n, A_WIDTH), BF16)]
                  + no_mv(common) + [t_shape, t_shape, jax.ShapeDtypeStruct((nb, M_WIDTH, t), BF16)],
        input_output_aliases=aliases,
        compiler_params=_cparams("parallel"),
        name="inproj_t",
    )(x, g, w, b, kv["wt"], kv["bt"], *prev)


def _gates_kernel(sm_ref, col_ref, logf_ref, *, seq_a, seg_m):
    x = sm_ref[...]
    rows = x.shape[0]
    lane = lax.broadcasted_iota(jnp.int32, (1, LANES), 1)
    row = lax.broadcasted_iota(jnp.int32, (rows, 1), 0)
    ls = _log_sigmoid(x)
    logf_ref[...] = ls[:, SM_AF:SM_AF + A_HEADS]
    is_a = lane < SM_MI
    is_i = (lane >= SM_MI) & (lane < SM_MF)
    is_f = (lane >= SM_MF) & (lane < SM_USED)
    y = jnp.where(is_i, x, ls)
    rmod_a = row & (seq_a - 1)
    rmod_m = row & (seg_m - 1)
    s = 1
    while s < max(seq_a, seg_m):
        take = jnp.zeros((rows, LANES), jnp.bool_)
        if s < seq_a:
            take = take | (is_a & (rmod_a >= s))
        if s < seg_m:
            take = take | (is_f & (rmod_m >= s))
        y = y + jnp.where(take, pltpu.roll(y, s, axis=0), 0.0)
        s *= 2
    nm = SM_MF - SM_MI
    is_g = (lane >= SM_MG) & (lane < SM_MG + nm)
    g = pltpu.roll(y, SM_MG - SM_MI, axis=1) - pltpu.roll(y, SM_MG - SM_MF, axis=1)
    s = 1
    while s < seg_m:
        g = jnp.where(is_g & (rmod_m >= s), jnp.maximum(g, pltpu.roll(g, s, axis=0)), g)
        s *= 2
    col_ref[...] = jnp.where(is_g, g, y)


def _gates(sm, rows, seq_a, seg_m):
    n = sm.shape[0]
    return pl.pallas_call(
        functools.partial(_gates_kernel, seq_a=seq_a, seg_m=seg_m),
        grid=(n // rows,),
        in_specs=[pl.BlockSpec((rows, LANES), lambda i: (i, 0))],
        out_specs=[pl.BlockSpec((rows, LANES), lambda i: (i, 0)),
                   pl.BlockSpec((rows, A_HEADS), lambda i: (i, 0))],
        out_shape=[jax.ShapeDtypeStruct((n, LANES), F32),
                   jax.ShapeDtypeStruct((n, A_HEADS), F32)],
        compiler_params=_cparams("parallel"),
        name="gates",
    )(sm)


V_EXT = A_HEAD_DIM + 16


def _fox_kernel(q_ref, kt_ref, vt_ref, col_ref, o_ref, vb_ref, ka_ref, sa_ref, sb_ref, pa_ref, pb_ref, *, tq, t):
    hp = pl.program_id(1)
    lane = lax.broadcasted_iota(jnp.int32, (1, LANES), 1)
    hd = A_HEAD_DIM
    aug0 = [hd * (1 - hh) for hh in range(2)]

    ones_rows = jnp.ones((V_EXT - hd, t), BF16)
    for hh in range(2):
        vb_ref[hh, 0:hd, :] = vt_ref[hh * hd:(hh + 1) * hd, :].astype(BF16)
        vb_ref[hh, hd:V_EXT, :] = ones_rows
    colv = col_ref[...]
    kf = kt_ref[...].T.astype(BF16)
    for hh in range(2):
        ck = jnp.sum(jnp.where(lane == SM_AF + hp * 2 + hh, colv, 0.0), axis=1, keepdims=True) * (-LOG2E)
        c1 = ck.astype(BF16)
        r1 = ck - c1.astype(F32)
        c2 = r1.astype(BF16)
        c3 = (r1 - c2.astype(F32)).astype(BF16)
        a0 = aug0[hh]
        ka_ref[hh] = jnp.where(lane == a0, c1, jnp.where(lane == a0 + 1, c2, jnp.where(lane == a0 + 2, c3, kf)))

    slots = ((sa_ref, pa_ref), (sb_ref, pb_ref))
    kpos = lax.broadcasted_iota(jnp.int32, (tq, tq), 0)
    qpos = lax.broadcasted_iota(jnp.int32, (tq, tq), 1)

    def values(j, p_in):
        return [_dot(vb_ref[hh, :, j * tq:(j + 1) * tq], p_in[hh]) for hh in range(2)]

    for qi in range(t // tq):
        q2 = q_ref[qi * tq:(qi + 1) * tq, :]
        zero = jnp.zeros_like(q2)
        one = jnp.ones_like(q2)
        qm = [jnp.where((lane >= aug0[hh]) & (lane < aug0[hh] + 3), one,
                        jnp.where((lane >= hd) == bool(hh), q2, zero)) for hh in range(2)]

        def scores(j, s_out):
            for hh in range(2):
                s_out[hh] = lax.dot_general(ka_ref[hh, j * tq:(j + 1) * tq, :], qm[hh], NT,
                                            preferred_element_type=F32)

        def softmax(s_in, p_out, ms, diag):
            m_out, alphas = [], []
            for hh in range(2):
                s = s_in[hh]
                if diag:
                    s = jnp.where(kpos <= qpos, s, -jnp.inf)
                m_new = jnp.max(s, axis=0, keepdims=True)
                if ms is not None:
                    m_new = jnp.maximum(ms[hh], m_new)
                    alphas.append(jnp.exp2(ms[hh] - m_new))
                m_out.append(m_new)
                p_out[hh] = jnp.exp2(s - m_new).astype(BF16)
            return m_out, alphas

        scores(0, slots[qi % 2][0])
        ms = a_prev = accs = None
        for j in range(qi):
            (s_cur, p_cur), (s_nxt, p_nxt) = slots[(qi - j) % 2], slots[(qi - j - 1) % 2]
            pv = values(j - 1, p_nxt) if j > 0 else None
            scores(j + 1, s_nxt)
            ms, alphas = softmax(s_cur, p_cur, ms, False)
            if j == 1:
                accs = pv
            elif j > 1:
                accs = [a_prev[hh] * accs[hh] + pv[hh] for hh in range(2)]
            a_prev = alphas
        pv = values(qi - 1, pb_ref) if qi > 0 else None
        ms, alphas = softmax(sa_ref, pa_ref, ms, True)
        acc = values(qi, pa_ref)
        if qi == 1:
            acc = [alphas[hh] * pv[hh] + acc[hh] for hh in range(2)]
        elif qi > 1:
            acc = [alphas[hh] * (a_prev[hh] * accs[hh] + pv[hh]) + acc[hh] for hh in range(2)]
        yt = [acc[hh][0:hd, :] / acc[hh][hd:hd + 1, :] for hh in range(2)]
        o_ref[qi * tq:(qi + 1) * tq, :] = jnp.concatenate(yt, axis=0).T.astype(o_ref.dtype)


def _fox_prompt(q, kt, vt, col, layer, nb, t, tq):
    n = q.shape[0]
    pairs = A_WIDTH // LANES
    seq = pl.BlockSpec((t, LANES), lambda b, hp: (b, hp))
    feat = pl.BlockSpec((None, None, LANES, t), lambda b, hp: (b, layer, hp, 0))
    return pl.pallas_call(
        functools.partial(_fox_kernel, tq=tq, t=t),
        grid=(nb, pairs),
        in_specs=[seq, feat, feat, pl.BlockSpec((t, LANES), lambda b, hp: (b, 0))],
        out_specs=seq,
        out_shape=jax.ShapeDtypeStruct((n, A_WIDTH), BF16),
        scratch_shapes=[pltpu.VMEM((2, V_EXT, t), BF16), pltpu.VMEM((2, t, LANES), BF16),
                        pltpu.VMEM((2, tq, tq), F32), pltpu.VMEM((2, tq, tq), F32),
                        pltpu.VMEM((2, tq, tq), BF16), pltpu.VMEM((2, tq, tq), BF16)],
        compiler_params=_cparams("parallel", "parallel"),
        name="fox_prompt",
    )(q, kt, vt, col)


def _lane_cumsum(y):
    lane = lax.broadcasted_iota(jnp.int32, (1, LANES), 1)
    s = 1
    while s < LANES:
        y = y + jnp.where(lane >= s, pltpu.roll(y, s, axis=1), 0.0)
        s *= 2
    return y


def _fox_sample_kernel(pt_ref, q_ref, kn_ref, vn_ref, col_ref, rown_ref, *rest,
                       n_pages, layer, s_len):
    k_refs = rest[:n_pages]
    v_refs = rest[n_pages:2 * n_pages]
    lf_ref = rest[2 * n_pages]
    o_ref = rest[2 * n_pages + 1]
    s_ref = rest[2 * n_pages + 2]
    b = pl.program_id(0)
    nh = A_HEADS
    rows = nh * s_len
    lane_w = lax.broadcasted_iota(jnp.int32, (1, A_WIDTH), 1)
    rid = lax.broadcasted_iota(jnp.int32, (rows, 1), 0)
    own = jnp.right_shift(lane_w, 6) == jnp.right_shift(rid, 3)
    q = q_ref[...].astype(F32)
    qbd = jnp.where(own, jnp.concatenate([q] * nh, axis=0), 0.0)
    colv = col_ref[...]
    rown = rown_ref[...]
    lo = layer * nh

    lc_all = _lane_cumsum(jnp.concatenate([lf_ref[pt_ref[b, p], lo:lo + nh, :] for p in range(n_pages)], axis=0))
    lcs = [lc_all[p * nh:(p + 1) * nh, :] for p in range(n_pages)]
    tot = [lcs[p][:, LANES - 1:LANES] for p in range(n_pages)]
    base = [None] * n_pages
    suf = jnp.zeros((nh, 1), F32)
    for p in range(n_pages - 1, -1, -1):
        suf = suf + tot[p]
        base[p] = suf

    m_run = [None] * nh
    for p in range(n_pages):
        sp = _dot(qbd, k_refs[p][...])
        for h in range(nh):
            sl = slice(h * s_len, (h + 1) * s_len)
            cn = colv[:, SM_AF + h:SM_AF + h + 1]
            bias = (base[p][h:h + 1, :] - lcs[p][h:h + 1, :]) + cn
            sh = sp[sl, :] + bias
            s_ref[sl, p * LANES:(p + 1) * LANES] = sh
            mx = jnp.max(sh, axis=1, keepdims=True)
            m_run[h] = mx if m_run[h] is None else jnp.maximum(m_run[h], mx)
    sn = lax.dot_general(qbd, kn_ref[...], NT, preferred_element_type=F32)
    r = lax.broadcasted_iota(jnp.int32, (s_len, s_len), 0)
    c = lax.broadcasted_iota(jnp.int32, (s_len, s_len), 1)
    pn, m_all = [], []
    for h in range(nh):
        sl = slice(h * s_len, (h + 1) * s_len)
        cn = colv[:, SM_AF + h:SM_AF + h + 1]
        sh = sn[sl, :] + (cn - rown[SM_AF + h:SM_AF + h + 1, :])
        sh = jnp.where(c <= r, sh, -jnp.inf)
        pn.append(sh)
        m_all.append(jnp.maximum(m_run[h], jnp.max(sh, axis=1, keepdims=True)))
    m = jnp.concatenate(m_all, axis=0)
    p_new = jnp.exp(jnp.concatenate(pn, axis=0) - m)
    l = jnp.sum(p_new, axis=1, keepdims=True)
    acc = _dot(p_new, vn_ref[...])
    for p in range(n_pages):
        pp = jnp.exp(s_ref[:, p * LANES:(p + 1) * LANES] - m)
        l = l + jnp.sum(pp, axis=1, keepdims=True)
        acc = acc + lax.dot_general(pp, v_refs[p][...], NT, preferred_element_type=F32)
    acc = jnp.where(own, acc / l, 0.0)
    y = acc[0:s_len, :]
    for h in range(1, nh):
        y = y + acc[h * s_len:(h + 1) * s_len, :]
    o_ref[...] = y.astype(o_ref.dtype)


def _fox_sample(page_table, q, k_new, v_new, col, rown, cache_k, cache_v, lc, layer, nb, s_len):
    n_pages = page_table.shape[1]
    n_phys, page = cache_k.shape[0], cache_k.shape[1]
    depth = cache_k.shape[2]
    ck = cache_k.transpose(0, 2, 3, 4, 1).reshape(n_phys, depth, A_WIDTH, page)
    cv = cache_v.transpose(0, 2, 3, 4, 1).reshape(n_phys, depth, A_WIDTH, page)
    tok = lambda c: pl.BlockSpec((s_len, c), lambda b, pt: (b, 0))
    page_specs = [pl.BlockSpec((None, None, A_WIDTH, page),
                               lambda b, pt, j=j: (pt[b, j], layer, 0, 0)) for j in range(n_pages)]
    lc_spec = pl.BlockSpec(lc.shape, lambda b, pt: (0, 0, 0), pipeline_mode=pl.Buffered(1))
    grid_spec = pltpu.PrefetchScalarGridSpec(
        num_scalar_prefetch=1,
        grid=(nb,),
        in_specs=[tok(A_WIDTH), tok(A_WIDTH), tok(A_WIDTH), tok(LANES),
                  pl.BlockSpec((None, SM_ROWS, s_len), lambda b, pt: (b, 0, 0))]
                 + page_specs + page_specs + [lc_spec],
        out_specs=tok(A_WIDTH),
        scratch_shapes=[pltpu.VMEM((A_HEADS * s_len, n_pages * LANES), F32)],
    )
    return pl.pallas_call(
        functools.partial(_fox_sample_kernel, n_pages=n_pages, layer=layer, s_len=s_len),
        grid_spec=grid_spec,
        out_shape=jax.ShapeDtypeStruct((nb * s_len, A_WIDTH), BF16),
        compiler_params=_cparams("arbitrary"),
        name="fox_sample",
    )(page_table, q, k_new, v_new, col, rown, *([ck] * n_pages), *([cv] * n_pages), lc)


def _mlstm_kernel(q_ref, k_ref, v_ref, so_ref, col_ref, row_ref, g_ref, c0_ref, n0_ref, m0_ref, _,
                  y_ref, c1_ref, n1_ref, m1_ref, *, t, bb, group):
    nh, hd = M_HEADS, M_HEAD_DIM
    r = lax.broadcasted_iota(jnp.int32, (t, t), 0)
    c = lax.broadcasted_iota(jnp.int32, (t, t), 1)
    causal = c <= r
    g = g_ref[...]
    hs = lambda h: slice(h * hd, (h + 1) * hd)
    last = slice(t - 1, t)
    units = [(u, h) for u in range(group) for h in range(nh)]

    def group_body(gi, _):
        bis = [gi * group + u for u in range(group)]
        r0s = [pl.multiple_of(bi * t, t) for bi in bis]
        qa = [q_ref[pl.ds(r0, t), :].astype(F32) for r0 in r0s]
        ka = [k_ref[pl.ds(r0, t), :].astype(F32) for r0 in r0s]
        va = [v_ref[pl.ds(r0, t), :].astype(F32) for r0 in r0s]
        colv = [col_ref[pl.ds(r0, t), :] for r0 in r0s]
        rows = [row_ref[bi] for bi in bis]
        cs = {(u, h): c0_ref[bis[u], h] for u, h in units}
        ns = {(u, h): n0_ref[bis[u], h] for u, h in units}
        ms = {(u, h): m0_ref[bis[u], h] for u, h in units}
        qk = {(u, h): lax.dot_general(qa[u][:, hs(h)], ka[u][:, hs(h)], NT, preferred_element_type=F32)
              for u, h in units}
        qc = {(u, h): lax.dot_general(qa[u][:, hs(h)], cs[u, h], NT, preferred_element_type=F32) for u, h in units}
        wqk, a_, m_t_, wl_ = {}, {}, {}, {}
        for u, h in units:
            f_col = colv[u][:, SM_MF + h:SM_MF + h + 1]
            i_col = colv[u][:, SM_MI + h:SM_MI + h + 1]
            g_max = colv[u][:, SM_MG + h:SM_MG + h + 1]
            f_row = rows[u][SM_MF + h:SM_MF + h + 1, :]
            i_row = rows[u][SM_MI + h:SM_MI + h + 1, :]
            mx = jnp.maximum(ms[u, h], g_max)
            m_t = f_col + mx
            d = jnp.where(causal, f_col + (i_row - f_row), -jnp.inf)
            wqk[u, h] = jnp.exp(d - m_t) * qk[u, h]
            a_[u, h] = jnp.exp(ms[u, h] - mx)
            m_t_[u, h] = m_t
            wl_[u, h] = jnp.exp((i_col - f_col) + (f_col[last, :] - m_t[last, :]))
        pv = {(u, h): _dot(wqk[u, h], va[u][:, hs(h)]) for u, h in units}
        ys = {}
        for u, h in units:
            a, m_t, w_last = a_[u, h], m_t_[u, h], wl_[u, h]
            num = a * qc[u, h] + pv[u, h]
            den = (a * jnp.sum(qa[u][:, hs(h)] * ns[u, h], axis=1, keepdims=True)
                   + jnp.sum(wqk[u, h], axis=1, keepdims=True))
            h_t = num / jnp.maximum(jnp.abs(den), jnp.exp(-m_t))
            ys[u, h] = h_t * lax.rsqrt(jnp.mean(h_t * h_t, axis=1, keepdims=True) + EPS)
            a_last = a[last, :]
            vw = va[u][:, hs(h)] * w_last
            c1_ref[bis[u], h] = a_last * cs[u, h] + _dot(vw.T, ka[u][:, hs(h)])
            n1_ref[bis[u], h] = a_last * ns[u, h] + jnp.sum(ka[u][:, hs(h)] * w_last, axis=0, keepdims=True)
            m1_ref[bis[u], h] = m_t[last, :]
        for u in range(group):
            hn = jnp.concatenate([ys[u, h] for h in range(nh)], axis=1) * g
            y_ref[pl.ds(r0s[u], t), :] = (so_ref[pl.ds(r0s[u], t), :] * hn).astype(y_ref.dtype)
        return 0

    lax.fori_loop(0, bb // group, group_body, 0)


def _mlstm(q, k, v, so, col, rowt, g, c0, n0, m0, layer_in, c_prev, layer_out, depth, nb, t, bb):
    n = q.shape[0]
    nh, hd = M_HEADS, M_HEAD_DIM
    seq = lambda w: pl.BlockSpec((bb * t, w), lambda i: (i, 0))
    st_in = lambda a, b_: pl.BlockSpec((bb, None, nh, a, b_), lambda i: (i, layer_in, 0, 0, 0))
    st_out = lambda a, b_: pl.BlockSpec((bb, nh, a, b_), lambda i: (i, 0, 0, 0))
    if c_prev is None:
        c_prev = jnp.zeros((SUBLANES, LANES), F32)
        aliases = {}
    else:
        aliases = {10: 1}
    return pl.pallas_call(
        functools.partial(_mlstm_kernel, t=t, bb=bb, group=2 if bb % 2 == 0 else 1),
        grid=(nb // bb,),
        in_specs=[seq(M_WIDTH), seq(M_WIDTH), seq(M_WIDTH), seq(M_WIDTH), seq(LANES),
                  pl.BlockSpec((bb, SM_ROWS, t), lambda i: (i, 0, 0)),
                  _const_spec((1, M_WIDTH)),
                  st_in(hd, hd), st_in(1, hd), st_in(1, 1),
                  pl.BlockSpec(memory_space=pl.ANY)],
        out_specs=[seq(M_WIDTH),
                   pl.BlockSpec((bb, None, nh, hd, hd), lambda i: (i, layer_out, 0, 0, 0)),
                   st_out(1, hd), st_out(1, 1)],
        out_shape=[jax.ShapeDtypeStruct((n, M_WIDTH), BF16),
                   jax.ShapeDtypeStruct((nb, depth, nh, hd, hd), F32),
                   jax.ShapeDtypeStruct((nb, nh, 1, hd), F32),
                   jax.ShapeDtypeStruct((nb, nh, 1, 1), F32)],
        input_output_aliases=aliases,
        compiler_params=_cparams("parallel"),
        name="mlstm",
    )(q, k, v, so, col, rowt, g, c0, n0.reshape(n0.shape[:3] + (1, hd)), m0.reshape(m0.shape[:3] + (1, 1)), c_prev)


C_EXT = M_HEAD_DIM + 16


def _mlstm_t_kernel(q_ref, k_ref, vt_ref, so_ref, col_ref, row_ref, g_ref, _,
                    y_ref, c1_ref, n1_ref, m1_ref, *, t, chunk):
    nh, hd = M_HEADS, M_HEAD_DIM
    s_idx = lax.broadcasted_iota(jnp.int32, (chunk, chunk), 0)
    l_idx = lax.broadcasted_iota(jnp.int32, (chunk, chunk), 1)
    causal = s_idx <= l_idx
    heads = range(nh)
    hs = lambda h: slice(h * hd, (h + 1) * hd)
    last = slice(chunk - 1, chunk)
    pad_rows = jnp.zeros((C_EXT - hd - 1, chunk), F32)

    def chunk_body(ci, carry):
        cs, ms = carry
        r0 = pl.multiple_of(ci * chunk, chunk)
        qa = q_ref[pl.ds(r0, chunk), :]
        ka = k_ref[pl.ds(r0, chunk), :]
        vta = vt_ref[:, pl.ds(r0, chunk)]
        colv = col_ref[pl.ds(r0, chunk), :]
        rows = row_ref[:, pl.ds(r0, chunk)]
        qk = [lax.dot_general(ka[:, hs(h)], qa[:, hs(h)], NT, preferred_element_type=F32) for h in heads]
        cq = [lax.dot_general(cs[h].astype(BF16), qa[:, hs(h)], NT, preferred_element_type=F32) for h in heads]
        wqk, a_, m_t_, wl_ = [], [], [], []
        for h in heads:
            f_row = rows[SM_MF + h:SM_MF + h + 1, :]
            i_row = rows[SM_MI + h:SM_MI + h + 1, :]
            g_max = rows[SM_MG + h:SM_MG + h + 1, :]
            g_col = colv[:, SM_MI + h:SM_MI + h + 1] - colv[:, SM_MF + h:SM_MF + h + 1]
            mx = jnp.maximum(ms[h], g_max)
            m_t = f_row + mx
            d = jnp.where(causal, f_row + g_col, -jnp.inf)
            wqk.append(jnp.exp(d - m_t) * qk[h])
            a_.append(jnp.exp(ms[h] - mx))
            m_t_.append(m_t)
            wl_.append(jnp.exp((i_row - f_row) + (f_row[:, last] - m_t[:, last])))
        pv = [_dot(vta[hs(h), :], wqk[h].astype(BF16)) for h in heads]
        ys, cs_new, ms_new = [], [], []
        for h in heads:
            a, m_t = a_[h], m_t_[h]
            num = a * cq[h][0:hd, :] + pv[h]
            den = a * cq[h][hd:hd + 1, :] + jnp.sum(wqk[h], axis=0, keepdims=True)
            h_t = num / jnp.maximum(jnp.abs(den), jnp.exp(-m_t))
            ys.append((h_t * lax.rsqrt(jnp.mean(h_t * h_t, axis=0, keepdims=True) + EPS)).T)
            vw = jnp.concatenate([vta[hs(h), :].astype(F32) * wl_[h], wl_[h], pad_rows], axis=0)
            cs_new.append(a[:, last] * cs[h] + _dot(vw.astype(BF16), ka[:, hs(h)]))
            ms_new.append(m_t[:, last])
        hn = jnp.concatenate(ys, axis=1) * g_ref[...]
        y_ref[pl.ds(r0, chunk), :] = (so_ref[pl.ds(r0, chunk), :] * hn).astype(y_ref.dtype)
        return cs_new, ms_new

    init = ([jnp.zeros((C_EXT, hd), F32)] * nh, [jnp.zeros((1, 1), F32)] * nh)
    cs, ms = lax.fori_loop(0, t // chunk, chunk_body, init, unroll=True)
    for h in heads:
        c1_ref[h] = cs[h][0:hd, :]
        n1_ref[h] = cs[h][hd:hd + 1, :]
        m1_ref[h] = ms[h]


def _mlstm_t(q, k, vt, so, col, rowt, g, c_prev, layer_out, depth, nb, t, chunk):
    n = q.shape[0]
    nh, hd = M_HEADS, M_HEAD_DIM
    seq = lambda w: pl.BlockSpec((t, w), lambda i: (i, 0))
    st_out = lambda a, b_: pl.BlockSpec((None, nh, a, b_), lambda i: (i, 0, 0, 0))
    if c_prev is None:
        c_prev = jnp.zeros((SUBLANES, LANES), F32)
        aliases = {}
    else:
        aliases = {7: 1}
    return pl.pallas_call(
        functools.partial(_mlstm_t_kernel, t=t, chunk=chunk),
        grid=(nb,),
        in_specs=[seq(M_WIDTH), seq(M_WIDTH), pl.BlockSpec((None, M_WIDTH, t), lambda i: (i, 0, 0)),
                  seq(M_WIDTH), seq(LANES), pl.BlockSpec((None, SM_ROWS, t), lambda i: (i, 0, 0)),
                  _const_spec((1, M_WIDTH)), pl.BlockSpec(memory_space=pl.ANY)],
        out_specs=[seq(M_WIDTH),
                   pl.BlockSpec((None, None, nh, hd, hd), lambda i: (i, layer_out, 0, 0, 0)),
                   st_out(1, hd), st_out(1, 1)],
        out_shape=[jax.ShapeDtypeStruct((n, M_WIDTH), BF16),
                   jax.ShapeDtypeStruct((nb, depth, nh, hd, hd), F32),
                   jax.ShapeDtypeStruct((nb, nh, 1, hd), F32),
                   jax.ShapeDtypeStruct((nb, nh, 1, 1), F32)],
        input_output_aliases=aliases,
        compiler_params=_cparams("parallel"),
        name="mlstm_t",
    )(q, k, vt, so, col, rowt, g, c_prev)


def _conv_carry_init(state_ref, carry_ref, tiles_per_seq):
    @pl.when((pl.program_id(0) % tiles_per_seq) == 0)
    def _():
        carry_ref[...] = state_ref[...]


def _conv_long(u, w_ref, c0, carry_ref):
    tm, ch = u.shape
    prev = carry_ref[:, c0:c0 + ch]
    row = lax.broadcasted_iota(jnp.int32, (tm, 1), 0)
    p1 = jnp.where(row == 0, prev[1:2, :], pltpu.roll(u, 1, axis=0))
    p2 = jnp.where(row == 0, prev[0:1, :], jnp.where(row == 1, prev[1:2, :], pltpu.roll(u, 2, axis=0)))
    carry_ref[:, c0:c0 + ch] = u[tm - 2:tm, :]
    return p2 * w_ref[0:1, c0:c0 + ch] + p1 * w_ref[1:2, c0:c0 + ch] + u * w_ref[2:3, c0:c0 + ch]


def _conv_short(u, w_ref, c0, fill_ref, s_len):
    tm, ch = u.shape
    fill = fill_ref[:, c0:c0 + ch]
    rmod = lax.broadcasted_iota(jnp.int32, (tm, 1), 0) & (s_len - 1)
    p1 = jnp.where(rmod >= 1, pltpu.roll(u, 1, axis=0), pltpu.roll(fill, tm - 1, axis=0))
    p2 = jnp.where(rmod >= 2, pltpu.roll(u, 2, axis=0), fill)
    return p2 * w_ref[0:1, c0:c0 + ch] + p1 * w_ref[1:2, c0:c0 + ch] + u * w_ref[2:3, c0:c0 + ch]


def _merge_kernel(x_ref, g_ref, wc_ref, bc_ref, wg_ref, bg_ref, ws_ref, ya_ref, ym_ref,
                  woa_ref, wom_ref, woc_ref, wo_ref, st_ref, x1_ref, cst_ref, *scratch,
                  long_seq, tiles_per_seq, s_len):
    x = x_ref[...]
    h = _rms(x, g_ref[...]).astype(BF16)
    cw = C_WIDTH
    cb = _dot(h, wc_ref[:, 0:cw]) + bc_ref[:, 0:cw]
    u = (_dot(h, wc_ref[:, cw:2 * cw]) + bc_ref[:, cw:2 * cw]) * (_dot(h, wc_ref[:, 2 * cw:3 * cw]) + bc_ref[:, 2 * cw:3 * cw])
    if long_seq:
        _conv_carry_init(st_ref, scratch[0], tiles_per_seq)
        uc = _conv_long(u, ws_ref, 0, scratch[0])
        cst_ref[...] = u[u.shape[0] - 2:, :]
    else:
        uc = _conv_short(u, ws_ref, 0, st_ref, s_len)
        cst_ref[...] = u
    yc = (cb * uc).astype(BF16)
    d = D_MODEL

    def gate(i):
        return _sigmoid(_dot(h, wg_ref[:, i * d:(i + 1) * d]) + bg_ref[:, i * d:(i + 1) * d])

    merged = gate(0) * _dot(ya_ref[...], woa_ref[...])
    merged = merged + gate(1) * _dot(ym_ref[...], wom_ref[...])
    merged = merged + gate(2) * _dot(yc, woc_ref[...])
    x1_ref[...] = x + _dot(merged.astype(BF16), wo_ref[...])


def _merge(x, g, wc, bc, wg, bg, ws, ya, ym, woa, wom, woc, wo, st, nb, t, tm):
    n = x.shape[0]
    long_seq = t >= tm
    tiles_per_seq = max(t // tm, 1)
    row = lambda c: pl.BlockSpec((tm, c), lambda i: (i, 0))
    if long_seq:
        st_spec = pl.BlockSpec((None, 2, C_WIDTH), lambda i: (i // tiles_per_seq, 0, 0))
        cst_spec = pl.BlockSpec((None, 2, C_WIDTH), lambda i: (i // tiles_per_seq, 0, 0))
        cst_shape = jax.ShapeDtypeStruct((nb, 2, C_WIDTH), F32)
        scratch = [pltpu.VMEM((2, C_WIDTH), F32)]
    else:
        st_spec = row(C_WIDTH)
        cst_spec = row(C_WIDTH)
        cst_shape = jax.ShapeDtypeStruct((n, C_WIDTH), F32)
        scratch = []
    return pl.pallas_call(
        functools.partial(_merge_kernel, long_seq=long_seq, tiles_per_seq=tiles_per_seq, s_len=t),
        grid=(n // tm,),
        in_specs=[row(D_MODEL), _const_spec((1, D_MODEL)),
                  _const_spec(wc.shape), _const_spec(bc.shape), _const_spec(wg.shape), _const_spec(bg.shape),
                  _const_spec(ws.shape), row(A_WIDTH), row(M_WIDTH),
                  _const_spec(woa.shape), _const_spec(wom.shape), _const_spec(woc.shape), _const_spec(wo.shape),
                  st_spec],
        out_specs=[row(D_MODEL), cst_spec],
        out_shape=[jax.ShapeDtypeStruct((n, D_MODEL), F32), cst_shape],
        scratch_shapes=scratch,
        compiler_params=_cparams("arbitrary"),
        name="merge",
    )(x, g, wc, bc, wg, bg, ws, ya, ym, woa, wom, woc, wo, st)


FF_CHUNK = D_FF // 2
FFN_SUB_ROWS = 256


def _ffn_kernel(x_ref, pe_ref, gf_ref, wup_ref, wfc_ref, bfc_ref, wdn_ref, gp_ref, wpg_ref, wpp_ref, gfin_ref,
                st_ref, xo_ref, fst_ref, *scratch, long_seq, tiles_per_seq, s_len, final, sub):
    if long_seq:
        _conv_carry_init(st_ref, scratch[0], tiles_per_seq)
    for r0 in range(0, x_ref.shape[0], sub):
        rs = slice(r0, r0 + sub)
        x = x_ref[rs, :]
        h2 = _rms(x, gf_ref[...]).astype(BF16)
        acc = jnp.zeros((sub, D_MODEL), F32)
        for half in range(D_FF // FF_CHUNK):
            c0 = half * FF_CHUNK
            ua = _dot(h2, wup_ref[:, c0:c0 + FF_CHUNK])
            ub = _dot(h2, wup_ref[:, D_FF + c0:D_FF + c0 + FF_CHUNK])
            if long_seq:
                uac = _conv_long(ua, wfc_ref, c0, scratch[0])
                fst_ref[:, c0:c0 + FF_CHUNK] = ua[sub - 2:, :]
            else:
                uac = _conv_short(ua, wfc_ref, c0, st_ref.at[rs, :], s_len)
                fst_ref[rs, c0:c0 + FF_CHUNK] = ua
            z = uac + bfc_ref[:, c0:c0 + FF_CHUNK]
            act = (z * _sigmoid(z) * ub).astype(BF16)
            acc = acc + _dot(act, wdn_ref[c0:c0 + FF_CHUNK, :])
        x2 = x + acc
        gate = _sigmoid(_dot(_rms(x2, gp_ref[...]).astype(BF16), wpg_ref[...]))
        x3 = x2 + gate * _dot(pe_ref[rs, :].astype(BF16), wpp_ref[...])
        if final:
            x3 = _rms(x3, gfin_ref[...])
        xo_ref[rs, :] = x3


def _ffn(x, pe, layer, gf, wup, wfc, bfc, wdn, gp, wpg, wpp, gfin, st, nb, t, tm, final):
    n = x.shape[0]
    long_seq = t >= tm
    tiles_per_seq = max(t // tm, 1)
    row = lambda c: pl.BlockSpec((tm, c), lambda i: (i, 0))
    if long_seq:
        st_spec = pl.BlockSpec((None, 2, D_FF), lambda i: (i // tiles_per_seq, 0, 0))
        fst_spec = pl.BlockSpec((None, 2, D_FF), lambda i: (i // tiles_per_seq, 0, 0))
        fst_shape = jax.ShapeDtypeStruct((nb, 2, D_FF), F32)
        scratch = [pltpu.VMEM((2, D_FF), F32)]
    else:
        st_spec = row(D_FF)
        fst_spec = row(D_FF)
        fst_shape = jax.ShapeDtypeStruct((n, D_FF), F32)
        scratch = []
    return pl.pallas_call(
        functools.partial(_ffn_kernel, long_seq=long_seq, tiles_per_seq=tiles_per_seq, s_len=t, final=final,
                          sub=min(tm, FFN_SUB_ROWS)),
        grid=(n // tm,),
        in_specs=[row(D_MODEL), pl.BlockSpec((None, tm, pe.shape[2]), lambda i: (layer, i, 0)),
                  _const_spec((1, D_MODEL)),
                  _const_spec(wup.shape), _const_spec(wfc.shape), _const_spec(bfc.shape), _const_spec(wdn.shape),
                  _const_spec((1, D_MODEL)), _const_spec(wpg.shape), _const_spec(wpp.shape),
                  _const_spec((1, D_MODEL)), st_spec],
        out_specs=[row(D_MODEL), fst_spec],
        out_shape=[jax.ShapeDtypeStruct((n, D_MODEL), F32), fst_shape],
        scratch_shapes=scratch,
        compiler_params=_cparams("arbitrary"),
        name="ffn",
    )(x, pe, gf, wup, wfc, bfc, wdn, gp, wpg, wpp, gfin, st)


def _row_form(col, nb, t):
    return col[:, :SM_ROWS].reshape(nb, t, SM_ROWS).transpose(0, 2, 1)


def _short_fill(state, s_len):
    nb, _, ch = state.shape
    return jnp.pad(state, ((0, 0), (0, s_len - 2), (0, 0))).reshape(nb * s_len, ch)


def _pick(n, pref):
    return pref if n % pref == 0 else n


def _layer(x, pe, w, nb, t, chunk, past, final, carry):
    n = x.shape[0]
    layer, depth = carry["layer"], carry["depth"]
    tm_in = _pick(n, 512)
    tm_merge = _pick(n, 512)
    tm_ffn = _pick(n, 2 * FFN_SUB_ROWS)
    if past is None:
        kvp = dict(layer=layer, depth=depth, prev=carry["kvt"], nb=nb, t=t, wt=w["w_kvt"], bt=w["b_kvt"])
        q, mq, mk, so, sm, kt, vt, mvt = _inproj(x, w["g_mix"], w["w_am"], w["b_am"], tm_in, kvp)
        col, logf = _gates(sm, t, t, chunk)
        rowt = _row_form(col, nb, t)
        ya = _fox_prompt(q, kt, vt, col, layer, nb, t, _pick(t, 512))
        k, v = kt, vt
        ym, c1, n1, m1 = _mlstm_t(mq, mk, mvt, so, col, rowt, w["g_mhead"], carry["c"], layer, depth, nb, t, chunk)
        conv0 = jnp.zeros((nb, 2, C_WIDTH), F32)
        ffn0 = jnp.zeros((nb, 2, D_FF), F32)
    else:
        q, k, v, mq, mk, mv, so, sm = _inproj(x, w["g_mix"], w["w_am"], w["b_am"], tm_in)
        col, logf = _gates(sm, n, t, t)
        rowt = _row_form(col, nb, t)
        ya = _fox_sample(past["page_table"], q, k, v, col, rowt, past["cache_k"], past["cache_v"],
                         past["lc"], layer, nb, t)
        conv0, ffn0 = past["conv0"], past["ffn0"]
        ym, c1, n1, m1 = _mlstm(mq, mk, mv, so, col, rowt, w["g_mhead"], past["c0"], past["n0"], past["m0"], layer,
                                carry["c"], layer, depth, nb, t, _pick(nb, SUBLANES))
    long_merge = t >= tm_merge
    long_ffn = t >= tm_ffn
    x1, conv1 = _merge(x, w["g_mix"], w["w_c"], w["b_c"], w["w_g"], w["b_g"], w["w_sconv"], ya, ym,
                       w["w_oa"], w["w_om"], w["w_oc"], w["w_o"],
                       conv0 if long_merge else _short_fill(conv0, t), nb, t, tm_merge)
    x3, ffn1 = _ffn(x1, pe, layer, w["g_ffn"], w["w_up"], w["w_fconv"], w["b_fconv"], w["w_down"],
                    w["g_ple"], w["w_ple_gate"], w["w_ple"], w["g_final"],
                    ffn0 if long_ffn else _short_fill(ffn0, t), nb, t, tm_ffn, final)
    if not long_merge:
        conv1 = conv1.reshape(nb, t, C_WIDTH)[:, t - 2:, :]
    if not long_ffn:
        ffn1 = ffn1.reshape(nb, t, D_FF)[:, t - 2:, :]
    if past is not None:
        k = k.reshape(nb, t, A_HEADS, A_HEAD_DIM)
        v = v.reshape(nb, t, A_HEADS, A_HEAD_DIM)
    state = (k, v, logf.reshape(nb, t, A_HEADS), c1, n1.reshape(nb, M_HEADS, M_HEAD_DIM),
             m1.reshape(nb, M_HEADS), conv1, ffn1)
    return x3, state


def _layer_weights(l, w_in, b_in, g_mix, g_mhead, w_sconv, w_oa, w_om, w_oc, w_o, g_ffn, w_up,
                   w_fconv, b_fconv, w_down, g_ple, w_ple_gate, w_ple, g_final):
    a0 = 0
    af0 = 3 * A_WIDTH
    m0 = af0 + A_HEADS
    mi0 = m0 + 3 * M_WIDTH
    mo0 = mi0 + 2 * M_HEADS
    c0 = mo0 + M_WIDTH
    g0 = c0 + 3 * C_WIDTH
    wl, bl = w_in[l], b_in[l]
    pad = LANES - SM_USED

    def cols(a, lo, hi):
        return a[..., lo:hi]

    def am(a):
        return jnp.concatenate(
            [cols(a, a0, af0), cols(a, m0, mi0), cols(a, mo0, c0),
             cols(a, af0, m0), cols(a, mi0, mo0), jnp.zeros(a.shape[:-1] + (pad,), a.dtype)], axis=-1)

    row = lambda a: a.reshape(1, -1)
    return {
        "w_am": am(wl).astype(BF16), "b_am": row(am(bl)),
        "w_kvt": jnp.concatenate([cols(wl, A_WIDTH, af0), cols(wl, m0 + 2 * M_WIDTH, mi0)], axis=-1).T.astype(BF16),
        "b_kvt": jnp.concatenate([cols(bl, A_WIDTH, af0), cols(bl, m0 + 2 * M_WIDTH, mi0)], axis=-1).reshape(-1, 1),
        "w_c": cols(wl, c0, g0).astype(BF16), "b_c": row(cols(bl, c0, g0)),
        "w_g": cols(wl, g0, g0 + 3 * D_MODEL).astype(BF16), "b_g": row(cols(bl, g0, g0 + 3 * D_MODEL)),
        "g_mix": row(g_mix[l]), "g_mhead": row(g_mhead[l]), "w_sconv": w_sconv[l],
        "w_oa": w_oa[l].astype(BF16), "w_om": w_om[l].astype(BF16), "w_oc": w_oc[l].astype(BF16),
        "w_o": w_o[l].astype(BF16), "g_ffn": row(g_ffn[l]), "w_up": w_up[l].astype(BF16),
        "w_fconv": w_fconv[l], "b_fconv": row(b_fconv[l]), "w_down": w_down[l].astype(BF16),
        "g_ple": row(g_ple[l]), "w_ple_gate": w_ple_gate[l].astype(BF16), "w_ple": w_ple[l].astype(BF16),
        "g_final": row(g_final),
    }


def kernel(x_prompt, x_sample, cache_k, cache_v, cache_logf, state_mlstm_C, state_mlstm_n, state_mlstm_m, state_conv, state_ffn_conv, page_table, p_prompt, p_sample, w_in, b_in, g_mix, g_mhead, w_sconv, w_oa, w_om, w_oc, w_o, g_ffn, w_up, w_fconv, b_fconv, w_down, g_ple, w_ple_gate, w_ple, g_final):
    nbp, tp, _ = x_prompt.shape
    nbs, ts, _ = x_sample.shape
    depth = w_in.shape[0]
    n_phys, page = cache_logf.shape[0], cache_logf.shape[1]
    assert page == LANES and ts == SUBLANES and depth * A_HEADS == SM_USED

    lc = cache_logf.reshape(n_phys, page, depth * A_HEADS).transpose(0, 2, 1)

    xp = x_prompt.reshape(nbp * tp, D_MODEL)
    xs = x_sample.reshape(nbs * ts, D_MODEL)
    chunk_p = _pick(tp, 256)
    new_p, new_s = [], []
    kvt = c_p = c_s = None
    for l in range(depth):
        w = _layer_weights(l, w_in, b_in, g_mix, g_mhead, w_sconv, w_oa, w_om, w_oc, w_o, g_ffn, w_up,
                           w_fconv, b_fconv, w_down, g_ple, w_ple_gate, w_ple, g_final)
        final = l == depth - 1
        xp, st_p = _layer(xp, p_prompt.reshape(depth, nbp * tp, -1), w, nbp, tp, chunk_p, None, final,
                          {"layer": l, "depth": depth, "kvt": kvt, "c": c_p})
        kvt, c_p = st_p[:2], st_p[3]
        past = {"page_table": page_table, "cache_k": cache_k, "cache_v": cache_v, "lc": lc,
                "c0": state_mlstm_C, "n0": state_mlstm_n, "m0": state_mlstm_m,
                "conv0": state_conv[:, l], "ffn0": state_ffn_conv[:, l]}
        xs, st_s = _layer(xs, p_sample.reshape(depth, nbs * ts, -1), w, nbs, ts, ts, past, final,
                          {"layer": l, "depth": depth, "kvt": None, "c": c_s})
        c_s = st_s[3]
        new_p.append(st_p)
        new_s.append(st_s)

    def stack(per_layer, c_all, first):
        rows = [jnp.stack([s[i] for s in per_layer], axis=2) for i in range(first, 3)]
        states = [jnp.stack([s[i] for s in per_layer], axis=1) for i in range(4, 8)]
        return tuple(rows + [c_all] + states)

    kv_p = tuple(a.reshape(nbp, depth, A_HEADS, A_HEAD_DIM, tp).transpose(0, 4, 1, 2, 3) for a in kvt)
    return ((xp.reshape(nbp, tp, D_MODEL), xs.reshape(nbs, ts, D_MODEL)) + kv_p + stack(new_p, c_p, 2)
            + stack(new_s, c_s, 0))
```

```python
import functools

import jax
import jax.numpy as jnp
from jax import lax
from jax.experimental import pallas as pl
from jax.experimental.pallas import tpu as pltpu

F32 = jnp.float32
BF16 = jnp.bfloat16

D_MODEL = 1024
A_HEADS = 8
A_HEAD_DIM = 64
A_WIDTH = A_HEADS * A_HEAD_DIM
M_HEADS = 4
M_HEAD_DIM = 128
M_WIDTH = M_HEADS * M_HEAD_DIM
C_WIDTH = 512
D_FF = 2816
EPS = 1e-6
LANES = 128
SUBLANES = 8
VMEM_LIMIT = 56 * 1024 * 1024

SM_AF = 0
SM_MI = 8
SM_MF = 12
SM_USED = 16
SM_MG = 16
SM_ROWS = 24

NT = (((1,), (1,)), ((), ()))
LOG2E = 1.4426950408889634


def _cparams(*sem):
    return pltpu.CompilerParams(dimension_semantics=sem, vmem_limit_bytes=VMEM_LIMIT)


def _const_spec(shape):
    nd = len(shape)
    return pl.BlockSpec(shape, lambda *_: (0,) * nd, pipeline_mode=pl.Buffered(1))


def _rms(x, g):
    return x * lax.rsqrt(jnp.mean(x * x, axis=-1, keepdims=True) + EPS) * g


def _log_sigmoid(x):
    return jnp.minimum(x, 0.0) - jnp.log1p(jnp.exp(-jnp.abs(x)))


def _sigmoid(x):
    return 1.0 / (1.0 + jnp.exp(-x))


def _dot(a, b):
    return jnp.dot(a, b, preferred_element_type=F32)


def _inproj_kernel(*refs, transposed_kv):
    if transposed_kv:
        (x_ref, g_ref, w_ref, b_ref, wt_ref, bt_ref, _, _,
         q_ref, mq_ref, mk_ref, so_ref, sm_ref, kt_ref, vt_ref, mvt_ref) = refs
    else:
        x_ref, g_ref, w_ref, b_ref, q_ref, k_ref, v_ref, mq_ref, mk_ref, mv_ref, so_ref, sm_ref = refs
    h = _rms(x_ref[...], g_ref[...]).astype(BF16)

    def proj(c0, n):
        return _dot(h, w_ref[:, c0:c0 + n]) + b_ref[:, c0:c0 + n]

    q_scale = A_HEAD_DIM ** -0.5 * (LOG2E if transposed_kv else 1.0)
    q_ref[...] = (proj(0, A_WIDTH) * q_scale).astype(BF16)
    if transposed_kv:
        kt_ref[...] = (lax.dot_general(wt_ref[0:A_WIDTH, :], h, NT, preferred_element_type=F32)
                       + bt_ref[0:A_WIDTH, :])
        vt_ref[...] = (lax.dot_general(wt_ref[A_WIDTH:2 * A_WIDTH, :], h, NT, preferred_element_type=F32)
                       + bt_ref[A_WIDTH:2 * A_WIDTH, :])
        mvt_ref[...] = (lax.dot_general(wt_ref[2 * A_WIDTH:, :], h, NT, preferred_element_type=F32)
                        + bt_ref[2 * A_WIDTH:, :]).astype(BF16)
    else:
        k_ref[...] = proj(A_WIDTH, A_WIDTH)
        v_ref[...] = proj(2 * A_WIDTH, A_WIDTH)
    o = 3 * A_WIDTH
    mq_ref[...] = proj(o, M_WIDTH).astype(BF16)
    mk_ref[...] = (proj(o + M_WIDTH, M_WIDTH) * (M_HEAD_DIM ** -0.5)).astype(BF16)
    if not transposed_kv:
        mv_ref[...] = proj(o + 2 * M_WIDTH, M_WIDTH).astype(BF16)
    so_ref[...] = _sigmoid(proj(o + 3 * M_WIDTH, M_WIDTH))
    sm_ref[...] = proj(o + 4 * M_WIDTH, LANES)


def _inproj(x, g, w, b, tm, kv=None):
    n = x.shape[0]
    wcols = w.shape[1]
    row = lambda c: pl.BlockSpec((tm, c), lambda i: (i, 0))
    common = [
        jax.ShapeDtypeStruct((n, M_WIDTH), BF16),
        jax.ShapeDtypeStruct((n, M_WIDTH), BF16),
        jax.ShapeDtypeStruct((n, M_WIDTH), BF16),
        jax.ShapeDtypeStruct((n, M_WIDTH), F32),
        jax.ShapeDtypeStruct((n, LANES), F32),
    ]
    common_specs = [row(M_WIDTH)] * 4 + [row(LANES)]
    in_specs = [row(D_MODEL), _const_spec((1, D_MODEL)), _const_spec((D_MODEL, wcols)), _const_spec((1, wcols))]
    if kv is None:
        return pl.pallas_call(
            functools.partial(_inproj_kernel, transposed_kv=False),
            grid=(n // tm,),
            in_specs=in_specs,
            out_specs=[row(A_WIDTH)] * 3 + common_specs,
            out_shape=[jax.ShapeDtypeStruct((n, A_WIDTH), BF16), jax.ShapeDtypeStruct((n, A_WIDTH), F32),
                       jax.ShapeDtypeStruct((n, A_WIDTH), F32)] + common,
            compiler_params=_cparams("parallel"),
            name="inproj",
        )(x, g, w, b)
    nb, t, depth, layer = kv["nb"], kv["t"], kv["depth"], kv["layer"]
    tps = t // tm
    t_shape = jax.ShapeDtypeStruct((nb, depth, A_WIDTH, t), F32)
    t_spec = pl.BlockSpec((None, None, A_WIDTH, tm), lambda i: (i // tps, layer, 0, i % tps))
    prev = kv["prev"]
    if prev is None:
        prev = (jnp.zeros((SUBLANES, LANES), F32),) * 2
        aliases = {}
    else:
        aliases = {6: 5, 7: 6}
    any_spec = pl.BlockSpec(memory_space=pl.ANY)
    no_mv = lambda items: items[:2] + items[3:]
    return pl.pallas_call(
        functools.partial(_inproj_kernel, transposed_kv=True),
        grid=(n // tm,),
        in_specs=in_specs + [_const_spec(kv["wt"].shape), _const_spec(kv["bt"].shape), any_spec, any_spec],
        out_specs=[row(A_WIDTH)] + no_mv(common_specs) + [t_spec, t_spec,
                   pl.BlockSpec((None, M_WIDTH, tm), lambda i: (i // tps, 0, i % tps))],
        out_shape=[jax.ShapeDtypeStruct((n, A_WIDTH), BF16)]
                  + no_mv(common) + [t_shape, t_shape, jax.ShapeDtypeStruct((nb, M_WIDTH, t), BF16)],
        input_output_aliases=aliases,
        compiler_params=_cparams("parallel"),
        name="inproj_t",
    )(x, g, w, b, kv["wt"], kv["bt"], *prev)


def _gates_kernel(sm_ref, col_ref, logf_ref, *, seq_a, seg_m):
    x = sm_ref[...]
    rows = x.shape[0]
    lane = lax.broadcasted_iota(jnp.int32, (1, LANES), 1)
    row = lax.broadcasted_iota(jnp.int32, (rows, 1), 0)
    ls = _log_sigmoid(x)
    logf_ref[...] = ls[:, SM_AF:SM_AF + A_HEADS]
    is_a = lane < SM_MI
    is_i = (lane >= SM_MI) & (lane < SM_MF)
    is_f = (lane >= SM_MF) & (lane < SM_USED)
    y = jnp.where(is_i, x, ls)
    rmod_a = row & (seq_a - 1)
    rmod_m = row & (seg_m - 1)
    s = 1
    while s < max(seq_a, seg_m):
        take = jnp.zeros((rows, LANES), jnp.bool_)
        if s < seq_a:
            take = take | (is_a & (rmod_a >= s))
        if s < seg_m:
            take = take | (is_f & (rmod_m >= s))
        y = y + jnp.where(take, pltpu.roll(y, s, axis=0), 0.0)
        s *= 2
    nm = SM_MF - SM_MI
    is_g = (lane >= SM_MG) & (lane < SM_MG + nm)
    g = pltpu.roll(y, SM_MG - SM_MI, axis=1) - pltpu.roll(y, SM_MG - SM_MF, axis=1)
    s = 1
    while s < seg_m:
        g = jnp.where(is_g & (rmod_m >= s), jnp.maximum(g, pltpu.roll(g, s, axis=0)), g)
        s *= 2
    col_ref[...] = jnp.where(is_g, g, y)


def _gates(sm, rows, seq_a, seg_m):
    n = sm.shape[0]
    return pl.pallas_call(
        functools.partial(_gates_kernel, seq_a=seq_a, seg_m=seg_m),
        grid=(n // rows,),
        in_specs=[pl.BlockSpec((rows, LANES), lambda i: (i, 0))],
        out_specs=[pl.BlockSpec((rows, LANES), lambda i: (i, 0)),
                   pl.BlockSpec((rows, A_HEADS), lambda i: (i, 0))],
        out_shape=[jax.ShapeDtypeStruct((n, LANES), F32),
                   jax.ShapeDtypeStruct((n, A_HEADS), F32)],
        compiler_params=_cparams("parallel"),
        name="gates",
    )(sm)


V_EXT = A_HEAD_DIM + 16


def _fox_kernel(q_ref, kt_ref, vt_ref, col_ref, o_ref, vb_ref, ka_ref, sa_ref, sb_ref, pa_ref, pb_ref, *, tq, t):
    hp = pl.program_id(1)
    lane = lax.broadcasted_iota(jnp.int32, (1, LANES), 1)
    hd = A_HEAD_DIM
    aug0 = [hd * (1 - hh) for hh in range(2)]

    ones_rows = jnp.ones((V_EXT - hd, t), BF16)
    for hh in range(2):
        vb_ref[hh, 0:hd, :] = vt_ref[hh * hd:(hh + 1) * hd, :].astype(BF16)
        vb_ref[hh, hd:V_EXT, :] = ones_rows
    colv = col_ref[...]
    kf = kt_ref[...].T.astype(BF16)
    for hh in range(2):
        ck = jnp.sum(jnp.where(lane == SM_AF + hp * 2 + hh, colv, 0.0), axis=1, keepdims=True) * (-LOG2E)
        c1 = ck.astype(BF16)
        r1 = ck - c1.astype(F32)
        c2 = r1.astype(BF16)
        c3 = (r1 - c2.astype(F32)).astype(BF16)
        a0 = aug0[hh]
        ka_ref[hh] = jnp.where(lane == a0, c1, jnp.where(lane == a0 + 1, c2, jnp.where(lane == a0 + 2, c3, kf)))

    slots = ((sa_ref, pa_ref), (sb_ref, pb_ref))
    kpos = lax.broadcasted_iota(jnp.int32, (tq, tq), 0)
    qpos = lax.broadcasted_iota(jnp.int32, (tq, tq), 1)

    def values(j, p_in):
        return [_dot(vb_ref[hh, :, j * tq:(j + 1) * tq], p_in[hh]) for hh in range(2)]

    for qi in range(t // tq):
        q2 = q_ref[qi * tq:(qi + 1) * tq, :]
        zero = jnp.zeros_like(q2)
        one = jnp.ones_like(q2)
        qm = [jnp.where((lane >= aug0[hh]) & (lane < aug0[hh] + 3), one,
                        jnp.where((lane >= hd) == bool(hh), q2, zero)) for hh in range(2)]

        def scores(j, s_out):
            for hh in range(2):
                s_out[hh] = lax.dot_general(ka_ref[hh, j * tq:(j + 1) * tq, :], qm[hh], NT,
                                            preferred_element_type=F32)

        def softmax(s_in, p_out, ms, diag):
            m_out, alphas = [], []
            for hh in range(2):
                s = s_in[hh]
                if diag:
                    s = jnp.where(kpos <= qpos, s, -jnp.inf)
                m_new = jnp.max(s, axis=0, keepdims=True)
                if ms is not None:
                    m_new = jnp.maximum(ms[hh], m_new)
                    alphas.append(jnp.exp2(ms[hh] - m_new))
                m_out.append(m_new)
                p_out[hh] = jnp.exp2(s - m_new).astype(BF16)
            return m_out, alphas

        scores(0, slots[qi % 2][0])
        ms = a_prev = accs = None
        for j in range(qi):
            (s_cur, p_cur), (s_nxt, p_nxt) = slots[(qi - j) % 2], slots[(qi - j - 1) % 2]
            pv = values(j - 1, p_nxt) if j > 0 else None
            scores(j + 1, s_nxt)
            ms, alphas = softmax(s_cur, p_cur, ms, False)
            if j == 1:
                accs = pv
            elif j > 1:
                accs = [a_prev[hh] * accs[hh] + pv[hh] for hh in range(2)]
            a_prev = alphas
        pv = values(qi - 1, pb_ref) if qi > 0 else None
        ms, alphas = softmax(sa_ref, pa_ref, ms, True)
        acc = values(qi, pa_ref)
        if qi == 1:
            acc = [alphas[hh] * pv[hh] + acc[hh] for hh in range(2)]
        elif qi > 1:
            acc = [alphas[hh] * (a_prev[hh] * accs[hh] + pv[hh]) + acc[hh] for hh in range(2)]
        yt = [acc[hh][0:hd, :] / acc[hh][hd:hd + 1, :] for hh in range(2)]
        o_ref[qi * tq:(qi + 1) * tq, :] = jnp.concatenate(yt, axis=0).T.astype(o_ref.dtype)


def _fox_prompt(q, kt, vt, col, layer, nb, t, tq):
    n = q.shape[0]
    pairs = A_WIDTH // LANES
    seq = pl.BlockSpec((t, LANES), lambda b, hp: (b, hp))
    feat = pl.BlockSpec((None, None, LANES, t), lambda b, hp: (b, layer, hp, 0))
    return pl.pallas_call(
        functools.partial(_fox_kernel, tq=tq, t=t),
        grid=(nb, pairs),
        in_specs=[seq, feat, feat, pl.BlockSpec((t, LANES), lambda b, hp: (b, 0))],
        out_specs=seq,
        out_shape=jax.ShapeDtypeStruct((n, A_WIDTH), BF16),
        scratch_shapes=[pltpu.VMEM((2, V_EXT, t), BF16), pltpu.VMEM((2, t, LANES), BF16),
                        pltpu.VMEM((2, tq, tq), F32), pltpu.VMEM((2, tq, tq), F32),
                        pltpu.VMEM((2, tq, tq), BF16), pltpu.VMEM((2, tq, tq), BF16)],
        compiler_params=_cparams("parallel", "parallel"),
        name="fox_prompt",
    )(q, kt, vt, col)


def _lane_cumsum(y):
    lane = lax.broadcasted_iota(jnp.int32, (1, LANES), 1)
    s = 1
    while s < LANES:
        y = y + jnp.where(lane >= s, pltpu.roll(y, s, axis=1), 0.0)
        s *= 2
    return y


def _fox_sample_kernel(pt_ref, q_ref, kn_ref, vn_ref, col_ref, rown_ref, *rest,
                       n_pages, layer, s_len):
    k_refs = rest[:n_pages]
    v_refs = rest[n_pages:2 * n_pages]
    lf_ref = rest[2 * n_pages]
    o_ref = rest[2 * n_pages + 1]
    s_ref = rest[2 * n_pages + 2]
    b = pl.program_id(0)
    nh = A_HEADS
    rows = nh * s_len
    lane_w = lax.broadcasted_iota(jnp.int32, (1, A_WIDTH), 1)
    rid = lax.broadcasted_iota(jnp.int32, (rows, 1), 0)
    own = jnp.right_shift(lane_w, 6) == jnp.right_shift(rid, 3)
    q = q_ref[...].astype(F32)
    qbd = jnp.where(own, jnp.concatenate([q] * nh, axis=0), 0.0)
    colv = col_ref[...]
    rown = rown_ref[...]
    lo = layer * nh

    lc_all = _lane_cumsum(jnp.concatenate([lf_ref[pt_ref[b, p], lo:lo + nh, :] for p in range(n_pages)], axis=0))
    lcs = [lc_all[p * nh:(p + 1) * nh, :] for p in range(n_pages)]
    tot = [lcs[p][:, LANES - 1:LANES] for p in range(n_pages)]
    base = [None] * n_pages
    suf = jnp.zeros((nh, 1), F32)
    for p in range(n_pages - 1, -1, -1):
        suf = suf + tot[p]
        base[p] = suf

    m_run = [None] * nh
    for p in range(n_pages):
        sp = _dot(qbd, k_refs[p][...])
        for h in range(nh):
            sl = slice(h * s_len, (h + 1) * s_len)
            cn = colv[:, SM_AF + h:SM_AF + h + 1]
            bias = (base[p][h:h + 1, :] - lcs[p][h:h + 1, :]) + cn
            sh = sp[sl, :] + bias
            s_ref[sl, p * LANES:(p + 1) * LANES] = sh
            mx = jnp.max(sh, axis=1, keepdims=True)
            m_run[h] = mx if m_run[h] is None else jnp.maximum(m_run[h], mx)
    sn = lax.dot_general(qbd, kn_ref[...], NT, preferred_element_type=F32)
    r = lax.broadcasted_iota(jnp.int32, (s_len, s_len), 0)
    c = lax.broadcasted_iota(jnp.int32, (s_len, s_len), 1)
    pn, m_all = [], []
    for h in range(nh):
        sl = slice(h * s_len, (h + 1) * s_len)
        cn = colv[:, SM_AF + h:SM_AF + h + 1]
        sh = sn[sl, :] + (cn - rown[SM_AF + h:SM_AF + h + 1, :])
        sh = jnp.where(c <= r, sh, -jnp.inf)
        pn.append(sh)
        m_all.append(jnp.maximum(m_run[h], jnp.max(sh, axis=1, keepdims=True)))
    m = jnp.concatenate(m_all, axis=0)
    p_new = jnp.exp(jnp.concatenate(pn, axis=0) - m)
    l = jnp.sum(p_new, axis=1, keepdims=True)
    acc = _dot(p_new, vn_ref[...])
    for p in range(n_pages):
        pp = jnp.exp(s_ref[:, p * LANES:(p + 1) * LANES] - m)
        l = l + jnp.sum(pp, axis=1, keepdims=True)
        acc = acc + lax.dot_general(pp, v_refs[p][...], NT, preferred_element_type=F32)
    acc = jnp.where(own, acc / l, 0.0)
    y = acc[0:s_len, :]
    for h in range(1, nh):
        y = y + acc[h * s_len:(h + 1) * s_len, :]
    o_ref[...] = y.astype(o_ref.dtype)


def _fox_sample(page_table, q, k_new, v_new, col, rown, cache_k, cache_v, lc, layer, nb, s_len):
    n_pages = page_table.shape[1]
    n_phys, page = cache_k.shape[0], cache_k.shape[1]
    depth = cache_k.shape[2]
    ck = cache_k.transpose(0, 2, 3, 4, 1).reshape(n_phys, depth, A_WIDTH, page)
    cv = cache_v.transpose(0, 2, 3, 4, 1).reshape(n_phys, depth, A_WIDTH, page)
    tok = lambda c: pl.BlockSpec((s_len, c), lambda b, pt: (b, 0))
    page_specs = [pl.BlockSpec((None, None, A_WIDTH, page),
                               lambda b, pt, j=j: (pt[b, j], layer, 0, 0)) for j in range(n_pages)]
    lc_spec = pl.BlockSpec(lc.shape, lambda b, pt: (0, 0, 0), pipeline_mode=pl.Buffered(1))
    grid_spec = pltpu.PrefetchScalarGridSpec(
        num_scalar_prefetch=1,
        grid=(nb,),
        in_specs=[tok(A_WIDTH), tok(A_WIDTH), tok(A_WIDTH), tok(LANES),
                  pl.BlockSpec((None, SM_ROWS, s_len), lambda b, pt: (b, 0, 0))]
                 + page_specs + page_specs + [lc_spec],
        out_specs=tok(A_WIDTH),
        scratch_shapes=[pltpu.VMEM((A_HEADS * s_len, n_pages * LANES), F32)],
    )
    return pl.pallas_call(
        functools.partial(_fox_sample_kernel, n_pages=n_pages, layer=layer, s_len=s_len),
        grid_spec=grid_spec,
        out_shape=jax.ShapeDtypeStruct((nb * s_len, A_WIDTH), BF16),
        compiler_params=_cparams("arbitrary"),
        name="fox_sample",
    )(page_table, q, k_new, v_new, col, rown, *([ck] * n_pages), *([cv] * n_pages), lc)


def _mlstm_kernel(q_ref, k_ref, v_ref, so_ref, col_ref, row_ref, g_ref, c0_ref, n0_ref, m0_ref, _,
                  y_ref, c1_ref, n1_ref, m1_ref, *, t, bb, group):
    nh, hd = M_HEADS, M_HEAD_DIM
    r = lax.broadcasted_iota(jnp.int32, (t, t), 0)
    c = lax.broadcasted_iota(jnp.int32, (t, t), 1)
    causal = c <= r
    g = g_ref[...]
    hs = lambda h: slice(h * hd, (h + 1) * hd)
    last = slice(t - 1, t)
    units = [(u, h) for u in range(group) for h in range(nh)]

    def group_body(gi, _):
        bis = [gi * group + u for u in range(group)]
        r0s = [pl.multiple_of(bi * t, t) for bi in bis]
        qa = [q_ref[pl.ds(r0, t), :].astype(F32) for r0 in r0s]
        ka = [k_ref[pl.ds(r0, t), :].astype(F32) for r0 in r0s]
        va = [v_ref[pl.ds(r0, t), :].astype(F32) for r0 in r0s]
        colv = [col_ref[pl.ds(r0, t), :] for r0 in r0s]
        rows = [row_ref[bi] for bi in bis]
        cs = {(u, h): c0_ref[bis[u], h] for u, h in units}
        ns = {(u, h): n0_ref[bis[u], h] for u, h in units}
        ms = {(u, h): m0_ref[bis[u], h] for u, h in units}
        qk = {(u, h): lax.dot_general(qa[u][:, hs(h)], ka[u][:, hs(h)], NT, preferred_element_type=F32)
              for u, h in units}
        qc = {(u, h): lax.dot_general(qa[u][:, hs(h)], cs[u, h], NT, preferred_element_type=F32) for u, h in units}
        wqk, a_, m_t_, wl_ = {}, {}, {}, {}
        for u, h in units:
            f_col = colv[u][:, SM_MF + h:SM_MF + h + 1]
            i_col = colv[u][:, SM_MI + h:SM_MI + h + 1]
            g_max = colv[u][:, SM_MG + h:SM_MG + h + 1]
            f_row = rows[u][SM_MF + h:SM_MF + h + 1, :]
            i_row = rows[u][SM_MI + h:SM_MI + h + 1, :]
            mx = jnp.maximum(ms[u, h], g_max)
            m_t = f_col + mx
            d = jnp.where(causal, f_col + (i_row - f_row), -jnp.inf)
            wqk[u, h] = jnp.exp(d - m_t) * qk[u, h]
            a_[u, h] = jnp.exp(ms[u, h] - mx)
            m_t_[u, h] = m_t
            wl_[u, h] = jnp.exp((i_col - f_col) + (f_col[last, :] - m_t[last, :]))
        pv = {(u, h): _dot(wqk[u, h], va[u][:, hs(h)]) for u, h in units}
        ys = {}
        for u, h in units:
            a, m_t, w_last = a_[u, h], m_t_[u, h], wl_[u, h]
            num = a * qc[u, h] + pv[u, h]
            den = (a * jnp.sum(qa[u][:, hs(h)] * ns[u, h], axis=1, keepdims=True)
                   + jnp.sum(wqk[u, h], axis=1, keepdims=True))
            h_t = num / jnp.maximum(jnp.abs(den), jnp.exp(-m_t))
            ys[u, h] = h_t * lax.rsqrt(jnp.mean(h_t * h_t, axis=1, keepdims=True) + EPS)
            a_last = a[last, :]
            vw = va[u][:, hs(h)] * w_last
            c1_ref[bis[u], h] = a_last * cs[u, h] + _dot(vw.T, ka[u][:, hs(h)])
            n1_ref[bis[u], h] = a_last * ns[u, h] + jnp.sum(ka[u][:, hs(h)] * w_last, axis=0, keepdims=True)
            m1_ref[bis[u], h] = m_t[last, :]
        for u in range(group):
            hn = jnp.concatenate([ys[u, h] for h in range(nh)], axis=1) * g
            y_ref[pl.ds(r0s[u], t), :] = (so_ref[pl.ds(r0s[u], t), :] * hn).astype(y_ref.dtype)
        return 0

    lax.fori_loop(0, bb // group, group_body, 0)


def _mlstm(q, k, v, so, col, rowt, g, c0, n0, m0, layer_in, c_prev, layer_out, depth, nb, t, bb):
    n = q.shape[0]
    nh, hd = M_HEADS, M_HEAD_DIM
    seq = lambda w: pl.BlockSpec((bb * t, w), lambda i: (i, 0))
    st_in = lambda a, b_: pl.BlockSpec((bb, None, nh, a, b_), lambda i: (i, layer_in, 0, 0, 0))
    st_out = lambda a, b_: pl.BlockSpec((bb, nh, a, b_), lambda i: (i, 0, 0, 0))
    if c_prev is None:
        c_prev = jnp.zeros((SUBLANES, LANES), F32)
        aliases = {}
    else:
        aliases = {10: 1}
    return pl.pallas_call(
        functools.partial(_mlstm_kernel, t=t, bb=bb, group=2 if bb % 2 == 0 else 1),
        grid=(nb // bb,),
        in_specs=[seq(M_WIDTH), seq(M_WIDTH), seq(M_WIDTH), seq(M_WIDTH), seq(LANES),
                  pl.BlockSpec((bb, SM_ROWS, t), lambda i: (i, 0, 0)),
                  _const_spec((1, M_WIDTH)),
                  st_in(hd, hd), st_in(1, hd), st_in(1, 1),
                  pl.BlockSpec(memory_space=pl.ANY)],
        out_specs=[seq(M_WIDTH),
                   pl.BlockSpec((bb, None, nh, hd, hd), lambda i: (i, layer_out, 0, 0, 0)),
                   st_out(1, hd), st_out(1, 1)],
        out_shape=[jax.ShapeDtypeStruct((n, M_WIDTH), BF16),
                   jax.ShapeDtypeStruct((nb, depth, nh, hd, hd), F32),
                   jax.ShapeDtypeStruct((nb, nh, 1, hd), F32),
                   jax.ShapeDtypeStruct((nb, nh, 1, 1), F32)],
        input_output_aliases=aliases,
        compiler_params=_cparams("parallel"),
        name="mlstm",
    )(q, k, v, so, col, rowt, g, c0, n0.reshape(n0.shape[:3] + (1, hd)), m0.reshape(m0.shape[:3] + (1, 1)), c_prev)


C_EXT = M_HEAD_DIM + 16


def _mlstm_t_kernel(q_ref, k_ref, vt_ref, so_ref, col_ref, row_ref, g_ref, _,
                    y_ref, c1_ref, n1_ref, m1_ref, *, t, chunk):
    nh, hd = M_HEADS, M_HEAD_DIM
    s_idx = lax.broadcasted_iota(jnp.int32, (chunk, chunk), 0)
    l_idx = lax.broadcasted_iota(jnp.int32, (chunk, chunk), 1)
    causal = s_idx <= l_idx
    heads = range(nh)
    hs = lambda h: slice(h * hd, (h + 1) * hd)
    last = slice(chunk - 1, chunk)
    pad_rows = jnp.zeros((C_EXT - hd - 1, chunk), F32)

    def chunk_body(ci, carry):
        cs, ms = carry
        r0 = pl.multiple_of(ci * chunk, chunk)
        qa = q_ref[pl.ds(r0, chunk), :]
        ka = k_ref[pl.ds(r0, chunk), :]
        vta = vt_ref[:, pl.ds(r0, chunk)]
        colv = col_ref[pl.ds(r0, chunk), :]
        rows = row_ref[:, pl.ds(r0, chunk)]
        qk = [lax.dot_general(ka[:, hs(h)], qa[:, hs(h)], NT, preferred_element_type=F32) for h in heads]
        cq = [lax.dot_general(cs[h].astype(BF16), qa[:, hs(h)], NT, preferred_element_type=F32) for h in heads]
        wqk, a_, m_t_, wl_ = [], [], [], []
        for h in heads:
            f_row = rows[SM_MF + h:SM_MF + h + 1, :]
            i_row = rows[SM_MI + h:SM_MI + h + 1, :]
            g_max = rows[SM_MG + h:SM_MG + h + 1, :]
            g_col = colv[:, SM_MI + h:SM_MI + h + 1] - colv[:, SM_MF + h:SM_MF + h + 1]
            mx = jnp.maximum(ms[h], g_max)
            m_t = f_row + mx
            d = jnp.where(causal, f_row + g_col, -jnp.inf)
            wqk.append(jnp.exp(d - m_t) * qk[h])
            a_.append(jnp.exp(ms[h] - mx))
            m_t_.append(m_t)
            wl_.append(jnp.exp((i_row - f_row) + (f_row[:, last] - m_t[:, last])))
        pv = [_dot(vta[hs(h), :], wqk[h].astype(BF16)) for h in heads]
        ys, cs_new, ms_new = [], [], []
        for h in heads:
            a, m_t = a_[h], m_t_[h]
            num = a * cq[h][0:hd, :] + pv[h]
            den = a * cq[h][hd:hd + 1, :] + jnp.sum(wqk[h], axis=0, keepdims=True)
            h_t = num / jnp.maximum(jnp.abs(den), jnp.exp(-m_t))
            ys.append((h_t * lax.rsqrt(jnp.mean(h_t * h_t, axis=0, keepdims=True) + EPS)).T)
            vw = jnp.concatenate([vta[hs(h), :].astype(F32) * wl_[h], wl_[h], pad_rows], axis=0)
            cs_new.append(a[:, last] * cs[h] + _dot(vw.astype(BF16), ka[:, hs(h)]))
            ms_new.append(m_t[:, last])
        hn = jnp.concatenate(ys, axis=1) * g_ref[...]
        y_ref[pl.ds(r0, chunk), :] = (so_ref[pl.ds(r0, chunk), :] * hn).astype(y_ref.dtype)
        return cs_new, ms_new

    init = ([jnp.zeros((C_EXT, hd), F32)] * nh, [jnp.zeros((1, 1), F32)] * nh)
    cs, ms = lax.fori_loop(0, t // chunk, chunk_body, init, unroll=True)
    for h in heads:
        c1_ref[h] = cs[h][0:hd, :]
        n1_ref[h] = cs[h][hd:hd + 1, :]
        m1_ref[h] = ms[h]


def _mlstm_t(q, k, vt, so, col, rowt, g, c_prev, layer_out, depth, nb, t, chunk):
    n = q.shape[0]
    nh, hd = M_HEADS, M_HEAD_DIM
    seq = lambda w: pl.BlockSpec((t, w), lambda i: (i, 0))
    st_out = lambda a, b_: pl.BlockSpec((None, nh, a, b_), lambda i: (i, 0, 0, 0))
    if c_prev is None:
        c_prev = jnp.zeros((SUBLANES, LANES), F32)
        aliases = {}
    else:
        aliases = {7: 1}
    return pl.pallas_call(
        functools.partial(_mlstm_t_kernel, t=t, chunk=chunk),
        grid=(nb,),
        in_specs=[seq(M_WIDTH), seq(M_WIDTH), pl.BlockSpec((None, M_WIDTH, t), lambda i: (i, 0, 0)),
                  seq(M_WIDTH), seq(LANES), pl.BlockSpec((None, SM_ROWS, t), lambda i: (i, 0, 0)),
                  _const_spec((1, M_WIDTH)), pl.BlockSpec(memory_space=pl.ANY)],
        out_specs=[seq(M_WIDTH),
                   pl.BlockSpec((None, None, nh, hd, hd), lambda i: (i, layer_out, 0, 0, 0)),
                   st_out(1, hd), st_out(1, 1)],
        out_shape=[jax.ShapeDtypeStruct((n, M_WIDTH), BF16),
                   jax.ShapeDtypeStruct((nb, depth, nh, hd, hd), F32),
                   jax.ShapeDtypeStruct((nb, nh, 1, hd), F32),
                   jax.ShapeDtypeStruct((nb, nh, 1, 1), F32)],
        input_output_aliases=aliases,
        compiler_params=_cparams("parallel"),
        name="mlstm_t",
    )(q, k, vt, so, col, rowt, g, c_prev)


def _conv_carry_init(state_ref, carry_ref, tiles_per_seq):
    @pl.when((pl.program_id(0) % tiles_per_seq) == 0)
    def _():
        carry_ref[...] = state_ref[...]


def _conv_long(u, w_ref, c0, carry_ref):
    tm, ch = u.shape
    prev = carry_ref[:, c0:c0 + ch]
    row = lax.broadcasted_iota(jnp.int32, (tm, 1), 0)
    p1 = jnp.where(row == 0, prev[1:2, :], pltpu.roll(u, 1, axis=0))
    p2 = jnp.where(row == 0, prev[0:1, :], jnp.where(row == 1, prev[1:2, :], pltpu.roll(u, 2, axis=0)))
    carry_ref[:, c0:c0 + ch] = u[tm - 2:tm, :]
    return p2 * w_ref[0:1, c0:c0 + ch] + p1 * w_ref[1:2, c0:c0 + ch] + u * w_ref[2:3, c0:c0 + ch]


def _conv_short(u, w_ref, c0, fill_ref, s_len):
    tm, ch = u.shape
    fill = fill_ref[:, c0:c0 + ch]
    rmod = lax.broadcasted_iota(jnp.int32, (tm, 1), 0) & (s_len - 1)
    p1 = jnp.where(rmod >= 1, pltpu.roll(u, 1, axis=0), pltpu.roll(fill, tm - 1, axis=0))
    p2 = jnp.where(rmod >= 2, pltpu.roll(u, 2, axis=0), fill)
    return p2 * w_ref[0:1, c0:c0 + ch] + p1 * w_ref[1:2, c0:c0 + ch] + u * w_ref[2:3, c0:c0 + ch]


def _merge_kernel(x_ref, g_ref, wc_ref, bc_ref, wg_ref, bg_ref, ws_ref, ya_ref, ym_ref,
                  woa_ref, wom_ref, woc_ref, wo_ref, st_ref, x1_ref, cst_ref, *scratch,
                  long_seq, tiles_per_seq, s_len):
    x = x_ref[...]
    h = _rms(x, g_ref[...]).astype(BF16)
    cw = C_WIDTH
    cb = _dot(h, wc_ref[:, 0:cw]) + bc_ref[:, 0:cw]
    u = (_dot(h, wc_ref[:, cw:2 * cw]) + bc_ref[:, cw:2 * cw]) * (_dot(h, wc_ref[:, 2 * cw:3 * cw]) + bc_ref[:, 2 * cw:3 * cw])
    if long_seq:
        _conv_carry_init(st_ref, scratch[0], tiles_per_seq)
        uc = _conv_long(u, ws_ref, 0, scratch[0])
        cst_ref[...] = u[u.shape[0] - 2:, :]
    else:
        uc = _conv_short(u, ws_ref, 0, st_ref, s_len)
        cst_ref[...] = u
    yc = (cb * uc).astype(BF16)
    d = D_MODEL

    def gate(i):
        return _sigmoid(_dot(h, wg_ref[:, i * d:(i + 1) * d]) + bg_ref[:, i * d:(i + 1) * d])

    merged = gate(0) * _dot(ya_ref[...], woa_ref[...])
    merged = merged + gate(1) * _dot(ym_ref[...], wom_ref[...])
    merged = merged + gate(2) * _dot(yc, woc_ref[...])
    x1_ref[...] = x + _dot(merged.astype(BF16), wo_ref[...])


def _merge(x, g, wc, bc, wg, bg, ws, ya, ym, woa, wom, woc, wo, st, nb, t, tm):
    n = x.shape[0]
    long_seq = t >= tm
    tiles_per_seq = max(t // tm, 1)
    row = lambda c: pl.BlockSpec((tm, c), lambda i: (i, 0))
    if long_seq:
        st_spec = pl.BlockSpec((None, 2, C_WIDTH), lambda i: (i // tiles_per_seq, 0, 0))
        cst_spec = pl.BlockSpec((None, 2, C_WIDTH), lambda i: (i // tiles_per_seq, 0, 0))
        cst_shape = jax.ShapeDtypeStruct((nb, 2, C_WIDTH), F32)
        scratch = [pltpu.VMEM((2, C_WIDTH), F32)]
    else:
        st_spec = row(C_WIDTH)
        cst_spec = row(C_WIDTH)
        cst_shape = jax.ShapeDtypeStruct((n, C_WIDTH), F32)
        scratch = []
    return pl.pallas_call(
        functools.partial(_merge_kernel, long_seq=long_seq, tiles_per_seq=tiles_per_seq, s_len=t),
        grid=(n // tm,),
        in_specs=[row(D_MODEL), _const_spec((1, D_MODEL)),
                  _const_spec(wc.shape), _const_spec(bc.shape), _const_spec(wg.shape), _const_spec(bg.shape),
                  _const_spec(ws.shape), row(A_WIDTH), row(M_WIDTH),
                  _const_spec(woa.shape), _const_spec(wom.shape), _const_spec(woc.shape), _const_spec(wo.shape),
                  st_spec],
        out_specs=[row(D_MODEL), cst_spec],
        out_shape=[jax.ShapeDtypeStruct((n, D_MODEL), F32), cst_shape],
        scratch_shapes=scratch,
        compiler_params=_cparams("arbitrary"),
        name="merge",
    )(x, g, wc, bc, wg, bg, ws, ya, ym, woa, wom, woc, wo, st)


FF_CHUNK = D_FF // 2
FFN_SUB_ROWS = 256


def _ffn_kernel(x_ref, pe_ref, gf_ref, wup_ref, wfc_ref, bfc_ref, wdn_ref, gp_ref, wpg_ref, wpp_ref, gfin_ref,
                st_ref, xo_ref, fst_ref, *scratch, long_seq, tiles_per_seq, s_len, final, sub):
    if long_seq:
        _conv_carry_init(st_ref, scratch[0], tiles_per_seq)
    for r0 in range(0, x_ref.shape[0], sub):
        rs = slice(r0, r0 + sub)
        x = x_ref[rs, :]
        h2 = _rms(x, gf_ref[...]).astype(BF16)
        acc = jnp.zeros((sub, D_MODEL), F32)
        for half in range(D_FF // FF_CHUNK):
            c0 = half * FF_CHUNK
            ua = _dot(h2, wup_ref[:, c0:c0 + FF_CHUNK])
            ub = _dot(h2, wup_ref[:, D_FF + c0:D_FF + c0 + FF_CHUNK])
            if long_seq:
                uac = _conv_long(ua, wfc_ref, c0, scratch[0])
                fst_ref[:, c0:c0 + FF_CHUNK] = ua[sub - 2:, :]
            else:
                uac = _conv_short(ua, wfc_ref, c0, st_ref.at[rs, :], s_len)
                fst_ref[rs, c0:c0 + FF_CHUNK] = ua
            z = uac + bfc_ref[:, c0:c0 + FF_CHUNK]
            act = (z * _sigmoid(z) * ub).astype(BF16)
            acc = acc + _dot(act, wdn_ref[c0:c0 + FF_CHUNK, :])
        x2 = x + acc
        gate = _sigmoid(_dot(_rms(x2, gp_ref[...]).astype(BF16), wpg_ref[...]))
        x3 = x2 + gate * _dot(pe_ref[rs, :].astype(BF16), wpp_ref[...])
        if final:
            x3 = _rms(x3, gfin_ref[...])
        xo_ref[rs, :] = x3


def _ffn(x, pe, layer, gf, wup, wfc, bfc, wdn, gp, wpg, wpp, gfin, st, nb, t, tm, final):
    n = x.shape[0]
    long_seq = t >= tm
    tiles_per_seq = max(t // tm, 1)
    row = lambda c: pl.BlockSpec((tm, c), lambda i: (i, 0))
    if long_seq:
        st_spec = pl.BlockSpec((None, 2, D_FF), lambda i: (i // tiles_per_seq, 0, 0))
        fst_spec = pl.BlockSpec((None, 2, D_FF), lambda i: (i // tiles_per_seq, 0, 0))
        fst_shape = jax.ShapeDtypeStruct((nb, 2, D_FF), F32)
        scratch = [pltpu.VMEM((2, D_FF), F32)]
    else:
        st_spec = row(D_FF)
        fst_spec = row(D_FF)
        fst_shape = jax.ShapeDtypeStruct((n, D_FF), F32)
        scratch = []
    return pl.pallas_call(
        functools.partial(_ffn_kernel, long_seq=long_seq, tiles_per_seq=tiles_per_seq, s_len=t, final=final,
                          sub=min(tm, FFN_SUB_ROWS)),
        grid=(n // tm,),
        in_specs=[row(D_MODEL), pl.BlockSpec((None, tm, pe.shape[2]), lambda i: (layer, i, 0)),
                  _const_spec((1, D_MODEL)),
                  _const_spec(wup.shape), _const_spec(wfc.shape), _const_spec(bfc.shape), _const_spec(wdn.shape),
                  _const_spec((1, D_MODEL)), _const_spec(wpg.shape), _const_spec(wpp.shape),
                  _const_spec((1, D_MODEL)), st_spec],
        out_specs=[row(D_MODEL), fst_spec],
        out_shape=[jax.ShapeDtypeStruct((n, D_MODEL), F32), fst_shape],
        scratch_shapes=scratch,
        compiler_params=_cparams("arbitrary"),
        name="ffn",
    )(x, pe, gf, wup, wfc, bfc, wdn, gp, wpg, wpp, gfin, st)


def _row_form(col, nb, t):
    return col[:, :SM_ROWS].reshape(nb, t, SM_ROWS).transpose(0, 2, 1)


def _short_fill(state, s_len):
    nb, _, ch = state.shape
    return jnp.pad(state, ((0, 0), (0, s_len - 2), (0, 0))).reshape(nb * s_len, ch)


def _pick(n, pref):
    return pref if n % pref == 0 else n


def _layer(x, pe, w, nb, t, chunk, past, final, carry):
    n = x.shape[0]
    layer, depth = carry["layer"], carry["depth"]
    tm_in = _pick(n, 512)
    tm_merge = _pick(n, 512)
    tm_ffn = _pick(n, 2 * FFN_SUB_ROWS)
    if past is None:
        kvp = dict(layer=layer, depth=depth, prev=carry["kvt"], nb=nb, t=t, wt=w["w_kvt"], bt=w["b_kvt"])
        q, mq, mk, so, sm, kt, vt, mvt = _inproj(x, w["g_mix"], w["w_am"], w["b_am"], tm_in, kvp)
        col, logf = _gates(sm, t, t, chunk)
        rowt = _row_form(col, nb, t)
        ya = _fox_prompt(q, kt, vt, col, layer, nb, t, _pick(t, 512))
        k, v = kt, vt
        ym, c1, n1, m1 = _mlstm_t(mq, mk, mvt, so, col, rowt, w["g_mhead"], carry["c"], layer, depth, nb, t, chunk)
        conv0 = jnp.zeros((nb, 2, C_WIDTH), F32)
        ffn0 = jnp.zeros((nb, 2, D_FF), F32)
    else:
        q, k, v, mq, mk, mv, so, sm = _inproj(x, w["g_mix"], w["w_am"], w["b_am"], tm_in)
        col, logf = _gates(sm, n, t, t)
        rowt = _row_form(col, nb, t)
        ya = _fox_sample(past["page_table"], q, k, v, col, rowt, past["cache_k"], past["cache_v"],
                         past["lc"], layer, nb, t)
        conv0, ffn0 = past["conv0"], past["ffn0"]
        ym, c1, n1, m1 = _mlstm(mq, mk, mv, so, col, rowt, w["g_mhead"], past["c0"], past["n0"], past["m0"], layer,
                                carry["c"], layer, depth, nb, t, _pick(nb, SUBLANES))
    long_merge = t >= tm_merge
    long_ffn = t >= tm_ffn
    x1, conv1 = _merge(x, w["g_mix"], w["w_c"], w["b_c"], w["w_g"], w["b_g"], w["w_sconv"], ya, ym,
                       w["w_oa"], w["w_om"], w["w_oc"], w["w_o"],
                       conv0 if long_merge else _short_fill(conv0, t), nb, t, tm_merge)
    x3, ffn1 = _ffn(x1, pe, layer, w["g_ffn"], w["w_up"], w["w_fconv"], w["b_fconv"], w["w_down"],
                    w["g_ple"], w["w_ple_gate"], w["w_ple"], w["g_final"],
                    ffn0 if long_ffn else _short_fill(ffn0, t), nb, t, tm_ffn, final)
    if not long_merge:
        conv1 = conv1.reshape(nb, t, C_WIDTH)[:, t - 2:, :]
    if not long_ffn:
        ffn1 = ffn1.reshape(nb, t, D_FF)[:, t - 2:, :]
    if past is not None:
        k = k.reshape(nb, t, A_HEADS, A_HEAD_DIM)
        v = v.reshape(nb, t, A_HEADS, A_HEAD_DIM)
    state = (k, v, logf.reshape(nb, t, A_HEADS), c1, n1.reshape(nb, M_HEADS, M_HEAD_DIM),
             m1.reshape(nb, M_HEADS), conv1, ffn1)
    return x3, state


def _layer_weights(l, w_in, b_in, g_mix, g_mhead, w_sconv, w_oa, w_om, w_oc, w_o, g_ffn, w_up,
                   w_fconv, b_fconv, w_down, g_ple, w_ple_gate, w_ple, g_final):
    a0 = 0
    af0 = 3 * A_WIDTH
    m0 = af0 + A_HEADS
    mi0 = m0 + 3 * M_WIDTH
    mo0 = mi0 + 2 * M_HEADS
    c0 = mo0 + M_WIDTH
    g0 = c0 + 3 * C_WIDTH
    wl, bl = w_in[l], b_in[l]
    pad = LANES - SM_USED

    def cols(a, lo, hi):
        return a[..., lo:hi]

    def am(a):
        return jnp.concatenate(
            [cols(a, a0, af0), cols(a, m0, mi0), cols(a, mo0, c0),
             cols(a, af0, m0), cols(a, mi0, mo0), jnp.zeros(a.shape[:-1] + (pad,), a.dtype)], axis=-1)

    row = lambda a: a.reshape(1, -1)
    return {
        "w_am": am(wl).astype(BF16), "b_am": row(am(bl)),
        "w_kvt": jnp.concatenate([cols(wl, A_WIDTH, af0), cols(wl, m0 + 2 * M_WIDTH, mi0)], axis=-1).T.astype(BF16),
        "b_kvt": jnp.concatenate([cols(bl, A_WIDTH, af0), cols(bl, m0 + 2 * M_WIDTH, mi0)], axis=-1).reshape(-1, 1),
        "w_c": cols(wl, c0, g0).astype(BF16), "b_c": row(cols(bl, c0, g0)),
        "w_g": cols(wl, g0, g0 + 3 * D_MODEL).astype(BF16), "b_g": row(cols(bl, g0, g0 + 3 * D_MODEL)),
        "g_mix": row(g_mix[l]), "g_mhead": row(g_mhead[l]), "w_sconv": w_sconv[l],
        "w_oa": w_oa[l].astype(BF16), "w_om": w_om[l].astype(BF16), "w_oc": w_oc[l].astype(BF16),
        "w_o": w_o[l].astype(BF16), "g_ffn": row(g_ffn[l]), "w_up": w_up[l].astype(BF16),
        "w_fconv": w_fconv[l], "b_fconv": row(b_fconv[l]), "w_down": w_down[l].astype(BF16),
        "g_ple": row(g_ple[l]), "w_ple_gate": w_ple_gate[l].astype(BF16), "w_ple": w_ple[l].astype(BF16),
        "g_final": row(g_final),
    }


def kernel(x_prompt, x_sample, cache_k, cache_v, cache_logf, state_mlstm_C, state_mlstm_n, state_mlstm_m, state_conv, state_ffn_conv, page_table, p_prompt, p_sample, w_in, b_in, g_mix, g_mhead, w_sconv, w_oa, w_om, w_oc, w_o, g_ffn, w_up, w_fconv, b_fconv, w_down, g_ple, w_ple_gate, w_ple, g_final):
    nbp, tp, _ = x_prompt.shape
    nbs, ts, _ = x_sample.shape
    depth = w_in.shape[0]
    n_phys, page = cache_logf.shape[0], cache_logf.shape[1]
    assert page == LANES and ts == SUBLANES and depth * A_HEADS == SM_USED

    lc = cache_logf.reshape(n_phys, page, depth * A_HEADS).transpose(0, 2, 1)

    xp = x_prompt.reshape(nbp * tp, D_MODEL)
    xs = x_sample.reshape(nbs * ts, D_MODEL)
    chunk_p = _pick(tp, 128)
    new_p, new_s = [], []
    kvt = c_p = c_s = None
    for l in range(depth):
        w = _layer_weights(l, w_in, b_in, g_mix, g_mhead, w_sconv, w_oa, w_om, w_oc, w_o, g_ffn, w_up,
                           w_fconv, b_fconv, w_down, g_ple, w_ple_gate, w_ple, g_final)
        final = l == depth - 1
        xp, st_p = _layer(xp, p_prompt.reshape(depth, nbp * tp, -1), w, nbp, tp, chunk_p, None, final,
                          {"layer": l, "depth": depth, "kvt": kvt, "c": c_p})
        kvt, c_p = st_p[:2], st_p[3]
        past = {"page_table": page_table, "cache_k": cache_k, "cache_v": cache_v, "lc": lc,
                "c0": state_mlstm_C, "n0": state_mlstm_n, "m0": state_mlstm_m,
                "conv0": state_conv[:, l], "ffn0": state_ffn_conv[:, l]}
        xs, st_s = _layer(xs, p_sample.reshape(depth, nbs * ts, -1), w, nbs, ts, ts, past, final,
                          {"layer": l, "depth": depth, "kvt": None, "c": c_s})
        c_s = st_s[3]
        new_p.append(st_p)
        new_s.append(st_s)

    def stack(per_layer, c_all, first):
        rows = [jnp.stack([s[i] for s in per_layer], axis=2) for i in range(first, 3)]
        states = [jnp.stack([s[i] for s in per_layer], axis=1) for i in range(4, 8)]
        return tuple(rows + [c_all] + states)

    kv_p = tuple(a.reshape(nbp, depth, A_HEADS, A_HEAD_DIM, tp).transpose(0, 4, 1, 2, 3) for a in kvt)
    return ((xp.reshape(nbp, tp, D_MODEL), xs.reshape(nbs, ts, D_MODEL)) + kv_p + stack(new_p, c_p, 2)
            + stack(new_s, c_s, 0))
```
